```python
import math
import jax
import jax.numpy as jnp
from jax import lax
import numpy as np

D_MODEL = 1024
BATCH = 16
SEQ = 2048
DEPTH = 1

MEM_LEN = 256
MIX_WIDTH = 2 * D_MODEL
SSD_WIDTH = MIX_WIDTH // 2
SSD_HEAD_DIM = 64
SSD_HEADS = SSD_WIDTH // SSD_HEAD_DIM
SSD_STATE = 128
SSD_GROUPS = 4
SSD_CONV = 5
SSD_CHUNK = 128
DT_MIN = 0.001
DT_MAX = 0.1
ATT_WIDTH = MIX_WIDTH - SSD_WIDTH
ATT_HEAD_DIM = 64
ATT_HEADS = ATT_WIDTH // ATT_HEAD_DIM
DILATED_PATTERNS = ((128, 1), (512, 4), (2048, 16))
WIN_BLOCK = 64
MEM_HEADS = 4
MEM_WIDTH = D_MODEL
MEM_HEAD_DIM = MEM_WIDTH // MEM_HEADS
N_EXPERTS = 256
ROUTE_GROUPS = 8
TOPK_GROUPS = 4
TOP_K = 8
EXPERT_DIM = 256
SHARED_DIM = 256
ROUTED_SCALE = 2.5
MOE_BLOCK = 256
DN_ALPHA = (2.0 * DEPTH) ** 0.25
DN_BETA = (8.0 * DEPTH) ** -0.25
LN_EPS = 1e-5
RMS_EPS = 1e-5
NEG_INF = -1e30
GN = SSD_GROUPS * SSD_STATE
CONV_CH = SSD_WIDTH + 2 * GN
Z_OFF = 0
XBC_OFF = Z_OFF + SSD_WIDTH
DT_OFF = XBC_OFF + CONV_CH
Q_OFF = DT_OFF + 2 * SSD_HEADS
K_OFF = Q_OFF + ATT_WIDTH
V_OFF = K_OFF + ATT_WIDTH
IN_COLS = V_OFF + ATT_WIDTH

kernel_name = 'hybrid_ssd_dilated_moe_encoder_layer'


def _layer_norm(x, g, b):
    xf = x.astype(jnp.float32)
    mu = jnp.mean(xf, -1, keepdims=True)
    var = jnp.mean(jnp.square(xf - mu), -1, keepdims=True)
    return ((xf - mu) * lax.rsqrt(var + LN_EPS) * g + b).astype(x.dtype)


def _rms_norm(y, w):
    y = y.astype(jnp.float32)
    return y * lax.rsqrt(jnp.mean(jnp.square(y), -1, keepdims=True) + RMS_EPS) * w


def _alibi_slopes():
    return np.array([2.0 ** (-8.0 * (h + 1) / ATT_HEADS) for h in range(ATT_HEADS)], dtype=np.float32)


def _centred_depthwise_conv(u, w, b):
    c = u.shape[-1]
    y = lax.conv_general_dilated(u, w.astype(u.dtype)[:, None, :], window_strides=(1,),
                                 padding=[(SSD_CONV // 2, SSD_CONV // 2)],
                                 dimension_numbers=('NWC', 'WIO', 'NWC'), feature_group_count=c)
    return y + b.astype(u.dtype)


def _ssd_scan(xh, dt, a, bm, cm):
    bsz, s, h, p = xh.shape
    g, n = bm.shape[2], bm.shape[3]
    r = h // g
    q = SSD_CHUNK
    c = s // q
    x = (xh * dt[..., None]).reshape(bsz, c, q, g, r, p)
    cs = jnp.cumsum((dt * a).reshape(bsz, c, q, g, r), axis=2)
    bmc = bm.reshape(bsz, c, q, g, n)
    cmc = cm.reshape(bsz, c, q, g, n)
    seg = cs[:, :, :, None] - cs[:, :, None, :]
    mask = jnp.tril(jnp.ones((q, q), dtype=bool))[:, :, None, None]
    decay = jnp.exp(jnp.where(mask, seg, NEG_INF))
    cb = jnp.einsum('bcqgn,bckgn->bcqkg', cmc, bmc)
    y_diag = jnp.einsum('bcqkg,bcqkgr,bckgrp->bcqgrp', cb, decay, x)
    decay_s = jnp.exp(cs[:, :, -1:] - cs)
    states = jnp.einsum('bckgn,bckgr,bckgrp->bcgrpn', bmc, decay_s, x)
    chunk_decay = jnp.exp(cs[:, :, -1])

    def step(hs, inp):
        st, dec = inp
        return hs * dec[..., None, None] + st, hs

    h0 = jnp.zeros((bsz, g, r, p, n), jnp.float32)
    _, prev = lax.scan(step, h0, (jnp.moveaxis(states, 1, 0), jnp.moveaxis(chunk_decay, 1, 0)))
    prev = jnp.moveaxis(prev, 0, 1)
    y_off = jnp.einsum('bcqgn,bcgrpn,bcqgr->bcqgrp', cmc, prev, jnp.exp(cs))
    return (y_diag + y_off).reshape(bsz, s, h, p)


def _ssd_group(z, xbc, dt_raw, conv_w, conv_b, dt_bias_f, dt_bias_b, a_log_f, a_log_b, d_skip, norm_w):
    bsz, s, _ = z.shape
    xbc = jax.nn.silu(_centred_depthwise_conv(xbc, conv_w, conv_b)).astype(jnp.float32)
    xh = xbc[..., :SSD_WIDTH].reshape(bsz, s, SSD_HEADS, SSD_HEAD_DIM)
    bm = xbc[..., SSD_WIDTH:SSD_WIDTH + GN].reshape(bsz, s, SSD_GROUPS, SSD_STATE)
    cm = xbc[..., SSD_WIDTH + GN:].reshape(bsz, s, SSD_GROUPS, SSD_STATE)
    dt_raw = dt_raw.astype(jnp.float32)
    dt_f = jax.nn.softplus(dt_raw[..., :SSD_HEADS] + dt_bias_f.astype(jnp.float32))
    dt_b = jax.nn.softplus(dt_raw[..., SSD_HEADS:] + dt_bias_b.astype(jnp.float32))
    a_f = -jnp.exp(a_log_f.astype(jnp.float32))
    a_b = -jnp.exp(a_log_b.astype(jnp.float32))
    flip = lambda t: jnp.flip(t, axis=1)
    y_f = _ssd_scan(xh, dt_f, a_f, bm, cm)
    y_b = flip(_ssd_scan(flip(xh), flip(dt_b), a_b, flip(bm), flip(cm)))
    y = (y_f + y_b + d_skip.astype(jnp.float32)[:, None] * xh).reshape(bsz, s, SSD_WIDTH)
    return _rms_norm(y * jax.nn.silu(z.astype(jnp.float32)), norm_w)


def _dilated_branch(q, k, v, slopes, dilation, radius):
    bsz, s, h, e = q.shape
    w = WIN_BLOCK
    l = s // dilation
    nb = -(-l // w)
    lp = nb * w

    def to_res(t):
        t = t.reshape(bsz, l, dilation, h, e).transpose(0, 2, 1, 3, 4)
        return jnp.pad(t, ((0, 0), (0, 0), (0, lp - l), (0, 0), (0, 0)))

    def neighbours(t):
        tp = jnp.pad(t, ((0, 0), (0, 0), (w, w), (0, 0), (0, 0)))
        return jnp.concatenate([tp[:, :, o:o + lp].reshape(bsz, dilation, nb, w, h, e)
                                for o in (0, w, 2 * w)], axis=3)

    qb = to_res(q).reshape(bsz, dilation, nb, w, h, e)
    kb = neighbours(to_res(k))
    vb = neighbours(to_res(v))
    qi = jnp.arange(nb)[:, None] * w + jnp.arange(w)[None, :]
    kj = (jnp.arange(nb)[:, None] - 1) * w + jnp.arange(3 * w)[None, :]
    rel = kj[:, None, :] - qi[:, :, None]
    valid = (jnp.abs(rel) <= radius) & (kj[:, None, :] >= 0) & (kj[:, None, :] < l)
    dist = (dilation * jnp.abs(rel)).astype(jnp.float32)
    bias = (-slopes[:, None, None, None] * dist[None]).transpose(1, 0, 2, 3)
    sc = jnp.einsum('bdnqhe,bdnkhe->bdnhqk', qb, kb) + bias
    sc = jnp.where(valid[:, None], sc, NEG_INF)
    lse = jax.nn.logsumexp(sc, axis=-1)
    o = jnp.einsum('bdnhqk,bdnkhe->bdnqhe', jnp.exp(sc - lse[..., None]), vb)
    o = o.reshape(bsz, dilation, lp, h, e)[:, :, :l].transpose(0, 2, 1, 3, 4).reshape(bsz, s, h, e)
    lse = lse.transpose(0, 1, 2, 4, 3).reshape(bsz, dilation, lp, h)[:, :, :l]
    lse = lse.transpose(0, 2, 1, 3).reshape(bsz, s, h)
    return o, lse


def _dilated_attention(q, k, v):
    slopes = jnp.asarray(_alibi_slopes())
    outs, lses = [], []
    for window, dilation in DILATED_PATTERNS:
        o, lse = _dilated_branch(q, k, v, slopes, dilation, window // (2 * dilation))
        outs.append(o)
        lses.append(lse)
    wts = jax.nn.softmax(jnp.stack(lses, 0), axis=0)
    y = jnp.einsum('nbsh,nbshe->bshe', wts, jnp.stack(outs, 0))
    return y.reshape(q.shape[0], q.shape[1], ATT_WIDTH)


def _hybrid_mixer(h, w_in, conv_w, conv_b, dt_bias_f, dt_bias_b, a_log_f, a_log_b, d_skip, norm_w, w_out):
    bsz, s, _ = h.shape
    proj = h @ w_in
    z = proj[..., Z_OFF:XBC_OFF]
    xbc = proj[..., XBC_OFF:DT_OFF]
    dt_raw = proj[..., DT_OFF:Q_OFF]
    hd = (bsz, s, ATT_HEADS, ATT_HEAD_DIM)
    q = proj[..., Q_OFF:K_OFF].astype(jnp.float32).reshape(hd) * (ATT_HEAD_DIM ** -0.5)
    k = proj[..., K_OFF:V_OFF].astype(jnp.float32).reshape(hd)
    v = proj[..., V_OFF:IN_COLS].astype(jnp.float32).reshape(hd)
    y_ssd = _ssd_group(z, xbc, dt_raw, conv_w, conv_b, dt_bias_f, dt_bias_b, a_log_f, a_log_b, d_skip, norm_w)
    y_att = _dilated_attention(q, k, v)
    y = jnp.concatenate([y_ssd, y_att], axis=-1).astype(h.dtype)
    return y @ w_out


def _memory_cross_attention(h, mem, wq, wk, wv, wo):
    bsz, s, _ = h.shape
    m = mem.shape[1]
    q = (h @ wq).reshape(bsz, s, MEM_HEADS, MEM_HEAD_DIM).astype(jnp.float32)
    k = (mem @ wk).reshape(bsz, m, MEM_HEADS, MEM_HEAD_DIM).astype(jnp.float32)
    v = (mem @ wv).reshape(bsz, m, MEM_HEADS, MEM_HEAD_DIM).astype(jnp.float32)
    p = jax.nn.softmax(jnp.einsum('bshe,bmhe->bhsm', q, k) * (MEM_HEAD_DIM ** -0.5), axis=-1)
    o = jnp.einsum('bhsm,bmhe->bshe', p, v).reshape(bsz, s, MEM_WIDTH).astype(h.dtype)
    return o @ wo


def _swiglu(x, wg, wu, wd):
    return (jax.nn.silu(x @ wg) * (x @ wu)) @ wd


def _routed_experts(xt, eidx, gates, w_gate, w_up, w_down):
    t, d = xt.shape
    a = t * TOP_K
    e_flat = eidx.reshape(a)
    tok_flat = jnp.arange(a, dtype=jnp.int32) // TOP_K
    order = jnp.argsort(e_flat)
    e_sorted = e_flat[order]
    counts = jnp.bincount(e_flat, length=N_EXPERTS)
    starts = jnp.cumsum(counts) - counts
    padded = (counts + MOE_BLOCK - 1) // MOE_BLOCK * MOE_BLOCK
    pends = jnp.cumsum(padded)
    dest = pends[e_sorted] - padded[e_sorted] + (jnp.arange(a, dtype=jnp.int32) - starts[e_sorted])
    n_blocks = -(-a // MOE_BLOCK) + N_EXPERTS
    n_pad = n_blocks * MOE_BLOCK
    tok_buf = jnp.zeros((n_pad,), jnp.int32).at[dest].set(tok_flat[order])
    gate_buf = jnp.zeros((n_pad,), jnp.float32).at[dest].set(gates.reshape(a)[order])
    block_e = jnp.minimum(jnp.searchsorted(pends, jnp.arange(n_blocks, dtype=jnp.int32) * MOE_BLOCK, side='right'),
                          N_EXPERTS - 1)

    def expert_block(args):
        e, tok, g = args
        xb = xt[tok]
        return _swiglu(xb, w_gate[e], w_up[e], w_down[e]).astype(jnp.float32) * g[:, None]

    outs = lax.map(expert_block, (block_e, tok_buf.reshape(n_blocks, MOE_BLOCK),
                                  gate_buf.reshape(n_blocks, MOE_BLOCK)))
    return jnp.zeros((t, d), jnp.float32).at[tok_buf].add(outs.reshape(n_pad, d))


def _moe_ffn(h, w_router, router_bias, w_gate, w_up, w_down, ws_gate, ws_up, ws_down):
    bsz, s, d = h.shape
    t = bsz * s
    xt = h.reshape(t, d)
    scores = jax.nn.sigmoid((xt @ w_router).astype(jnp.float32))
    choice = scores + router_bias.astype(jnp.float32)
    grp = choice.reshape(t, ROUTE_GROUPS, N_EXPERTS // ROUTE_GROUPS)
    grp_score = lax.top_k(grp, 2)[0].sum(-1)
    _, gidx = lax.top_k(grp_score, TOPK_GROUPS)
    gmask = jax.nn.one_hot(gidx, ROUTE_GROUPS, dtype=jnp.float32).sum(1) > 0
    emask = jnp.repeat(gmask, N_EXPERTS // ROUTE_GROUPS, axis=1)
    _, eidx = lax.top_k(jnp.where(emask, choice, NEG_INF), TOP_K)
    wts = jnp.take_along_axis(scores, eidx, axis=1)
    wts = wts / jnp.sum(wts, -1, keepdims=True) * ROUTED_SCALE
    routed = _routed_experts(xt, eidx, wts, w_gate, w_up, w_down)
    shared = _swiglu(xt, ws_gate, ws_up, ws_down).astype(jnp.float32)
    return (routed + shared).astype(h.dtype).reshape(bsz, s, d)


def setup_inputs(seed: int = 0) -> dict:
    key = jax.random.key(seed)
    ks = jax.random.split(key, 32)
    f32 = jnp.float32
    L = DEPTH
    D = D_MODEL

    def nrm(k, shape, scale):
        return jax.random.normal(k, shape, f32) * scale

    def dt_bias(k):
        dt = jnp.exp(jax.random.uniform(k, (L, SSD_HEADS), f32, math.log(DT_MIN), math.log(DT_MAX)))
        return dt + jnp.log(-jnp.expm1(-dt))

    w_in = nrm(ks[4], (L, D, IN_COLS), D ** -0.5)
    w_in = w_in.at[:, :, V_OFF:].multiply(DN_BETA)
    return {
        'x': nrm(ks[0], (BATCH, SEQ, D), 1.0),
        'mem': nrm(ks[1], (BATCH, MEM_LEN, D), 1.0),
        'ln_in_g': 1.0 + nrm(ks[2], (D,), 0.02),
        'ln_in_b': nrm(ks[3], (D,), 0.02),
        'w_in': w_in,
        'conv_w': nrm(ks[5], (L, SSD_CONV, CONV_CH), SSD_CONV ** -0.5),
        'conv_b': nrm(ks[6], (L, CONV_CH), 0.02),
        'dt_bias_f': dt_bias(ks[7]),
        'dt_bias_b': dt_bias(ks[8]),
        'a_log_f': jnp.log(jax.random.uniform(ks[9], (L, SSD_HEADS), f32, 1.0, 16.0)),
        'a_log_b': jnp.log(jax.random.uniform(ks[10], (L, SSD_HEADS), f32, 1.0, 16.0)),
        'd_skip': 1.0 + nrm(ks[11], (L, SSD_HEADS), 0.1),
        'ssd_norm_w': 1.0 + nrm(ks[12], (L, SSD_WIDTH), 0.02),
        'w_out_mix': nrm(ks[13], (L, MIX_WIDTH, D), MIX_WIDTH ** -0.5 * DN_BETA),
        'ln1_g': 1.0 + nrm(ks[14], (L, D), 0.02),
        'ln1_b': nrm(ks[15], (L, D), 0.02),
        'wq_mem': nrm(ks[16], (L, D, MEM_WIDTH), D ** -0.5),
        'wk_mem': nrm(ks[17], (L, D, MEM_WIDTH), D ** -0.5),
        'wv_mem': nrm(ks[18], (L, D, MEM_WIDTH), D ** -0.5 * DN_BETA),
        'wo_mem': nrm(ks[19], (L, MEM_WIDTH, D), MEM_WIDTH ** -0.5 * DN_BETA),
        'ln2_g': 1.0 + nrm(ks[20], (L, D), 0.02),
        'ln2_b': nrm(ks[21], (L, D), 0.02),
        'w_router': nrm(ks[22], (L, D, N_EXPERTS), D ** -0.5),
        'router_bias': nrm(ks[23], (L, N_EXPERTS), 0.01),
        'w_gate': nrm(ks[24], (L, N_EXPERTS, D, EXPERT_DIM), D ** -0.5),
        'w_up': nrm(ks[25], (L, N_EXPERTS, D, EXPERT_DIM), D ** -0.5),
        'w_down': nrm(ks[26], (L, N_EXPERTS, EXPERT_DIM, D), EXPERT_DIM ** -0.5 * DN_BETA),
        'ws_gate': nrm(ks[27], (L, D, SHARED_DIM), D ** -0.5),
        'ws_up': nrm(ks[28], (L, D, SHARED_DIM), D ** -0.5),
        'ws_down': nrm(ks[29], (L, SHARED_DIM, D), SHARED_DIM ** -0.5 * DN_BETA),
        'ln3_g': 1.0 + nrm(ks[30], (L, D), 0.02),
        'ln3_b': nrm(ks[31], (L, D), 0.02),
    }


def reference(x, mem, ln_in_g, ln_in_b, w_in, conv_w, conv_b, dt_bias_f, dt_bias_b, a_log_f, a_log_b,
              d_skip, ssd_norm_w, w_out_mix, ln1_g, ln1_b, wq_mem, wk_mem, wv_mem, wo_mem, ln2_g, ln2_b,
              w_router, router_bias, w_gate, w_up, w_down, ws_gate, ws_up, ws_down, ln3_g, ln3_b):
    h = _layer_norm(x, ln_in_g, ln_in_b)
    for l in range(DEPTH):
        mix = _hybrid_mixer(h, w_in[l], conv_w[l], conv_b[l], dt_bias_f[l], dt_bias_b[l], a_log_f[l],
                            a_log_b[l], d_skip[l], ssd_norm_w[l], w_out_mix[l])
        h = _layer_norm(DN_ALPHA * h + mix, ln1_g[l], ln1_b[l])
        xa = _memory_cross_attention(h, mem, wq_mem[l], wk_mem[l], wv_mem[l], wo_mem[l])
        h = _layer_norm(DN_ALPHA * h + xa, ln2_g[l], ln2_b[l])
        ff = _moe_ffn(h, w_router[l], router_bias[l], w_gate[l], w_up[l], w_down[l],
                      ws_gate[l], ws_up[l], ws_down[l])
        h = _layer_norm(DN_ALPHA * h + ff, ln3_g[l], ln3_b[l])
    return h
```

```python
import functools
import math

import numpy as np
import jax
import jax.numpy as jnp
from jax import lax
from jax.experimental import pallas as pl
from jax.experimental.pallas import tpu as pltpu

F32 = jnp.float32
BF16 = jnp.bfloat16
I32 = jnp.int32

D_MODEL = 1024
SSD_WIDTH = 1024
SSD_HEAD_DIM = 64
SSD_HEADS = 16
SSD_STATE = 128
SSD_GROUPS = 4
SSD_HEADS_PER_GROUP = SSD_HEADS // SSD_GROUPS
SSD_CONV = 5
SSD_CHUNK = 128
ATT_WIDTH = 1024
ATT_HEAD_DIM = 64
ATT_HEADS = 16
ATT_RADIUS = 64
DILATIONS = (1, 4, 16)
MEM_HEADS = 4
MEM_HEAD_DIM = 256
N_EXPERTS = 256
ROUTE_GROUPS = 8
GROUP_SIZE = N_EXPERTS // ROUTE_GROUPS
TOPK_GROUPS = 4
TOP_K = 8
EXPERT_DIM = 256
ROUTED_SCALE = 2.5
MOE_BLOCK = 256
DN_ALPHA = 2.0 ** 0.25
LN_EPS = 1e-5
RMS_EPS = 1e-5
NEG_INF = -1e30
GN = SSD_GROUPS * SSD_STATE
CONV_CH = SSD_WIDTH + 2 * GN
XBC_OFF = SSD_WIDTH
DT_OFF = XBC_OFF + CONV_CH
Q_OFF = DT_OFF + 2 * SSD_HEADS
IN_COLS = Q_OFF + 3 * ATT_WIDTH

LANES = 128
VMEM_LIMIT_BYTES = 56 * 1024 * 1024


def _cparams(*sem):
    return pltpu.CompilerParams(dimension_semantics=sem, vmem_limit_bytes=VMEM_LIMIT_BYTES)


def _ln_rows(x, g, b):
    mu = jnp.mean(x, -1, keepdims=True)
    xc = x - mu
    var = jnp.mean(xc * xc, -1, keepdims=True)
    return xc * lax.rsqrt(var + LN_EPS) * g + b


def _dot(a, b):
    return jnp.dot(a, b, preferred_element_type=F32)


def _dot_nt(a, b):
    return lax.dot_general(a, b, (((1,), (1,)), ((), ())), preferred_element_type=F32)


def _silu(x):
    return x * jax.nn.sigmoid(x)


def _inproj_kernel(x_ref, g_ref, b_ref, wz_ref, wxbc_ref, wdt_ref, wqkv_ref,
                   h_ref, z_ref, xbc_ref, dt_ref, qkv1_ref, qkv4_ref, qkv16_ref, stage_ref):
    tm = x_ref.shape[1]
    h = _ln_rows(x_ref[0], g_ref[...], b_ref[...])
    h_ref[0] = h
    hb = h.astype(BF16)
    z_ref[0] = _dot(hb, wz_ref[...]).astype(BF16)
    for c in range(CONV_CH // D_MODEL):
        cols = slice(c * D_MODEL, (c + 1) * D_MODEL)
        xbc_ref[0, :, cols] = _dot(hb, wxbc_ref[:, cols]).astype(BF16)
    dt_ref[0] = _dot(hb, wdt_ref[...])
    for c in range(3):
        cols = slice(c * ATT_WIDTH, (c + 1) * ATT_WIDTH)
        res = _dot(hb, wqkv_ref[:, cols])
        if c == 0:
            res = res * (ATT_HEAD_DIM ** -0.5)
        qkv1_ref[c, 0] = res.astype(BF16)
        for j in range(ATT_WIDTH // LANES):
            lanes = slice(j * LANES, (j + 1) * LANES)
            stage_ref[j] = res[:, lanes]
            for r in range(4):
                qkv4_ref[c, 0, r, :, lanes] = stage_ref[j, pl.ds(r, tm // 4, stride=4), :].astype(BF16)
            for r in range(16):
                qkv16_ref[c, 0, r, :, lanes] = stage_ref[j, pl.ds(r, tm // 16, stride=16), :].astype(BF16)


def _inproj(x, g, b, wz, wxbc, wdt, wqkv, tm=256):
    bsz, s, d = x.shape
    nt = s // tm
    const = lambda *_: (0, 0)
    wspec = lambda w: pl.BlockSpec(w.shape, const)
    row = lambda n: pl.BlockSpec((1, tm, n), lambda bi, i: (bi, i, 0))
    out_shape = (
        jax.ShapeDtypeStruct((bsz, s, d), F32),
        jax.ShapeDtypeStruct((bsz, s, SSD_WIDTH), BF16),
        jax.ShapeDtypeStruct((bsz, s, CONV_CH), BF16),
        jax.ShapeDtypeStruct((bsz, s, SSD_GROUPS * LANES), F32),
        jax.ShapeDtypeStruct((3, bsz, s, ATT_WIDTH), BF16),
        jax.ShapeDtypeStruct((3, bsz, 4, s // 4, ATT_WIDTH), BF16),
        jax.ShapeDtypeStruct((3, bsz, 16, s // 16, ATT_WIDTH), BF16),
    )
    out_specs = (
        row(d), row(SSD_WIDTH), row(CONV_CH), row(SSD_GROUPS * LANES),
        pl.BlockSpec((3, 1, tm, ATT_WIDTH), lambda bi, i: (0, bi, i, 0)),
        pl.BlockSpec((3, 1, 4, tm // 4, ATT_WIDTH), lambda bi, i: (0, bi, 0, i, 0)),
        pl.BlockSpec((3, 1, 16, tm // 16, ATT_WIDTH), lambda bi, i: (0, bi, 0, i, 0)),
    )
    return pl.pallas_call(
        _inproj_kernel,
        grid=(bsz, nt),
        in_specs=[row(d), wspec(g), wspec(b), wspec(wz), wspec(wxbc), wspec(wdt), wspec(wqkv)],
        out_specs=out_specs,
        out_shape=out_shape,
        scratch_shapes=[pltpu.VMEM((ATT_WIDTH // LANES, tm, LANES), F32)],
        compiler_params=_cparams("parallel", "parallel"),
        name="ln_inproj",
    )(x, g, b, wz, wxbc, wdt, wqkv)


GROUP_X = SSD_HEADS_PER_GROUP * SSD_HEAD_DIM
GROUP_CH = GROUP_X + 2 * SSD_STATE
CONV_ROWS = 256
PAD_ROWS = 8


def _softplus(x):
    return jnp.maximum(x, 0.0) + jnp.log(1.0 + jnp.exp(-jnp.abs(x)))


def _ssd_kernel(x_ref, b_ref, c_ref, dt_ref, cw_ref, cb_ref, dtb_ref, a_ref, dsk_ref,
                y_ref, pad_ref, act_ref, yacc_ref, st_ref):
    s = x_ref.shape[1]
    q = SSD_CHUNK
    nchunks = s // q

    zeros = jnp.zeros((PAD_ROWS, GROUP_CH), F32)
    pad_ref[0:PAD_ROWS, :] = zeros
    pad_ref[s + PAD_ROWS:s + 2 * PAD_ROWS, :] = zeros
    pad_ref[PAD_ROWS:s + PAD_ROWS, 0:GROUP_X] = x_ref[0].astype(F32)
    pad_ref[PAD_ROWS:s + PAD_ROWS, GROUP_X:GROUP_X + SSD_STATE] = b_ref[0].astype(F32)
    pad_ref[PAD_ROWS:s + PAD_ROWS, GROUP_X + SSD_STATE:GROUP_CH] = c_ref[0].astype(F32)
    for i in range(s // CONV_ROWS):
        acc = jnp.broadcast_to(cb_ref[0], (CONV_ROWS, GROUP_CH))
        for k in range(SSD_CONV):
            r0 = PAD_ROWS + i * CONV_ROWS + k - SSD_CONV // 2
            acc = acc + cw_ref[0, k:k + 1, :] * pad_ref[r0:r0 + CONV_ROWS, :]
        act_ref[i * CONV_ROWS:(i + 1) * CONV_ROWS, :] = _silu(acc)

    row_i = lax.broadcasted_iota(I32, (q, q), 0)
    col_i = lax.broadcasted_iota(I32, (q, q), 1)
    lane = lax.broadcasted_iota(I32, (q, GROUP_X), 1)

    def expand(v, off):
        out = jnp.broadcast_to(v[:, off:off + 1], (q, GROUP_X))
        for r in range(1, SSD_HEADS_PER_GROUP):
            out = jnp.where(lane >= r * SSD_HEAD_DIM,
                            jnp.broadcast_to(v[:, off + r:off + r + 1], (q, GROUP_X)), out)
        return out

    def run(reverse):
        off = SSD_HEADS_PER_GROUP if reverse else 0
        tri = (row_i <= col_i) if reverse else (row_i >= col_i)
        tri_f = tri.astype(F32)
        st_ref[...] = jnp.zeros_like(st_ref)

        def body(ci, carry):
            c = (nchunks - 1 - ci) if reverse else ci
            rows = pl.ds(pl.multiple_of(c * q, q), q)
            xa = act_ref[rows, 0:GROUP_X]
            bm = act_ref[rows, GROUP_X:GROUP_X + SSD_STATE]
            cm = act_ref[rows, GROUP_X + SSD_STATE:GROUP_CH]
            dt = _softplus(dt_ref[0, rows, :] + dtb_ref[0])
            dta = dt * a_ref[0]
            cs = jnp.dot(tri_f, dta, precision=lax.Precision.HIGHEST, preferred_element_type=F32)
            cs_t = cs.T
            tot = cs[0:1, :] if reverse else cs[q - 1:q, :]
            cm_b = cm.astype(BF16)
            cbm = _dot_nt(cm_b, bm.astype(BF16))
            xdt = xa * expand(dt, off)
            xdt_b = xdt.astype(BF16)
            ydiag = None
            for r in range(SSD_HEADS_PER_GROUP):
                seg = cs[:, off + r:off + r + 1] - cs_t[off + r:off + r + 1, :]
                w = (jnp.exp(jnp.where(tri, seg, NEG_INF)) * cbm).astype(BF16)
                yr = _dot(w, xdt_b)
                ydiag = yr if r == 0 else jnp.where(lane >= r * SSD_HEAD_DIM, yr, ydiag)
            xs_b = (xdt * expand(jnp.exp(tot - cs), off)).astype(BF16)
            states = _dot(bm.T.astype(BF16), xs_b)
            hprev = st_ref[...]
            yoff = _dot(cm_b, hprev.astype(BF16)) * expand(jnp.exp(cs), off)
            st_ref[...] = hprev * expand(jnp.broadcast_to(jnp.exp(tot), (q, LANES)), off) + states
            if reverse:
                yacc_ref[rows, :] += ydiag + yoff
            else:
                yacc_ref[rows, :] = ydiag + yoff + xa * dsk_ref[0]
            return carry

        lax.fori_loop(0, nchunks, body, 0)

    run(False)
    run(True)
    y_ref[0] = yacc_ref[...].astype(BF16)


def _ssd(xbc, dt, conv_w_g, conv_b_g, dt_bias_g, a_g, dskip_g):
    bsz, s, _ = xbc.shape
    xb = GROUP_X // LANES
    grp = lambda n: pl.BlockSpec((1,) + n, lambda bi, gi: (gi, 0, 0))
    in_specs = [
        pl.BlockSpec((1, s, GROUP_X), lambda bi, gi: (bi, 0, gi)),
        pl.BlockSpec((1, s, SSD_STATE), lambda bi, gi: (bi, 0, SSD_WIDTH // SSD_STATE + gi)),
        pl.BlockSpec((1, s, SSD_STATE), lambda bi, gi: (bi, 0, (SSD_WIDTH + GN) // SSD_STATE + gi)),
        pl.BlockSpec((1, s, LANES), lambda bi, gi: (bi, 0, gi)),
        grp((SSD_CONV, GROUP_CH)), grp((1, GROUP_CH)), grp((1, LANES)), grp((1, LANES)), grp((1, GROUP_X)),
    ]
    del xb
    return pl.pallas_call(
        _ssd_kernel,
        grid=(bsz, SSD_GROUPS),
        in_specs=in_specs,
        out_specs=pl.BlockSpec((1, s, GROUP_X), lambda bi, gi: (bi, 0, gi)),
        out_shape=jax.ShapeDtypeStruct((bsz, s, SSD_WIDTH), BF16),
        scratch_shapes=[
            pltpu.VMEM((s + 2 * PAD_ROWS, GROUP_CH), F32),
            pltpu.VMEM((s, GROUP_CH), F32),
            pltpu.VMEM((s, GROUP_X), F32),
            pltpu.VMEM((SSD_STATE, GROUP_X), F32),
        ],
        compiler_params=_cparams("parallel", "parallel"),
        name="ssd_scan",
    )(xbc, xbc, xbc, dt, conv_w_g, conv_b_g, dt_bias_g, a_g, dskip_g)


ATT_QBLOCK = 128
ATT_PAIR = 2 * ATT_HEAD_DIM


def _att_kernel(q1, k1, v1, q4, k4, v4, q16, k16, v16, slope_ref, o_ref, ob_ref, lb_ref):
    s = o_ref.shape[1]
    qb = ATT_QBLOCK
    lane = lax.broadcasted_iota(I32, (qb, ATT_PAIR), 1)
    first = lane < ATT_HEAD_DIM
    slope_lanes = slope_ref[0]

    def branch(bi, dil, q_ref, k_ref, v_ref):
        seg_len = s // dil
        win = min(2 * qb, seg_len)
        rel0 = (lax.broadcasted_iota(I32, (qb, win), 1) - lax.broadcasted_iota(I32, (qb, win), 0))

        def body(blk, carry):
            qs = pl.multiple_of(blk * qb, qb)
            seg = qs // seg_len
            within = qs - seg * seg_len
            kw = jnp.clip(within - ATT_RADIUS, 0, seg_len - win)
            ks = pl.multiple_of(seg * seg_len + kw, ATT_RADIUS)
            qv = q_ref[0, 0, pl.ds(qs, qb), :]
            kv = k_ref[0, 0, pl.ds(ks, win), :]
            vv = v_ref[0, 0, pl.ds(ks, win), :]
            rel = rel0 + (kw - within)
            valid = jnp.abs(rel) <= ATT_RADIUS
            dist = (dil * jnp.abs(rel)).astype(F32)
            outs, lses = [], []
            for j in range(2):
                sel = first if j == 0 else jnp.logical_not(first)
                qm = jnp.where(sel, qv, jnp.zeros_like(qv))
                slope = slope_lanes[:, j * ATT_HEAD_DIM:j * ATT_HEAD_DIM + 1]
                sc = _dot_nt(qm, kv) - slope * dist
                sc = jnp.where(valid, sc, NEG_INF)
                m = jnp.max(sc, axis=-1, keepdims=True)
                p = jnp.exp(sc - m)
                l = jnp.sum(p, axis=-1, keepdims=True)
                outs.append(_dot(p.astype(BF16), vv) / l)
                lses.append(jnp.broadcast_to(m + jnp.log(l), (qb, ATT_PAIR)))
            o_blk = jnp.where(first, outs[0], outs[1])
            l_blk = jnp.where(first, lses[0], lses[1])
            if dil == 1:
                dst = pl.ds(qs, qb)
            else:
                dst = pl.ds(within * dil + seg, qb, stride=dil)
            ob_ref[bi, dst, :] = o_blk
            lb_ref[bi, dst, :] = l_blk
            return carry

        lax.fori_loop(0, s // qb, body, 0)

    branch(0, DILATIONS[0], q1, k1, v1)
    branch(1, DILATIONS[1], q4, k4, v4)
    branch(2, DILATIONS[2], q16, k16, v16)

    rows = 256
    for i in range(s // rows):
        sl = slice(i * rows, (i + 1) * rows)
        l0, l1, l2 = lb_ref[0, sl, :], lb_ref[1, sl, :], lb_ref[2, sl, :]
        m = jnp.maximum(jnp.maximum(l0, l1), l2)
        w0, w1, w2 = jnp.exp(l0 - m), jnp.exp(l1 - m), jnp.exp(l2 - m)
        num = w0 * ob_ref[0, sl, :] + w1 * ob_ref[1, sl, :] + w2 * ob_ref[2, sl, :]
        o_ref[0, sl, :] = (num / (w0 + w1 + w2)).astype(BF16)


def _dilated_attention(qkv1, qkv4, qkv16, slopes):
    _, bsz, s, _ = qkv1.shape
    npairs = ATT_WIDTH // ATT_PAIR
    spec = lambda c: pl.BlockSpec((1, 1, s, ATT_PAIR), lambda bi, hp, c=c: (c, bi, 0, hp))
    in_specs = [spec(0), spec(1), spec(2)] * 3 + [pl.BlockSpec((1, 1, ATT_PAIR), lambda bi, hp: (hp, 0, 0))]
    return pl.pallas_call(
        _att_kernel,
        grid=(bsz, npairs),
        in_specs=in_specs,
        out_specs=pl.BlockSpec((1, s, ATT_PAIR), lambda bi, hp: (bi, 0, hp)),
        out_shape=jax.ShapeDtypeStruct((bsz, s, ATT_WIDTH), BF16),
        scratch_shapes=[pltpu.VMEM((3, s, ATT_PAIR), F32), pltpu.VMEM((3, s, ATT_PAIR), F32)],
        compiler_params=_cparams("parallel", "parallel"),
        name="dilated_attention",
    )(qkv1, qkv1, qkv1, qkv4, qkv4, qkv4, qkv16, qkv16, qkv16, slopes)


def _outproj_kernel(h_ref, yssd_ref, z_ref, yatt_ref, nw_ref, wo_ref, g_ref, b_ref, o_ref):
    y = yssd_ref[...].astype(F32) * _silu(z_ref[...].astype(F32))
    yn = y * lax.rsqrt(jnp.mean(y * y, -1, keepdims=True) + RMS_EPS) * nw_ref[...]
    mix = _dot(yn.astype(BF16), wo_ref[0:SSD_WIDTH, :]) + _dot(yatt_ref[...], wo_ref[SSD_WIDTH:, :])
    o_ref[...] = _ln_rows(DN_ALPHA * h_ref[...] + mix, g_ref[...], b_ref[...])


def _outproj(h, yssd, z, yatt, norm_w, w_out, g, b, tm=512):
    t, d = h.shape
    const = lambda *_: (0, 0)
    wspec = lambda w: pl.BlockSpec(w.shape, const)
    row = lambda n: pl.BlockSpec((tm, n), lambda i: (i, 0))
    return pl.pallas_call(
        _outproj_kernel,
        grid=(t // tm,),
        in_specs=[row(d), row(SSD_WIDTH), row(SSD_WIDTH), row(ATT_WIDTH),
                  wspec(norm_w), wspec(w_out), wspec(g), wspec(b)],
        out_specs=row(d),
        out_shape=jax.ShapeDtypeStruct((t, d), F32),
        compiler_params=_cparams("parallel"),
        name="outproj_ln",
    )(h, yssd, z, yatt, norm_w, w_out, g, b)


def _alibi_slopes():
    sl = np.array([2.0 ** (-8.0 * (h + 1) / ATT_HEADS) for h in range(ATT_HEADS)], dtype=np.float32)
    return jnp.asarray(np.repeat(sl, ATT_HEAD_DIM).reshape(ATT_WIDTH // ATT_PAIR, 1, ATT_PAIR))


def _mixer_params(w_in, conv_w, conv_b, dt_bias_f, dt_bias_b, a_log_f, a_log_b, d_skip):
    r = SSD_HEADS_PER_GROUP
    wz = w_in[:, 0:XBC_OFF].astype(BF16)
    wxbc = w_in[:, XBC_OFF:DT_OFF].astype(BF16)
    wqkv = w_in[:, Q_OFF:IN_COLS].astype(BF16)
    wdt_f = w_in[:, DT_OFF:DT_OFF + SSD_HEADS].reshape(D_MODEL, SSD_GROUPS, r)
    wdt_b = w_in[:, DT_OFF + SSD_HEADS:Q_OFF].reshape(D_MODEL, SSD_GROUPS, r)
    lane_pad = ((0, 0), (0, 0), (0, LANES - 2 * r))
    wdt = jnp.pad(jnp.concatenate([wdt_f, wdt_b], -1), lane_pad).reshape(D_MODEL, SSD_GROUPS * LANES).astype(BF16)

    def per_group_lanes(f, b):
        v = jnp.concatenate([f.reshape(SSD_GROUPS, 1, r), b.reshape(SSD_GROUPS, 1, r)], -1)
        return jnp.pad(v.astype(F32), lane_pad)

    dtb = per_group_lanes(dt_bias_f, dt_bias_b)
    a = per_group_lanes(-jnp.exp(a_log_f.astype(F32)), -jnp.exp(a_log_b.astype(F32)))

    def group_cols(w):
        xs = w[..., 0:SSD_WIDTH].reshape(w.shape[:-1] + (SSD_GROUPS, GROUP_X))
        bs = w[..., SSD_WIDTH:SSD_WIDTH + GN].reshape(w.shape[:-1] + (SSD_GROUPS, SSD_STATE))
        cs = w[..., SSD_WIDTH + GN:].reshape(w.shape[:-1] + (SSD_GROUPS, SSD_STATE))
        return jnp.moveaxis(jnp.concatenate([xs, bs, cs], -1), -2, 0)

    cw = group_cols(conv_w.astype(F32))
    cb = group_cols(conv_b.astype(F32)[None])
    dsk = jnp.repeat(d_skip.astype(F32), SSD_HEAD_DIM).reshape(SSD_GROUPS, 1, GROUP_X)
    return wz, wxbc, wdt, wqkv, cw, cb, dtb, a, dsk


def _mixer_layer(x, ln_in_g, ln_in_b, w_in, conv_w, conv_b, dt_bias_f, dt_bias_b, a_log_f, a_log_b,
                 d_skip, ssd_norm_w, w_out_mix, ln1_g, ln1_b):
    bsz, s, d = x.shape
    vec = lambda v: v.astype(F32).reshape(1, -1)
    wz, wxbc, wdt, wqkv, cw, cb, dtb, a, dsk = _mixer_params(
        w_in, conv_w, conv_b, dt_bias_f, dt_bias_b, a_log_f, a_log_b, d_skip)
    h, z, xbc, dt, qkv1, qkv4, qkv16 = _inproj(x, vec(ln_in_g), vec(ln_in_b), wz, wxbc, wdt, wqkv)
    yssd = _ssd(xbc, dt, cw, cb, dtb, a, dsk)
    yatt = _dilated_attention(qkv1, qkv4.reshape(qkv1.shape), qkv16.reshape(qkv1.shape), _alibi_slopes())
    t = bsz * s
    return _outproj(h.reshape(t, d), yssd.reshape(t, SSD_WIDTH), z.reshape(t, SSD_WIDTH),
                    yatt.reshape(t, ATT_WIDTH), vec(ssd_norm_w), w_out_mix.astype(BF16), vec(ln1_g), vec(ln1_b))


def _memkv_kernel(mem_ref, wk_ref, wv_ref, k_ref, v_ref):
    m = mem_ref[0].astype(BF16)
    k_ref[0] = _dot(m, wk_ref[...]).astype(BF16)
    v_ref[0] = _dot(m, wv_ref[...]).astype(BF16)


def _memkv(mem, wk, wv):
    bsz, m, d = mem.shape
    const = lambda *_: (0, 0)
    blk = pl.BlockSpec((1, m, d), lambda bi: (bi, 0, 0))
    return pl.pallas_call(
        _memkv_kernel,
        grid=(bsz,),
        in_specs=[blk, pl.BlockSpec(wk.shape, const), pl.BlockSpec(wv.shape, const)],
        out_specs=(blk, blk),
        out_shape=(jax.ShapeDtypeStruct((bsz, m, d), BF16),) * 2,
        compiler_params=_cparams("parallel"),
        name="mem_kv",
    )(mem, wk, wv)


def _memattn_kernel(h_ref, k_ref, v_ref, wq_ref, wo_ref, g_ref, b_ref, o_ref):
    h = h_ref[0]
    q = (_dot(h.astype(BF16), wq_ref[...]) * (MEM_HEAD_DIM ** -0.5)).astype(BF16)
    xa = None
    for hd in range(MEM_HEADS):
        cols = slice(hd * MEM_HEAD_DIM, (hd + 1) * MEM_HEAD_DIM)
        sc = _dot_nt(q[:, cols], k_ref[0, :, cols])
        p = jnp.exp(sc - jnp.max(sc, axis=-1, keepdims=True))
        o = _dot(p.astype(BF16), v_ref[0, :, cols]) / jnp.sum(p, axis=-1, keepdims=True)
        part = _dot(o.astype(BF16), wo_ref[cols, :])
        xa = part if xa is None else xa + part
    o_ref[0] = _ln_rows(DN_ALPHA * h + xa, g_ref[...], b_ref[...])


def _memattn(h, k, v, wq, wo, g, b, tm=512):
    bsz, s, d = h.shape
    m = k.shape[1]
    const = lambda *_: (0, 0)
    wspec = lambda w: pl.BlockSpec(w.shape, const)
    row = pl.BlockSpec((1, tm, d), lambda bi, i: (bi, i, 0))
    kv = pl.BlockSpec((1, m, d), lambda bi, i: (bi, 0, 0))
    return pl.pallas_call(
        _memattn_kernel,
        grid=(bsz, s // tm),
        in_specs=[row, kv, kv, wspec(wq), wspec(wo), wspec(g), wspec(b)],
        out_specs=row,
        out_shape=jax.ShapeDtypeStruct((bsz, s, d), F32),
        compiler_params=_cparams("parallel", "parallel"),
        name="mem_attn_ln",
    )(h, k, v, wq, wo, g, b)


def _first_max(work, idx_iota, sentinel):
    m = jnp.max(work, axis=0, keepdims=True)
    idx = jnp.min(jnp.where(work == m, idx_iota, sentinel), axis=0, keepdims=True)
    return m, idx


def _router_kernel(h_ref, wrt_ref, bias_ref, eidx_ref, gate_ref, rank_ref, cnt_ref, run_ref):
    tm = h_ref.shape[0]
    e = N_EXPERTS

    @pl.when(pl.program_id(0) == 0)
    def _():
        run_ref[...] = jnp.zeros_like(run_ref)

    logits = lax.dot_general(wrt_ref[...], h_ref[...], (((1,), (1,)), ((), ())),
                             precision=lax.Precision.HIGHEST, preferred_element_type=F32)
    scores = jax.nn.sigmoid(logits)
    choice = scores + bias_ref[...]
    row = lax.broadcasted_iota(I32, (e, tm), 0)
    grow = lax.broadcasted_iota(I32, (GROUP_SIZE, tm), 0)
    gs = []
    for g in range(ROUTE_GROUPS):
        blk = choice[g * GROUP_SIZE:(g + 1) * GROUP_SIZE, :]
        m1, i1 = _first_max(blk, grow, GROUP_SIZE)
        m2 = jnp.max(jnp.where(grow == i1, -jnp.inf, blk), axis=0, keepdims=True)
        gs.append(m1 + m2)
    work = jnp.concatenate(gs, axis=0)
    giota = lax.broadcasted_iota(I32, (ROUTE_GROUPS, tm), 0)
    gmask = jnp.zeros((ROUTE_GROUPS, tm), jnp.bool_)
    for _ in range(TOPK_GROUPS):
        _, gi = _first_max(work, giota, ROUTE_GROUPS)
        hit = giota == gi
        gmask = jnp.logical_or(gmask, hit)
        work = jnp.where(hit, -jnp.inf, work)
    work = jnp.concatenate(
        [jnp.where(gmask[g:g + 1, :], choice[g * GROUP_SIZE:(g + 1) * GROUP_SIZE, :], NEG_INF)
         for g in range(ROUTE_GROUPS)], axis=0)
    chosen = jnp.zeros((e, tm), jnp.bool_)
    idxs, svals = [], []
    for _ in range(TOP_K):
        _, ei = _first_max(work, row, e)
        hit = row == ei
        svals.append(jnp.sum(jnp.where(hit, scores, 0.0), axis=0, keepdims=True))
        idxs.append(ei)
        chosen = jnp.logical_or(chosen, hit)
        work = jnp.where(hit, -jnp.inf, work)
    eidx = jnp.concatenate(idxs, axis=0)
    sv = jnp.concatenate(svals, axis=0)
    eidx_ref[...] = eidx
    gate_ref[...] = sv / jnp.sum(sv, axis=0, keepdims=True) * ROUTED_SCALE
    upper = (lax.broadcasted_iota(I32, (tm, tm), 0) < lax.broadcasted_iota(I32, (tm, tm), 1))
    mt = chosen.astype(F32)
    cum = _dot(mt.astype(BF16), upper.astype(F32).astype(BF16)) + run_ref[...]
    rank_ref[...] = jnp.concatenate(
        [jnp.sum(jnp.where(row == idxs[k], cum, 0.0), axis=0, keepdims=True) for k in range(TOP_K)],
        axis=0).astype(I32)
    run_ref[...] += jnp.sum(mt, axis=1, keepdims=True)
    cnt_ref[...] = run_ref[...]


def _router(h, w_router_t, bias_col, tm=512):
    t, d = h.shape
    const = lambda *_: (0, 0)
    col = lambda dt: pl.BlockSpec((TOP_K, tm), lambda i: (0, i))
    return pl.pallas_call(
        _router_kernel,
        grid=(t // tm,),
        in_specs=[pl.BlockSpec((tm, d), lambda i: (i, 0)), pl.BlockSpec(w_router_t.shape, const),
                  pl.BlockSpec(bias_col.shape, const)],
        out_specs=(col(I32), col(F32), col(I32), pl.BlockSpec((N_EXPERTS, 1), const)),
        out_shape=(jax.ShapeDtypeStruct((TOP_K, t), I32), jax.ShapeDtypeStruct((TOP_K, t), F32),
                   jax.ShapeDtypeStruct((TOP_K, t), I32), jax.ShapeDtypeStruct((N_EXPERTS, 1), F32)),
        scratch_shapes=[pltpu.VMEM((N_EXPERTS, 1), F32)],
        compiler_params=_cparams("arbitrary"),
        name="router_topk",
    )(h, w_router_t, bias_col)


def _dispatch_kernel(cnt_ref, eidx_ref, rank_ref, dest_ref, bexp_ref, nused_ref):
    tm = eidx_ref.shape[1]
    e = N_EXPERTS
    nblk = bexp_ref.shape[1]
    nb = jnp.floor((cnt_ref[...] + (MOE_BLOCK - 1)) * (1.0 / MOE_BLOCK))
    strict = (lax.broadcasted_iota(I32, (e, e), 1) < lax.broadcasted_iota(I32, (e, e), 0))
    bstart = _dot(strict.astype(F32).astype(BF16), jnp.broadcast_to(nb, (e, LANES)).astype(BF16))[:, 0:1]
    pstart = bstart * MOE_BLOCK
    row = lax.broadcasted_iota(I32, (e, tm), 0)
    eidx = eidx_ref[...]
    rank = rank_ref[...]
    dest_ref[...] = jnp.concatenate(
        [jnp.sum(jnp.where(row == eidx[k:k + 1, :], pstart, 0.0), axis=0, keepdims=True).astype(I32)
         + rank[k:k + 1, :] for k in range(TOP_K)], axis=0)
    bend = bstart + nb
    jb = lax.broadcasted_iota(I32, (e, nblk), 1).astype(F32)
    nle = jnp.sum((bend <= jb).astype(F32), axis=0, keepdims=True)
    bexp_ref[...] = jnp.minimum(nle, e - 1).astype(I32)
    nused_ref[...] = jnp.broadcast_to(bend[e - 1:e, :], (1, LANES)).astype(I32)


def _dispatch(cnt, eidx, rank, n_blocks, tm=2048):
    t = eidx.shape[1]
    const = lambda *_: (0, 0)
    col = pl.BlockSpec((TOP_K, tm), lambda i: (0, i))
    return pl.pallas_call(
        _dispatch_kernel,
        grid=(t // tm,),
        in_specs=[pl.BlockSpec(cnt.shape, const), col, col],
        out_specs=(col, pl.BlockSpec((1, n_blocks), const), pl.BlockSpec((1, LANES), const)),
        out_shape=(jax.ShapeDtypeStruct((TOP_K, t), I32), jax.ShapeDtypeStruct((1, n_blocks), I32),
                   jax.ShapeDtypeStruct((1, LANES), I32)),
        compiler_params=_cparams("arbitrary"),
        name="dispatch_slots",
    )(cnt, eidx, rank)


def _experts_kernel(bexp_ref, nused_ref, xs_ref, wg_ref, wu_ref, wd_ref, ys_ref):
    @pl.when(pl.program_id(0) < nused_ref[0])
    def _():
        x = xs_ref[...]
        g = _dot(x, wg_ref[0].astype(BF16))
        u = _dot(x, wu_ref[0].astype(BF16))
        a = (_silu(g) * u).astype(BF16)
        ys_ref[...] = _dot(a, wd_ref[0].astype(BF16)).astype(BF16)


def _experts(bexp, nused, xs, w_gate, w_up, w_down):
    n_pad, d = xs.shape
    n_blocks = n_pad // MOE_BLOCK
    last = lambda j, nu: jnp.minimum(j, nu[0] - 1)
    rows = pl.BlockSpec((MOE_BLOCK, d), lambda j, be, nu: (last(j, nu), 0))
    wspec = lambda w: pl.BlockSpec((1,) + w.shape[1:], lambda j, be, nu: (be[last(j, nu)], 0, 0))
    return pl.pallas_call(
        _experts_kernel,
        grid_spec=pltpu.PrefetchScalarGridSpec(
            num_scalar_prefetch=2,
            grid=(n_blocks,),
            in_specs=[rows, wspec(w_gate), wspec(w_up), wspec(w_down)],
            out_specs=rows,
        ),
        out_shape=jax.ShapeDtypeStruct((n_pad, d), BF16),
        compiler_params=_cparams("arbitrary"),
        name="routed_experts",
    )(bexp, nused, xs, w_gate, w_up, w_down)


def _final_kernel(h_ref, yg_ref, gate_ref, wsg_ref, wsu_ref, wsd_ref, g_ref, b_ref, o_ref):
    h = h_ref[...]
    d = h.shape[1]
    hb = h.astype(BF16)
    ff = _dot((_silu(_dot(hb, wsg_ref[...])) * _dot(hb, wsu_ref[...])).astype(BF16), wsd_ref[...])
    gate = gate_ref[...]
    for k in range(TOP_K):
        ff = ff + gate[:, k:k + 1] * yg_ref[:, k * d:(k + 1) * d].astype(F32)
    o_ref[...] = _ln_rows(DN_ALPHA * h + ff, g_ref[...], b_ref[...])


def _final(h, yg, gate, wsg, wsu, wsd, g, b, tm=256):
    t, d = h.shape
    const = lambda *_: (0, 0)
    wspec = lambda w: pl.BlockSpec(w.shape, const)
    row = lambda n: pl.BlockSpec((tm, n), lambda i: (i, 0))
    return pl.pallas_call(
        _final_kernel,
        grid=(t // tm,),
        in_specs=[row(d), row(TOP_K * d), row(TOP_K), wspec(wsg), wspec(wsu), wspec(wsd), wspec(g), wspec(b)],
        out_specs=row(d),
        out_shape=jax.ShapeDtypeStruct((t, d), F32),
        compiler_params=_cparams("parallel"),
        name="shared_combine_ln",
    )(h, yg, gate, wsg, wsu, wsd, g, b)


def _moe_layer(h2, w_router, router_bias, w_gate, w_up, w_down, ws_gate, ws_up, ws_down, ln3_g, ln3_b):
    t, d = h2.shape
    vec = lambda v: v.astype(F32).reshape(1, -1)
    eidx, gate, rank, cnt = _router(h2, w_router.astype(F32).T, router_bias.astype(F32).reshape(-1, 1))
    n_blocks = t * TOP_K // MOE_BLOCK + N_EXPERTS
    dest, bexp, nused = _dispatch(cnt, eidx, rank, n_blocks)
    dest_flat = dest.T.reshape(t * TOP_K)
    xs = jnp.zeros((n_blocks * MOE_BLOCK, d), BF16).at[dest_flat].set(jnp.repeat(h2.astype(BF16), TOP_K, axis=0))
    ys = _experts(bexp.reshape(n_blocks), nused[0, 0:1], xs, w_gate, w_up, w_down)
    yg = ys[dest_flat].reshape(t, TOP_K * d)
    return _final(h2, yg, gate.T, ws_gate.astype(BF16), ws_up.astype(BF16), ws_down.astype(BF16),
                  vec(ln3_g), vec(ln3_b))


def kernel(x, mem, ln_in_g, ln_in_b, w_in, conv_w, conv_b, dt_bias_f, dt_bias_b, a_log_f, a_log_b, d_skip, ssd_norm_w, w_out_mix, ln1_g, ln1_b, wq_mem, wk_mem, wv_mem, wo_mem, ln2_g, ln2_b, w_router, router_bias, w_gate, w_up, w_down, ws_gate, ws_up, ws_down, ln3_g, ln3_b):
    l = 0
    h1 = _mixer_layer(x, ln_in_g, ln_in_b, w_in[l], conv_w[l], conv_b[l], dt_bias_f[l], dt_bias_b[l],
                      a_log_f[l], a_log_b[l], d_skip[l], ssd_norm_w[l], w_out_mix[l], ln1_g[l], ln1_b[l])
    bsz, s, d = x.shape
    vec = lambda v: v.astype(F32).reshape(1, -1)
    km, vm = _memkv(mem, wk_mem[l].astype(BF16), wv_mem[l].astype(BF16))
    h2 = _memattn(h1.reshape(bsz, s, d), km, vm, wq_mem[l].astype(BF16), wo_mem[l].astype(BF16),
                  vec(ln2_g[l]), vec(ln2_b[l]))
    out = _moe_layer(h2.reshape(bsz * s, d), w_router[l], router_bias[l], w_gate[l], w_up[l], w_down[l],
                     ws_gate[l], ws_up[l], ws_down[l], ln3_g[l], ln3_b[l])
    return out.reshape(x.shape)
```

```python
import functools
import math

import numpy as np
import jax
import jax.numpy as jnp
from jax import lax
from jax.experimental import pallas as pl
from jax.experimental.pallas import tpu as pltpu
from jax.experimental.pallas import tpu_sc as plsc

F32 = jnp.float32
BF16 = jnp.bfloat16
I32 = jnp.int32

D_MODEL = 1024
SSD_WIDTH = 1024
SSD_HEAD_DIM = 64
SSD_HEADS = 16
SSD_STATE = 128
SSD_GROUPS = 4
SSD_HEADS_PER_GROUP = SSD_HEADS // SSD_GROUPS
SSD_CONV = 5
SSD_CHUNK = 128
ATT_WIDTH = 1024
ATT_HEAD_DIM = 64
ATT_HEADS = 16
ATT_RADIUS = 64
DILATIONS = (1, 4, 16)
MEM_HEADS = 4
MEM_HEAD_DIM = 256
N_EXPERTS = 256
ROUTE_GROUPS = 8
GROUP_SIZE = N_EXPERTS // ROUTE_GROUPS
TOPK_GROUPS = 4
TOP_K = 8
EXPERT_DIM = 256
ROUTED_SCALE = 2.5
MOE_BLOCK = 256
DN_ALPHA = 2.0 ** 0.25
LN_EPS = 1e-5
RMS_EPS = 1e-5
NEG_INF = -1e30
GN = SSD_GROUPS * SSD_STATE
CONV_CH = SSD_WIDTH + 2 * GN
XBC_OFF = SSD_WIDTH
DT_OFF = XBC_OFF + CONV_CH
Q_OFF = DT_OFF + 2 * SSD_HEADS
IN_COLS = Q_OFF + 3 * ATT_WIDTH

LANES = 128
VMEM_LIMIT_BYTES = 56 * 1024 * 1024


def _cparams(*sem):
    return pltpu.CompilerParams(dimension_semantics=sem, vmem_limit_bytes=VMEM_LIMIT_BYTES)


def _ln_rows(x, g, b):
    mu = jnp.mean(x, -1, keepdims=True)
    xc = x - mu
    var = jnp.mean(xc * xc, -1, keepdims=True)
    return xc * lax.rsqrt(var + LN_EPS) * g + b


def _dot(a, b):
    return jnp.dot(a, b, preferred_element_type=F32)


def _dot_nt(a, b):
    return lax.dot_general(a, b, (((1,), (1,)), ((), ())), preferred_element_type=F32)


def _silu(x):
    return x * jax.nn.sigmoid(x)


def _pack_bf16_pairs(x):
    n = x.shape[1] // 2
    u = lax.bitcast_convert_type(x.astype(BF16).astype(F32), jnp.uint32)
    word = (u[:, :n] >> 16) | (u[:, n:] & jnp.uint32(0xFFFF0000))
    return lax.bitcast_convert_type(word, I32)


def _unpack_bf16_pairs(w):
    u = lax.bitcast_convert_type(w, jnp.uint32)
    lo = lax.bitcast_convert_type(u << 16, F32)
    hi = lax.bitcast_convert_type(u & jnp.uint32(0xFFFF0000), F32)
    return jnp.concatenate([lo, hi], axis=1)


def _inproj_kernel(x_ref, g_ref, b_ref, wz_ref, wxbc_ref, wdt_ref, wqkv_ref,
                   h_ref, z_ref, xbc_ref, dt_ref, qkv1_ref, qkv4_ref, qkv16_ref, stage_ref):
    tm = x_ref.shape[1]
    h = _ln_rows(x_ref[0], g_ref[...], b_ref[...])
    h_ref[0] = h
    hb = h.astype(BF16)
    z_ref[0] = _dot(hb, wz_ref[...]).astype(BF16)
    for c in range(CONV_CH // D_MODEL):
        cols = slice(c * D_MODEL, (c + 1) * D_MODEL)
        xbc_ref[0, :, cols] = _dot(hb, wxbc_ref[:, cols]).astype(BF16)
    dt_ref[0] = _dot(hb, wdt_ref[...])
    for c in range(3):
        cols = slice(c * ATT_WIDTH, (c + 1) * ATT_WIDTH)
        res = _dot(hb, wqkv_ref[:, cols])
        if c == 0:
            res = res * (ATT_HEAD_DIM ** -0.5)
        qkv1_ref[c, 0] = res.astype(BF16)
        for j in range(ATT_WIDTH // LANES):
            lanes = slice(j * LANES, (j + 1) * LANES)
            stage_ref[j] = res[:, lanes]
            for r in range(4):
                qkv4_ref[c, 0, r, :, lanes] = stage_ref[j, pl.ds(r, tm // 4, stride=4), :].astype(BF16)
            for r in range(16):
                qkv16_ref[c, 0, r, :, lanes] = stage_ref[j, pl.ds(r, tm // 16, stride=16), :].astype(BF16)


def _inproj(x, g, b, wz, wxbc, wdt, wqkv, tm=256):
    bsz, s, d = x.shape
    nt = s // tm
    const = lambda *_: (0, 0)
    wspec = lambda w: pl.BlockSpec(w.shape, const)
    row = lambda n: pl.BlockSpec((1, tm, n), lambda bi, i: (bi, i, 0))
    out_shape = (
        jax.ShapeDtypeStruct((bsz, s, d), F32),
        jax.ShapeDtypeStruct((bsz, s, SSD_WIDTH), BF16),
        jax.ShapeDtypeStruct((bsz, s, CONV_CH), BF16),
        jax.ShapeDtypeStruct((bsz, s, SSD_GROUPS * LANES), F32),
        jax.ShapeDtypeStruct((3, bsz, s, ATT_WIDTH), BF16),
        jax.ShapeDtypeStruct((3, bsz, 4, s // 4, ATT_WIDTH), BF16),
        jax.ShapeDtypeStruct((3, bsz, 16, s // 16, ATT_WIDTH), BF16),
    )
    out_specs = (
        row(d), row(SSD_WIDTH), row(CONV_CH), row(SSD_GROUPS * LANES),
        pl.BlockSpec((3, 1, tm, ATT_WIDTH), lambda bi, i: (0, bi, i, 0)),
        pl.BlockSpec((3, 1, 4, tm // 4, ATT_WIDTH), lambda bi, i: (0, bi, 0, i, 0)),
        pl.BlockSpec((3, 1, 16, tm // 16, ATT_WIDTH), lambda bi, i: (0, bi, 0, i, 0)),
    )
    return pl.pallas_call(
        _inproj_kernel,
        grid=(bsz, nt),
        in_specs=[row(d), wspec(g), wspec(b), wspec(wz), wspec(wxbc), wspec(wdt), wspec(wqkv)],
        out_specs=out_specs,
        out_shape=out_shape,
        scratch_shapes=[pltpu.VMEM((ATT_WIDTH // LANES, tm, LANES), F32)],
        compiler_params=_cparams("parallel", "parallel"),
        name="ln_inproj",
    )(x, g, b, wz, wxbc, wdt, wqkv)


GROUP_X = SSD_HEADS_PER_GROUP * SSD_HEAD_DIM
GROUP_CH = GROUP_X + 2 * SSD_STATE
CONV_ROWS = 256
PAD_ROWS = 8


def _softplus(x):
    return jnp.maximum(x, 0.0) + jnp.log(1.0 + jnp.exp(-jnp.abs(x)))


def _ssd_kernel(x_ref, b_ref, c_ref, dt_ref, cw_ref, cb_ref, dtb_ref, a_ref, dsk_ref,
                y_ref, pad_ref, act_ref, yacc_ref, st_ref):
    s = x_ref.shape[1]
    q = SSD_CHUNK
    nchunks = s // q

    zeros = jnp.zeros((PAD_ROWS, GROUP_CH), F32)
    pad_ref[0:PAD_ROWS, :] = zeros
    pad_ref[s + PAD_ROWS:s + 2 * PAD_ROWS, :] = zeros
    pad_ref[PAD_ROWS:s + PAD_ROWS, 0:GROUP_X] = x_ref[0].astype(F32)
    pad_ref[PAD_ROWS:s + PAD_ROWS, GROUP_X:GROUP_X + SSD_STATE] = b_ref[0].astype(F32)
    pad_ref[PAD_ROWS:s + PAD_ROWS, GROUP_X + SSD_STATE:GROUP_CH] = c_ref[0].astype(F32)
    for i in range(s // CONV_ROWS):
        acc = jnp.broadcast_to(cb_ref[0], (CONV_ROWS, GROUP_CH))
        for k in range(SSD_CONV):
            r0 = PAD_ROWS + i * CONV_ROWS + k - SSD_CONV // 2
            acc = acc + cw_ref[0, k:k + 1, :] * pad_ref[r0:r0 + CONV_ROWS, :]
        act_ref[i * CONV_ROWS:(i + 1) * CONV_ROWS, :] = _silu(acc)

    row_i = lax.broadcasted_iota(I32, (q, q), 0)
    col_i = lax.broadcasted_iota(I32, (q, q), 1)
    lane = lax.broadcasted_iota(I32, (q, GROUP_X), 1)

    def expand(v, off):
        out = jnp.broadcast_to(v[:, off:off + 1], (q, GROUP_X))
        for r in range(1, SSD_HEADS_PER_GROUP):
            out = jnp.where(lane >= r * SSD_HEAD_DIM,
                            jnp.broadcast_to(v[:, off + r:off + r + 1], (q, GROUP_X)), out)
        return out

    st_ref[...] = jnp.zeros_like(st_ref)

    def chunk(ci, reverse):
        off = SSD_HEADS_PER_GROUP if reverse else 0
        di = 1 if reverse else 0
        tri = (row_i <= col_i) if reverse else (row_i >= col_i)
        tri_f = tri.astype(F32)
        c = (nchunks - 1 - ci) if reverse else ci
        rows = pl.ds(pl.multiple_of(c * q, q), q)
        xa = act_ref[rows, 0:GROUP_X]
        bm = act_ref[rows, GROUP_X:GROUP_X + SSD_STATE]
        cm = act_ref[rows, GROUP_X + SSD_STATE:GROUP_CH]
        dt = _softplus(dt_ref[0, rows, :] + dtb_ref[0])
        dta = dt * a_ref[0]
        cs = jnp.dot(tri_f, dta, precision=lax.Precision.HIGHEST, preferred_element_type=F32)
        cs_t = cs.T
        tot = cs[0:1, :] if reverse else cs[q - 1:q, :]
        cm_b = cm.astype(BF16)
        cbm = _dot_nt(cm_b, bm.astype(BF16))
        xdt = xa * expand(dt, off)
        xdt_b = xdt.astype(BF16)
        ydiag = None
        for r in range(SSD_HEADS_PER_GROUP):
            seg = cs[:, off + r:off + r + 1] - cs_t[off + r:off + r + 1, :]
            w = (jnp.exp(jnp.where(tri, seg, NEG_INF)) * cbm).astype(BF16)
            yr = _dot(w, xdt_b)
            ydiag = yr if r == 0 else jnp.where(lane >= r * SSD_HEAD_DIM, yr, ydiag)
        xs_b = (xdt * expand(jnp.exp(tot - cs), off)).astype(BF16)
        states = _dot(bm.T.astype(BF16), xs_b)
        hprev = st_ref[di]
        yoff = _dot(cm_b, hprev.astype(BF16)) * expand(jnp.exp(cs), off)
        st_ref[di] = hprev * expand(jnp.broadcast_to(jnp.exp(tot), (q, LANES)), off) + states
        yacc_ref[di, rows, :] = ydiag + yoff

    def body(ci, carry):
        chunk(ci, False)
        chunk(ci, True)
        return carry

    lax.fori_loop(0, nchunks, body, 0)
    y_ref[0] = (yacc_ref[0] + yacc_ref[1] + act_ref[:, 0:GROUP_X] * dsk_ref[0]).astype(BF16)


def _ssd(xbc, dt, conv_w_g, conv_b_g, dt_bias_g, a_g, dskip_g):
    bsz, s, _ = xbc.shape
    xb = GROUP_X // LANES
    grp = lambda n: pl.BlockSpec((1,) + n, lambda bi, gi: (gi, 0, 0))
    in_specs = [
        pl.BlockSpec((1, s, GROUP_X), lambda bi, gi: (bi, 0, gi)),
        pl.BlockSpec((1, s, SSD_STATE), lambda bi, gi: (bi, 0, SSD_WIDTH // SSD_STATE + gi)),
        pl.BlockSpec((1, s, SSD_STATE), lambda bi, gi: (bi, 0, (SSD_WIDTH + GN) // SSD_STATE + gi)),
        pl.BlockSpec((1, s, LANES), lambda bi, gi: (bi, 0, gi)),
        grp((SSD_CONV, GROUP_CH)), grp((1, GROUP_CH)), grp((1, LANES)), grp((1, LANES)), grp((1, GROUP_X)),
    ]
    del xb
    return pl.pallas_call(
        _ssd_kernel,
        grid=(bsz, SSD_GROUPS),
        in_specs=in_specs,
        out_specs=pl.BlockSpec((1, s, GROUP_X), lambda bi, gi: (bi, 0, gi)),
        out_shape=jax.ShapeDtypeStruct((bsz, s, SSD_WIDTH), BF16),
        scratch_shapes=[
            pltpu.VMEM((s + 2 * PAD_ROWS, GROUP_CH), F32),
            pltpu.VMEM((s, GROUP_CH), F32),
            pltpu.VMEM((2, s, GROUP_X), F32),
            pltpu.VMEM((2, SSD_STATE, GROUP_X), F32),
        ],
        compiler_params=_cparams("parallel", "parallel"),
        name="ssd_scan",
    )(xbc, xbc, xbc, dt, conv_w_g, conv_b_g, dt_bias_g, a_g, dskip_g)


ATT_QBLOCK = 128
ATT_PAIR = 2 * ATT_HEAD_DIM
ATT_WINDOW = 2 * ATT_QBLOCK


def _att_kernel(q1, k1, v1, q4, k4, v4, q16, k16, v16, slope_ref, o_ref, km_ref, vm_ref, bias_ref, ob_ref, lb_ref):
    s = o_ref.shape[1]
    qb = ATT_QBLOCK
    first = lax.broadcasted_iota(I32, (qb, ATT_PAIR), 1) < ATT_HEAD_DIM
    first_all = lax.broadcasted_iota(I32, (s, ATT_PAIR), 1) < ATT_HEAD_DIM
    slope_lanes = slope_ref[0]

    def branch(bi, dil, q_ref, k_ref, v_ref):
        seg_len = s // dil
        win = min(ATT_WINDOW, seg_len)
        kk = k_ref[0, 0]
        vv = v_ref[0, 0]
        zero = jnp.zeros_like(kk)
        km_ref[0] = jnp.where(first_all, kk, zero)
        km_ref[1] = jnp.where(first_all, zero, kk)
        vm_ref[0] = jnp.where(first_all, vv, zero)
        vm_ref[1] = jnp.where(first_all, zero, vv)
        shifts = sorted({min(max(w0 - ATT_RADIUS, 0), seg_len - win) - w0 for w0 in range(0, seg_len, qb)},
                        reverse=True)
        rel0 = lax.broadcasted_iota(I32, (qb, win), 1) - lax.broadcasted_iota(I32, (qb, win), 0)
        for vi, shift in enumerate(shifts):
            rel = jnp.abs(rel0 + shift)
            dist = (dil * rel).astype(F32)
            for j in range(2):
                slope = slope_lanes[:, j * ATT_HEAD_DIM:j * ATT_HEAD_DIM + 1]
                bias_ref[vi, :, j * win:(j + 1) * win] = jnp.where(rel <= ATT_RADIUS, -slope * dist, NEG_INF)

        for blk in range(s // qb):
            qs = blk * qb
            seg, within = divmod(qs, seg_len)
            kw = min(max(within - ATT_RADIUS, 0), seg_len - win)
            ks = seg * seg_len + kw
            vi = shifts.index(kw - within)
            kcat = jnp.concatenate([km_ref[0, ks:ks + win, :], km_ref[1, ks:ks + win, :]], axis=0)
            vcat = jnp.concatenate([vm_ref[0, ks:ks + win, :], vm_ref[1, ks:ks + win, :]], axis=0)
            sc = _dot_nt(q_ref[0, 0, qs:qs + qb, :], kcat) + bias_ref[vi, :, 0:2 * win]
            s0, s1 = sc[:, :win], sc[:, win:]
            m0 = jnp.max(s0, axis=-1, keepdims=True)
            m1 = jnp.max(s1, axis=-1, keepdims=True)
            p0 = jnp.exp(s0 - m0)
            p1 = jnp.exp(s1 - m1)
            l0 = jnp.sum(p0, axis=-1, keepdims=True)
            l1 = jnp.sum(p1, axis=-1, keepdims=True)
            o = _dot(jnp.concatenate([p0, p1], axis=1).astype(BF16), vcat)
            o_blk = o * jnp.where(first, 1.0 / l0, 1.0 / l1)
            l_blk = jnp.where(first, m0 + jnp.log(l0), m1 + jnp.log(l1))
            if dil == 1:
                dst = pl.ds(qs, qb)
            else:
                dst = pl.ds(within * dil + seg, qb, stride=dil)
            ob_ref[bi, dst, :] = o_blk
            lb_ref[bi, dst, :] = l_blk

    branch(0, DILATIONS[0], q1, k1, v1)
    branch(1, DILATIONS[1], q4, k4, v4)
    branch(2, DILATIONS[2], q16, k16, v16)

    rows = 256
    for i in range(s // rows):
        sl = slice(i * rows, (i + 1) * rows)
        l0, l1, l2 = lb_ref[0, sl, :], lb_ref[1, sl, :], lb_ref[2, sl, :]
        m = jnp.maximum(jnp.maximum(l0, l1), l2)
        w0, w1, w2 = jnp.exp(l0 - m), jnp.exp(l1 - m), jnp.exp(l2 - m)
        num = w0 * ob_ref[0, sl, :] + w1 * ob_ref[1, sl, :] + w2 * ob_ref[2, sl, :]
        o_ref[0, sl, :] = (num / (w0 + w1 + w2)).astype(BF16)


def _dilated_attention(qkv1, qkv4, qkv16, slopes):
    _, bsz, s, _ = qkv1.shape
    npairs = ATT_WIDTH // ATT_PAIR
    spec = lambda c: pl.BlockSpec((1, 1, s, ATT_PAIR), lambda bi, hp, c=c: (c, bi, 0, hp))
    in_specs = [spec(0), spec(1), spec(2)] * 3 + [pl.BlockSpec((1, 1, ATT_PAIR), lambda bi, hp: (hp, 0, 0))]
    return pl.pallas_call(
        _att_kernel,
        grid=(bsz, npairs),
        in_specs=in_specs,
        out_specs=pl.BlockSpec((1, s, ATT_PAIR), lambda bi, hp: (bi, 0, hp)),
        out_shape=jax.ShapeDtypeStruct((bsz, s, ATT_WIDTH), BF16),
        scratch_shapes=[pltpu.VMEM((2, s, ATT_PAIR), BF16), pltpu.VMEM((2, s, ATT_PAIR), BF16),
                        pltpu.VMEM((3, ATT_QBLOCK, 2 * ATT_WINDOW), F32),
                        pltpu.VMEM((3, s, ATT_PAIR), F32), pltpu.VMEM((3, s, ATT_PAIR), F32)],
        compiler_params=_cparams("parallel", "parallel"),
        name="dilated_attention",
    )(qkv1, qkv1, qkv1, qkv4, qkv4, qkv4, qkv16, qkv16, qkv16, slopes)


def _outproj_kernel(h_ref, yssd_ref, z_ref, yatt_ref, nw_ref, wo_ref, g_ref, b_ref, o_ref):
    y = yssd_ref[...].astype(F32) * _silu(z_ref[...].astype(F32))
    yn = y * lax.rsqrt(jnp.mean(y * y, -1, keepdims=True) + RMS_EPS) * nw_ref[...]
    mix = _dot(yn.astype(BF16), wo_ref[0:SSD_WIDTH, :]) + _dot(yatt_ref[...], wo_ref[SSD_WIDTH:, :])
    o_ref[...] = _ln_rows(DN_ALPHA * h_ref[...] + mix, g_ref[...], b_ref[...])


def _outproj(h, yssd, z, yatt, norm_w, w_out, g, b, tm=512):
    t, d = h.shape
    const = lambda *_: (0, 0)
    wspec = lambda w: pl.BlockSpec(w.shape, const)
    row = lambda n: pl.BlockSpec((tm, n), lambda i: (i, 0))
    return pl.pallas_call(
        _outproj_kernel,
        grid=(t // tm,),
        in_specs=[row(d), row(SSD_WIDTH), row(SSD_WIDTH), row(ATT_WIDTH),
                  wspec(norm_w), wspec(w_out), wspec(g), wspec(b)],
        out_specs=row(d),
        out_shape=jax.ShapeDtypeStruct((t, d), F32),
        compiler_params=_cparams("parallel"),
        name="outproj_ln",
    )(h, yssd, z, yatt, norm_w, w_out, g, b)


def _alibi_slopes():
    sl = np.array([2.0 ** (-8.0 * (h + 1) / ATT_HEADS) for h in range(ATT_HEADS)], dtype=np.float32)
    return jnp.asarray(np.repeat(sl, ATT_HEAD_DIM).reshape(ATT_WIDTH // ATT_PAIR, 1, ATT_PAIR))


def _mixer_params(w_in, conv_w, conv_b, dt_bias_f, dt_bias_b, a_log_f, a_log_b, d_skip):
    r = SSD_HEADS_PER_GROUP
    wz = w_in[:, 0:XBC_OFF].astype(BF16)
    wxbc = w_in[:, XBC_OFF:DT_OFF].astype(BF16)
    wqkv = w_in[:, Q_OFF:IN_COLS].astype(BF16)
    wdt_f = w_in[:, DT_OFF:DT_OFF + SSD_HEADS].reshape(D_MODEL, SSD_GROUPS, r)
    wdt_b = w_in[:, DT_OFF + SSD_HEADS:Q_OFF].reshape(D_MODEL, SSD_GROUPS, r)
    lane_pad = ((0, 0), (0, 0), (0, LANES - 2 * r))
    wdt = jnp.pad(jnp.concatenate([wdt_f, wdt_b], -1), lane_pad).reshape(D_MODEL, SSD_GROUPS * LANES).astype(BF16)

    def per_group_lanes(f, b):
        v = jnp.concatenate([f.reshape(SSD_GROUPS, 1, r), b.reshape(SSD_GROUPS, 1, r)], -1)
        return jnp.pad(v.astype(F32), lane_pad)

    dtb = per_group_lanes(dt_bias_f, dt_bias_b)
    a = per_group_lanes(-jnp.exp(a_log_f.astype(F32)), -jnp.exp(a_log_b.astype(F32)))

    def group_cols(w):
        xs = w[..., 0:SSD_WIDTH].reshape(w.shape[:-1] + (SSD_GROUPS, GROUP_X))
        bs = w[..., SSD_WIDTH:SSD_WIDTH + GN].reshape(w.shape[:-1] + (SSD_GROUPS, SSD_STATE))
        cs = w[..., SSD_WIDTH + GN:].reshape(w.shape[:-1] + (SSD_GROUPS, SSD_STATE))
        return jnp.moveaxis(jnp.concatenate([xs, bs, cs], -1), -2, 0)

    cw = group_cols(conv_w.astype(F32))
    cb = group_cols(conv_b.astype(F32)[None])
    dsk = jnp.repeat(d_skip.astype(F32), SSD_HEAD_DIM).reshape(SSD_GROUPS, 1, GROUP_X)
    return wz, wxbc, wdt, wqkv, cw, cb, dtb, a, dsk


def _mixer_layer(x, ln_in_g, ln_in_b, w_in, conv_w, conv_b, dt_bias_f, dt_bias_b, a_log_f, a_log_b,
                 d_skip, ssd_norm_w, w_out_mix, ln1_g, ln1_b):
    bsz, s, d = x.shape
    vec = lambda v: v.astype(F32).reshape(1, -1)
    wz, wxbc, wdt, wqkv, cw, cb, dtb, a, dsk = _mixer_params(
        w_in, conv_w, conv_b, dt_bias_f, dt_bias_b, a_log_f, a_log_b, d_skip)
    h, z, xbc, dt, qkv1, qkv4, qkv16 = _inproj(x, vec(ln_in_g), vec(ln_in_b), wz, wxbc, wdt, wqkv)
    yssd = _ssd(xbc, dt, cw, cb, dtb, a, dsk)
    yatt = _dilated_attention(qkv1, qkv4.reshape(qkv1.shape), qkv16.reshape(qkv1.shape), _alibi_slopes())
    t = bsz * s
    return _outproj(h.reshape(t, d), yssd.reshape(t, SSD_WIDTH), z.reshape(t, SSD_WIDTH),
                    yatt.reshape(t, ATT_WIDTH), vec(ssd_norm_w), w_out_mix.astype(BF16), vec(ln1_g), vec(ln1_b))


def _memkv_kernel(mem_ref, wk_ref, wv_ref, k_ref, v_ref):
    m = mem_ref[0].astype(BF16)
    k_ref[0] = _dot(m, wk_ref[...]).astype(BF16)
    v_ref[0] = _dot(m, wv_ref[...]).astype(BF16)


def _memkv(mem, wk, wv):
    bsz, m, d = mem.shape
    const = lambda *_: (0, 0)
    blk = pl.BlockSpec((1, m, d), lambda bi: (bi, 0, 0))
    return pl.pallas_call(
        _memkv_kernel,
        grid=(bsz,),
        in_specs=[blk, pl.BlockSpec(wk.shape, const), pl.BlockSpec(wv.shape, const)],
        out_specs=(blk, blk),
        out_shape=(jax.ShapeDtypeStruct((bsz, m, d), BF16),) * 2,
        compiler_params=_cparams("parallel"),
        name="mem_kv",
    )(mem, wk, wv)


def _memattn_kernel(h_ref, k_ref, v_ref, wq_ref, wo_ref, g_ref, b_ref, o_ref, ow_ref):
    h = h_ref[0]
    q = (_dot(h.astype(BF16), wq_ref[...]) * (MEM_HEAD_DIM ** -0.5)).astype(BF16)
    xa = None
    for hd in range(MEM_HEADS):
        cols = slice(hd * MEM_HEAD_DIM, (hd + 1) * MEM_HEAD_DIM)
        sc = _dot_nt(q[:, cols], k_ref[0, :, cols])
        p = jnp.exp(sc - jnp.max(sc, axis=-1, keepdims=True))
        o = _dot(p.astype(BF16), v_ref[0, :, cols]) / jnp.sum(p, axis=-1, keepdims=True)
        part = _dot(o.astype(BF16), wo_ref[cols, :])
        xa = part if xa is None else xa + part
    out = _ln_rows(DN_ALPHA * h + xa, g_ref[...], b_ref[...])
    o_ref[0] = out
    ow_ref[0] = _pack_bf16_pairs(out)


def _memattn(h, k, v, wq, wo, g, b, tm=512):
    bsz, s, d = h.shape
    m = k.shape[1]
    const = lambda *_: (0, 0)
    wspec = lambda w: pl.BlockSpec(w.shape, const)
    row = pl.BlockSpec((1, tm, d), lambda bi, i: (bi, i, 0))
    kv = pl.BlockSpec((1, m, d), lambda bi, i: (bi, 0, 0))
    return pl.pallas_call(
        _memattn_kernel,
        grid=(bsz, s // tm),
        in_specs=[row, kv, kv, wspec(wq), wspec(wo), wspec(g), wspec(b)],
        out_specs=(row, pl.BlockSpec((1, tm, d // 2), lambda bi, i: (bi, i, 0))),
        out_shape=(jax.ShapeDtypeStruct((bsz, s, d), F32), jax.ShapeDtypeStruct((bsz, s, d // 2), I32)),
        compiler_params=_cparams("parallel", "parallel"),
        name="mem_attn_ln",
    )(h, k, v, wq, wo, g, b)


def _first_max(work, idx_iota, sentinel):
    m = jnp.max(work, axis=0, keepdims=True)
    idx = jnp.min(jnp.where(work == m, idx_iota, sentinel), axis=0, keepdims=True)
    return m, idx


def _router_kernel(h_ref, wrt_ref, bias_ref, eidx_ref, gate_ref, rank_ref, cnt_ref, run_ref):
    tm = h_ref.shape[0]
    e = N_EXPERTS

    @pl.when(pl.program_id(0) == 0)
    def _():
        run_ref[...] = jnp.zeros_like(run_ref)

    logits = lax.dot_general(wrt_ref[...], h_ref[...], (((1,), (1,)), ((), ())),
                             precision=lax.Precision.HIGHEST, preferred_element_type=F32)
    scores = jax.nn.sigmoid(logits)
    choice = scores + bias_ref[...]
    row = lax.broadcasted_iota(I32, (e, tm), 0)
    grow = lax.broadcasted_iota(I32, (GROUP_SIZE, tm), 0)
    gs = []
    for g in range(ROUTE_GROUPS):
        blk = choice[g * GROUP_SIZE:(g + 1) * GROUP_SIZE, :]
        m1, i1 = _first_max(blk, grow, GROUP_SIZE)
        m2 = jnp.max(jnp.where(grow == i1, -jnp.inf, blk), axis=0, keepdims=True)
        gs.append(m1 + m2)
    work = jnp.concatenate(gs, axis=0)
    giota = lax.broadcasted_iota(I32, (ROUTE_GROUPS, tm), 0)
    gmask = jnp.zeros((ROUTE_GROUPS, tm), jnp.bool_)
    for _ in range(TOPK_GROUPS):
        _, gi = _first_max(work, giota, ROUTE_GROUPS)
        hit = giota == gi
        gmask = jnp.logical_or(gmask, hit)
        work = jnp.where(hit, -jnp.inf, work)
    work = jnp.concatenate(
        [jnp.where(gmask[g:g + 1, :], choice[g * GROUP_SIZE:(g + 1) * GROUP_SIZE, :], NEG_INF)
         for g in range(ROUTE_GROUPS)], axis=0)
    chosen = jnp.zeros((e, tm), jnp.bool_)
    idxs, svals = [], []
    for _ in range(TOP_K):
        _, ei = _first_max(work, row, e)
        hit = row == ei
        svals.append(jnp.sum(jnp.where(hit, scores, 0.0), axis=0, keepdims=True))
        idxs.append(ei)
        chosen = jnp.logical_or(chosen, hit)
        work = jnp.where(hit, -jnp.inf, work)
    eidx = jnp.concatenate(idxs, axis=0)
    sv = jnp.concatenate(svals, axis=0)
    eidx_ref[...] = eidx
    gate_ref[...] = sv / jnp.sum(sv, axis=0, keepdims=True) * ROUTED_SCALE
    upper = (lax.broadcasted_iota(I32, (tm, tm), 0) < lax.broadcasted_iota(I32, (tm, tm), 1))
    mt = chosen.astype(F32)
    cum = _dot(mt.astype(BF16), upper.astype(F32).astype(BF16)) + run_ref[...]
    rank_ref[...] = jnp.concatenate(
        [jnp.sum(jnp.where(row == idxs[k], cum, 0.0), axis=0, keepdims=True) for k in range(TOP_K)],
        axis=0).astype(I32)
    run_ref[...] += jnp.sum(mt, axis=1, keepdims=True)
    cnt_ref[...] = run_ref[...]


def _router(h, w_router_t, bias_col, tm=512):
    t, d = h.shape
    const = lambda *_: (0, 0)
    col = lambda dt: pl.BlockSpec((TOP_K, tm), lambda i: (0, i))
    return pl.pallas_call(
        _router_kernel,
        grid=(t // tm,),
        in_specs=[pl.BlockSpec((tm, d), lambda i: (i, 0)), pl.BlockSpec(w_router_t.shape, const),
                  pl.BlockSpec(bias_col.shape, const)],
        out_specs=(col(I32), col(F32), col(I32), pl.BlockSpec((N_EXPERTS, 1), const)),
        out_shape=(jax.ShapeDtypeStruct((TOP_K, t), I32), jax.ShapeDtypeStruct((TOP_K, t), F32),
                   jax.ShapeDtypeStruct((TOP_K, t), I32), jax.ShapeDtypeStruct((N_EXPERTS, 1), F32)),
        scratch_shapes=[pltpu.VMEM((N_EXPERTS, 1), F32)],
        compiler_params=_cparams("arbitrary"),
        name="router_topk",
    )(h, w_router_t, bias_col)


def _dispatch_kernel(cnt_ref, eidx_ref, rank_ref, dest_ref, bexp_ref, nused_ref):
    tm = eidx_ref.shape[1]
    e = N_EXPERTS
    nblk = bexp_ref.shape[1]
    nb = jnp.floor((cnt_ref[...] + (MOE_BLOCK - 1)) * (1.0 / MOE_BLOCK))
    strict = (lax.broadcasted_iota(I32, (e, e), 1) < lax.broadcasted_iota(I32, (e, e), 0))
    bstart = _dot(strict.astype(F32).astype(BF16), jnp.broadcast_to(nb, (e, LANES)).astype(BF16))[:, 0:1]
    pstart = bstart * MOE_BLOCK
    row = lax.broadcasted_iota(I32, (e, tm), 0)
    eidx = eidx_ref[...]
    rank = rank_ref[...]
    dest_ref[...] = jnp.concatenate(
        [jnp.sum(jnp.where(row == eidx[k:k + 1, :], pstart, 0.0), axis=0, keepdims=True).astype(I32)
         + rank[k:k + 1, :] for k in range(TOP_K)], axis=0)
    bend = bstart + nb
    jb = lax.broadcasted_iota(I32, (e, nblk), 1).astype(F32)
    nle = jnp.sum((bend <= jb).astype(F32), axis=0, keepdims=True)
    bexp_ref[...] = jnp.minimum(nle, e - 1).astype(I32)
    nused_ref[...] = jnp.broadcast_to(bend[e - 1:e, :], (1, LANES)).astype(I32)


def _dispatch(cnt, eidx, rank, n_blocks, tm=2048):
    t = eidx.shape[1]
    const = lambda *_: (0, 0)
    col = pl.BlockSpec((TOP_K, tm), lambda i: (0, i))
    return pl.pallas_call(
        _dispatch_kernel,
        grid=(t // tm,),
        in_specs=[pl.BlockSpec(cnt.shape, const), col, col],
        out_specs=(col, pl.BlockSpec((1, n_blocks), const), pl.BlockSpec((1, LANES), const)),
        out_shape=(jax.ShapeDtypeStruct((TOP_K, t), I32), jax.ShapeDtypeStruct((1, n_blocks), I32),
                   jax.ShapeDtypeStruct((1, LANES), I32)),
        compiler_params=_cparams("arbitrary"),
        name="dispatch_slots",
    )(cnt, eidx, rank)


def _experts_kernel(bexp_ref, nused_ref, xs_ref, wg_ref, wu_ref, wd_ref, ys_ref):
    @pl.when(pl.program_id(0) < nused_ref[0])
    def _():
        x = _unpack_bf16_pairs(xs_ref[...]).astype(BF16)
        g = _dot(x, wg_ref[0].astype(BF16))
        u = _dot(x, wu_ref[0].astype(BF16))
        a = (_silu(g) * u).astype(BF16)
        ys_ref[...] = _pack_bf16_pairs(_dot(a, wd_ref[0].astype(BF16)))


def _experts(bexp, nused, xs, w_gate, w_up, w_down):
    n_pad, words = xs.shape
    n_blocks = n_pad // MOE_BLOCK
    last = lambda j, nu: jnp.minimum(j, nu[0] - 1)
    rows = pl.BlockSpec((MOE_BLOCK, words), lambda j, be, nu: (last(j, nu), 0))
    wspec = lambda w: pl.BlockSpec((1,) + w.shape[1:], lambda j, be, nu: (be[last(j, nu)], 0, 0))
    return pl.pallas_call(
        _experts_kernel,
        grid_spec=pltpu.PrefetchScalarGridSpec(
            num_scalar_prefetch=2,
            grid=(n_blocks,),
            in_specs=[rows, wspec(w_gate), wspec(w_up), wspec(w_down)],
            out_specs=rows,
        ),
        out_shape=jax.ShapeDtypeStruct((n_pad, words), I32),
        compiler_params=_cparams("arbitrary"),
        name="routed_experts",
    )(bexp, nused, xs, w_gate, w_up, w_down)


def _final_kernel(h_ref, yg_ref, gate_ref, wsg_ref, wsu_ref, wsd_ref, g_ref, b_ref, o_ref):
    h = h_ref[...]
    d = h.shape[1]
    hb = h.astype(BF16)
    ff = _dot((_silu(_dot(hb, wsg_ref[...])) * _dot(hb, wsu_ref[...])).astype(BF16), wsd_ref[...])
    gate = gate_ref[...]
    words = d // 2
    for k in range(TOP_K):
        ff = ff + gate[:, k:k + 1] * _unpack_bf16_pairs(yg_ref[:, k * words:(k + 1) * words])
    o_ref[...] = _ln_rows(DN_ALPHA * h + ff, g_ref[...], b_ref[...])


def _final(h, yg, gate, wsg, wsu, wsd, g, b, tm=256):
    t, d = h.shape
    const = lambda *_: (0, 0)
    wspec = lambda w: pl.BlockSpec(w.shape, const)
    row = lambda n: pl.BlockSpec((tm, n), lambda i: (i, 0))
    return pl.pallas_call(
        _final_kernel,
        grid=(t // tm,),
        in_specs=[row(d), row(TOP_K * d // 2), row(TOP_K), wspec(wsg), wspec(wsu), wspec(wsd), wspec(g), wspec(b)],
        out_specs=row(d),
        out_shape=jax.ShapeDtypeStruct((t, d), F32),
        compiler_params=_cparams("parallel"),
        name="shared_combine_ln",
    )(h, yg, gate, wsg, wsu, wsd, g, b)


SC_CORES = 2
SC_SUBCORES = 16
SC_WORKERS = SC_CORES * SC_SUBCORES
SC_WINDOW = 64


def _sc_scatter_rows(x_words, dest_kmajor, n_rows):
    t, w = x_words.shape
    per_worker = t // SC_WORKERS
    nchunk = per_worker // SC_WINDOW
    mesh = plsc.VectorSubcoreMesh(core_axis_name="c", subcore_axis_name="s")

    @functools.partial(
        pl.kernel, mesh=mesh,
        out_type=jax.ShapeDtypeStruct((n_rows, w), I32),
        scratch_types=[pltpu.VMEM((SC_WINDOW,), I32), pltpu.VMEM((SC_WINDOW, w), I32), pltpu.SemaphoreType.DMA],
    )
    def scatter(x_hbm, dest_hbm, out_hbm, idx_v, rows_v, sem):
        wid = lax.axis_index("s") * SC_CORES + lax.axis_index("c")
        base = wid * per_worker

        @pl.loop(0, nchunk)
        def _(i):
            t0 = base + i * SC_WINDOW
            pltpu.sync_copy(x_hbm.at[pl.ds(t0, SC_WINDOW)], rows_v)
            for k in range(TOP_K):
                pltpu.sync_copy(dest_hbm.at[pl.ds(k * t + t0, SC_WINDOW)], idx_v)
                pltpu.async_copy(rows_v, out_hbm.at[idx_v], sem).wait()

    return scatter(x_words, dest_kmajor)


def _sc_gather_rows(table, idx):
    a = idx.shape[0]
    w = table.shape[1]
    per_worker = a // SC_WORKERS
    nchunk = per_worker // SC_WINDOW
    mesh = plsc.VectorSubcoreMesh(core_axis_name="c", subcore_axis_name="s")

    @functools.partial(
        pl.kernel, mesh=mesh,
        out_type=jax.ShapeDtypeStruct((a, w), I32),
        scratch_types=[pltpu.VMEM((SC_WINDOW,), I32), pltpu.VMEM((SC_WINDOW, w), I32), pltpu.SemaphoreType.DMA],
    )
    def gather(table_hbm, idx_hbm, out_hbm, idx_v, rows_v, sem):
        wid = lax.axis_index("s") * SC_CORES + lax.axis_index("c")
        base = wid * per_worker

        @pl.loop(0, nchunk)
        def _(i):
            off = base + i * SC_WINDOW
            pltpu.sync_copy(idx_hbm.at[pl.ds(off, SC_WINDOW)], idx_v)
            pltpu.async_copy(table_hbm.at[idx_v], rows_v, sem).wait()
            pltpu.sync_copy(rows_v, out_hbm.at[pl.ds(off, SC_WINDOW)])

    return gather(table, idx)


def _moe_layer(h2, h2_words, w_router, router_bias, w_gate, w_up, w_down, ws_gate, ws_up, ws_down, ln3_g, ln3_b):
    t, d = h2.shape
    vec = lambda v: v.astype(F32).reshape(1, -1)
    eidx, gate, rank, cnt = _router(h2, w_router.astype(F32).T, router_bias.astype(F32).reshape(-1, 1))
    n_blocks = t * TOP_K // MOE_BLOCK + N_EXPERTS
    dest, bexp, nused = _dispatch(cnt, eidx, rank, n_blocks)
    xs = _sc_scatter_rows(h2_words, dest.reshape(TOP_K * t), n_blocks * MOE_BLOCK)
    ys = _experts(bexp.reshape(n_blocks), nused[0, 0:1], xs, w_gate, w_up, w_down)
    yg = _sc_gather_rows(ys, dest.T.reshape(t * TOP_K)).reshape(t, TOP_K * d // 2)
    return _final(h2, yg, gate.T, ws_gate.astype(BF16), ws_up.astype(BF16), ws_down.astype(BF16),
                  vec(ln3_g), vec(ln3_b))


def kernel(x, mem, ln_in_g, ln_in_b, w_in, conv_w, conv_b, dt_bias_f, dt_bias_b, a_log_f, a_log_b, d_skip, ssd_norm_w, w_out_mix, ln1_g, ln1_b, wq_mem, wk_mem, wv_mem, wo_mem, ln2_g, ln2_b, w_router, router_bias, w_gate, w_up, w_down, ws_gate, ws_up, ws_down, ln3_g, ln3_b):
    l = 0
    h1 = _mixer_layer(x, ln_in_g, ln_in_b, w_in[l], conv_w[l], conv_b[l], dt_bias_f[l], dt_bias_b[l],
                      a_log_f[l], a_log_b[l], d_skip[l], ssd_norm_w[l], w_out_mix[l], ln1_g[l], ln1_b[l])
    bsz, s, d = x.shape
    vec = lambda v: v.astype(F32).reshape(1, -1)
    km, vm = _memkv(mem, wk_mem[l].astype(BF16), wv_mem[l].astype(BF16))
    h2, h2_words = _memattn(h1.reshape(bsz, s, d), km, vm, wq_mem[l].astype(BF16), wo_mem[l].astype(BF16),
                            vec(ln2_g[l]), vec(ln2_b[l]))
    out = _moe_layer(h2.reshape(bsz * s, d), h2_words.reshape(bsz * s, d // 2), w_router[l], router_bias[l], w_gate[l], w_up[l], w_down[l],
                     ws_gate[l], ws_up[l], ws_down[l], ln3_g[l], ln3_b[l])
    return out.reshape(x.shape)
```

```python
import functools
import math

import numpy as np
import jax
import jax.numpy as jnp
from jax import lax
from jax.experimental import pallas as pl
from jax.experimental.pallas import tpu as pltpu
from jax.experimental.pallas import tpu_sc as plsc

F32 = jnp.float32
BF16 = jnp.bfloat16
I32 = jnp.int32

D_MODEL = 1024
SSD_WIDTH = 1024
SSD_HEAD_DIM = 64
SSD_HEADS = 16
SSD_STATE = 128
SSD_GROUPS = 4
SSD_HEADS_PER_GROUP = SSD_HEADS // SSD_GROUPS
SSD_CONV = 5
SSD_CHUNK = 128
ATT_WIDTH = 1024
ATT_HEAD_DIM = 64
ATT_HEADS = 16
ATT_RADIUS = 64
DILATIONS = (1, 4, 16)
MEM_HEADS = 4
MEM_HEAD_DIM = 256
N_EXPERTS = 256
ROUTE_GROUPS = 8
GROUP_SIZE = N_EXPERTS // ROUTE_GROUPS
TOPK_GROUPS = 4
TOP_K = 8
EXPERT_DIM = 256
ROUTED_SCALE = 2.5
MOE_BLOCK = 256
DN_ALPHA = 2.0 ** 0.25
LN_EPS = 1e-5
RMS_EPS = 1e-5
NEG_INF = -1e30
GN = SSD_GROUPS * SSD_STATE
CONV_CH = SSD_WIDTH + 2 * GN
XBC_OFF = SSD_WIDTH
DT_OFF = XBC_OFF + CONV_CH
Q_OFF = DT_OFF + 2 * SSD_HEADS
IN_COLS = Q_OFF + 3 * ATT_WIDTH

LANES = 128
VMEM_LIMIT_BYTES = 56 * 1024 * 1024


def _cparams(*sem):
    return pltpu.CompilerParams(dimension_semantics=sem, vmem_limit_bytes=VMEM_LIMIT_BYTES)


def _ln_rows(x, g, b):
    mu = jnp.mean(x, -1, keepdims=True)
    xc = x - mu
    var = jnp.mean(xc * xc, -1, keepdims=True)
    return xc * lax.rsqrt(var + LN_EPS) * g + b


def _dot(a, b):
    return jnp.dot(a, b, preferred_element_type=F32)


def _dot_nt(a, b):
    return lax.dot_general(a, b, (((1,), (1,)), ((), ())), preferred_element_type=F32)


def _silu(x):
    return x * jax.nn.sigmoid(x)


def _pack_bf16_pairs(x):
    n = x.shape[1] // 2
    u = lax.bitcast_convert_type(x.astype(BF16).astype(F32), jnp.uint32)
    word = (u[:, :n] >> 16) | (u[:, n:] & jnp.uint32(0xFFFF0000))
    return lax.bitcast_convert_type(word, I32)


def _unpack_bf16_pairs(w):
    u = lax.bitcast_convert_type(w, jnp.uint32)
    lo = lax.bitcast_convert_type(u << 16, F32)
    hi = lax.bitcast_convert_type(u & jnp.uint32(0xFFFF0000), F32)
    return jnp.concatenate([lo, hi], axis=1)


def _inproj_kernel(x_ref, g_ref, b_ref, wz_ref, wxbc_ref, wdt_ref, wqkv_ref,
                   h_ref, z_ref, xbc_ref, dt_ref, qkv1_ref, qkv4_ref, qkv16_ref, stage_ref):
    tm = x_ref.shape[1]
    h = _ln_rows(x_ref[0], g_ref[...], b_ref[...])
    h_ref[0] = h
    hb = h.astype(BF16)
    z_ref[0] = _dot(hb, wz_ref[...]).astype(BF16)
    for c in range(CONV_CH // D_MODEL):
        cols = slice(c * D_MODEL, (c + 1) * D_MODEL)
        xbc_ref[0, :, cols] = _dot(hb, wxbc_ref[:, cols]).astype(BF16)
    dt_ref[0] = _dot(hb, wdt_ref[...])
    for c in range(3):
        cols = slice(c * ATT_WIDTH, (c + 1) * ATT_WIDTH)
        res = _dot(hb, wqkv_ref[:, cols])
        if c == 0:
            res = res * (ATT_HEAD_DIM ** -0.5)
        qkv1_ref[c, 0] = res.astype(BF16)
        for j in range(ATT_WIDTH // LANES):
            lanes = slice(j * LANES, (j + 1) * LANES)
            stage_ref[j] = res[:, lanes]
            for r in range(4):
                qkv4_ref[c, 0, r, :, lanes] = stage_ref[j, pl.ds(r, tm // 4, stride=4), :].astype(BF16)
            for r in range(16):
                qkv16_ref[c, 0, r, :, lanes] = stage_ref[j, pl.ds(r, tm // 16, stride=16), :].astype(BF16)


def _inproj(x, g, b, wz, wxbc, wdt, wqkv, tm=256):
    bsz, s, d = x.shape
    nt = s // tm
    const = lambda *_: (0, 0)
    wspec = lambda w: pl.BlockSpec(w.shape, const)
    row = lambda n: pl.BlockSpec((1, tm, n), lambda bi, i: (bi, i, 0))
    out_shape = (
        jax.ShapeDtypeStruct((bsz, s, d), F32),
        jax.ShapeDtypeStruct((bsz, s, SSD_WIDTH), BF16),
        jax.ShapeDtypeStruct((bsz, s, CONV_CH), BF16),
        jax.ShapeDtypeStruct((bsz, s, SSD_GROUPS * LANES), F32),
        jax.ShapeDtypeStruct((3, bsz, s, ATT_WIDTH), BF16),
        jax.ShapeDtypeStruct((3, bsz, 4, s // 4, ATT_WIDTH), BF16),
        jax.ShapeDtypeStruct((3, bsz, 16, s // 16, ATT_WIDTH), BF16),
    )
    out_specs = (
        row(d), row(SSD_WIDTH), row(CONV_CH), row(SSD_GROUPS * LANES),
        pl.BlockSpec((3, 1, tm, ATT_WIDTH), lambda bi, i: (0, bi, i, 0)),
        pl.BlockSpec((3, 1, 4, tm // 4, ATT_WIDTH), lambda bi, i: (0, bi, 0, i, 0)),
        pl.BlockSpec((3, 1, 16, tm // 16, ATT_WIDTH), lambda bi, i: (0, bi, 0, i, 0)),
    )
    return pl.pallas_call(
        _inproj_kernel,
        grid=(bsz, nt),
        in_specs=[row(d), wspec(g), wspec(b), wspec(wz), wspec(wxbc), wspec(wdt), wspec(wqkv)],
        out_specs=out_specs,
        out_shape=out_shape,
        scratch_shapes=[pltpu.VMEM((ATT_WIDTH // LANES, tm, LANES), F32)],
        compiler_params=_cparams("parallel", "parallel"),
        name="ln_inproj",
    )(x, g, b, wz, wxbc, wdt, wqkv)


GROUP_X = SSD_HEADS_PER_GROUP * SSD_HEAD_DIM
GROUP_CH = GROUP_X + 2 * SSD_STATE
CONV_ROWS = 256
PAD_ROWS = 8
SSD_UNROLL = 2


def _softplus(x):
    return jnp.maximum(x, 0.0) + jnp.log(1.0 + jnp.exp(-jnp.abs(x)))


def _ssd_constants():
    q = SSD_CHUNK
    r = SSD_HEADS_PER_GROUP
    i = np.arange(q)
    tris = [(i[:, None] >= i[None, :]), (i[:, None] <= i[None, :])]
    tri_cat = np.stack([np.concatenate([t, t], axis=1) for t in tris])
    trit_cat = np.stack([np.concatenate([t.T, t.T], axis=0) for t in tris])

    def selector(width, per_head):
        src = np.arange(2 * LANES)[:, None] % LANES
        head = np.arange(width)[None, :] // per_head
        return np.stack([src == head + off for off in (0, r)])

    as_bf16 = lambda m: jnp.asarray(m.astype(np.float32), dtype=BF16)
    return (as_bf16(tri_cat), as_bf16(trit_cat), as_bf16(selector(GROUP_X, SSD_HEAD_DIM)),
            as_bf16(selector(r * LANES, LANES)))


def _ssd_kernel(x_ref, b_ref, c_ref, dt_ref, cw_ref, cb_ref, dtb_ref, a_ref, ax_ref, aw_ref, dsk_ref,
                tric_ref, tritc_ref, selx_ref, selw_ref, y_ref, pad_ref, act_ref, yacc_ref, st_ref):
    s = x_ref.shape[1]
    q = SSD_CHUNK
    nchunks = s // q

    zeros = jnp.zeros((PAD_ROWS, GROUP_CH), F32)
    pad_ref[0:PAD_ROWS, :] = zeros
    pad_ref[s + PAD_ROWS:s + 2 * PAD_ROWS, :] = zeros
    pad_ref[PAD_ROWS:s + PAD_ROWS, 0:GROUP_X] = x_ref[0].astype(F32)
    pad_ref[PAD_ROWS:s + PAD_ROWS, GROUP_X:GROUP_X + SSD_STATE] = b_ref[0].astype(F32)
    pad_ref[PAD_ROWS:s + PAD_ROWS, GROUP_X + SSD_STATE:GROUP_CH] = c_ref[0].astype(F32)
    for i in range(s // CONV_ROWS):
        acc = jnp.broadcast_to(cb_ref[0], (CONV_ROWS, GROUP_CH))
        for k in range(SSD_CONV):
            r0 = PAD_ROWS + i * CONV_ROWS + k - SSD_CONV // 2
            acc = acc + cw_ref[0, k:k + 1, :] * pad_ref[r0:r0 + CONV_ROWS, :]
        act_ref[i * CONV_ROWS:(i + 1) * CONV_ROWS, :] = _silu(acc)

    row_i = lax.broadcasted_iota(I32, (q, q), 0)
    col_i = lax.broadcasted_iota(I32, (q, q), 1)
    lane = lax.broadcasted_iota(I32, (q, GROUP_X), 1)

    def split(v):
        hi = v.astype(BF16)
        return hi, (v - hi.astype(F32)).astype(BF16)

    st_ref[...] = jnp.zeros_like(st_ref)

    def chunk(ci, reverse):
        di = 1 if reverse else 0
        off = SSD_HEADS_PER_GROUP if reverse else 0
        tri = (row_i <= col_i) if reverse else (row_i >= col_i)
        k = dict(tri_cat=tric_ref[di], trit_cat=tritc_ref[di], sel_x=selx_ref[di], sel_w=selw_ref[di])
        c = (nchunks - 1 - ci) if reverse else ci
        rows = pl.ds(pl.multiple_of(c * q, q), q)
        xa = act_ref[rows, 0:GROUP_X]
        bm = act_ref[rows, GROUP_X:GROUP_X + SSD_STATE]
        cm = act_ref[rows, GROUP_X + SSD_STATE:GROUP_CH]
        dt = _softplus(dt_ref[0, rows, :] + dtb_ref[0])
        dcat = jnp.concatenate(split(dt), axis=1)
        dt_x = _dot(dcat, k["sel_x"])
        dt_w = _dot(dcat, k["sel_w"])
        cs_x = _dot(k["tri_cat"], jnp.concatenate(split(dt_x * ax_ref[0, di]), axis=0))
        cs_w = _dot(k["tri_cat"], jnp.concatenate(split(dt_w * aw_ref[0, di]), axis=0))
        dta_t = (dt * a_ref[0]).T[off:off + 8, :]
        cs_t = _dot(jnp.concatenate(split(dta_t), axis=1), k["trit_cat"])
        tot = cs_x[0:1, :] if reverse else cs_x[q - 1:q, :]
        cm_b = cm.astype(BF16)
        cbm = _dot_nt(cm_b, bm.astype(BF16))
        xdt = xa * dt_x
        xdt_b = xdt.astype(BF16)
        ydiag = None
        for r in range(SSD_HEADS_PER_GROUP):
            seg = cs_w[:, r * LANES:(r + 1) * LANES] - cs_t[r:r + 1, :]
            w = (jnp.exp(jnp.where(tri, seg, NEG_INF)) * cbm).astype(BF16)
            yr = _dot(w, xdt_b)
            ydiag = yr if r == 0 else jnp.where(lane >= r * SSD_HEAD_DIM, yr, ydiag)
        xs_b = (xdt * jnp.exp(tot - cs_x)).astype(BF16)
        states = _dot(bm.T.astype(BF16), xs_b)
        hprev = st_ref[di]
        yoff = _dot(cm_b, hprev.astype(BF16)) * jnp.exp(cs_x)
        st_ref[di] = hprev * jnp.exp(tot) + states
        yacc_ref[di, rows, :] = ydiag + yoff

    def body(ci, carry):
        chunk(ci, False)
        chunk(ci, True)
        return carry

    lax.fori_loop(0, nchunks, body, 0, unroll=SSD_UNROLL)
    y_ref[0] = (yacc_ref[0] + yacc_ref[1] + act_ref[:, 0:GROUP_X] * dsk_ref[0]).astype(BF16)


def _ssd(xbc, dt, conv_w_g, conv_b_g, dt_bias_g, a_g, dskip_g):
    bsz, s, _ = xbc.shape
    a_lanes, a_x, a_wide = a_g
    grp = lambda n: pl.BlockSpec((1,) + n, lambda bi, gi: (gi,) + (0,) * len(n))
    in_specs = [
        pl.BlockSpec((1, s, GROUP_X), lambda bi, gi: (bi, 0, gi)),
        pl.BlockSpec((1, s, SSD_STATE), lambda bi, gi: (bi, 0, SSD_WIDTH // SSD_STATE + gi)),
        pl.BlockSpec((1, s, SSD_STATE), lambda bi, gi: (bi, 0, (SSD_WIDTH + GN) // SSD_STATE + gi)),
        pl.BlockSpec((1, s, LANES), lambda bi, gi: (bi, 0, gi)),
        grp((SSD_CONV, GROUP_CH)), grp((1, GROUP_CH)), grp((1, LANES)), grp((1, LANES)),
        grp((2, 1, GROUP_X)), grp((2, 1, SSD_HEADS_PER_GROUP * LANES)), grp((1, GROUP_X)),
    ]
    consts = _ssd_constants()
    in_specs += [pl.BlockSpec(c.shape, lambda bi, gi: (0, 0, 0)) for c in consts]
    return pl.pallas_call(
        _ssd_kernel,
        grid=(bsz, SSD_GROUPS),
        in_specs=in_specs,
        out_specs=pl.BlockSpec((1, s, GROUP_X), lambda bi, gi: (bi, 0, gi)),
        out_shape=jax.ShapeDtypeStruct((bsz, s, SSD_WIDTH), BF16),
        scratch_shapes=[
            pltpu.VMEM((s + 2 * PAD_ROWS, GROUP_CH), F32),
            pltpu.VMEM((s, GROUP_CH), F32),
            pltpu.VMEM((2, s, GROUP_X), F32),
            pltpu.VMEM((2, SSD_STATE, GROUP_X), F32),
        ],
        compiler_params=_cparams("parallel", "parallel"),
        name="ssd_scan",
    )(xbc, xbc, xbc, dt, conv_w_g, conv_b_g, dt_bias_g, a_lanes, a_x, a_wide, dskip_g, *consts)


ATT_QBLOCK = 128
ATT_PAIR = 2 * ATT_HEAD_DIM
ATT_WINDOW = 2 * ATT_QBLOCK


def _att_kernel(q1, k1, v1, q4, k4, v4, q16, k16, v16, slope_ref, o_ref, km_ref, vm_ref, bias_ref, ob_ref, lb_ref):
    s = o_ref.shape[1]
    qb = ATT_QBLOCK
    first = lax.broadcasted_iota(I32, (qb, ATT_PAIR), 1) < ATT_HEAD_DIM
    first_all = lax.broadcasted_iota(I32, (s, ATT_PAIR), 1) < ATT_HEAD_DIM
    slope_lanes = slope_ref[0]

    def branch(bi, dil, q_ref, k_ref, v_ref):
        seg_len = s // dil
        win = min(ATT_WINDOW, seg_len)
        kk = k_ref[0, 0]
        vv = v_ref[0, 0]
        zero = jnp.zeros_like(kk)
        km_ref[0] = jnp.where(first_all, kk, zero)
        km_ref[1] = jnp.where(first_all, zero, kk)
        vm_ref[0] = jnp.where(first_all, vv, zero)
        vm_ref[1] = jnp.where(first_all, zero, vv)
        shifts = sorted({min(max(w0 - ATT_RADIUS, 0), seg_len - win) - w0 for w0 in range(0, seg_len, qb)},
                        reverse=True)
        rel0 = lax.broadcasted_iota(I32, (qb, win), 1) - lax.broadcasted_iota(I32, (qb, win), 0)
        for vi, shift in enumerate(shifts):
            rel = jnp.abs(rel0 + shift)
            dist = (dil * rel).astype(F32)
            for j in range(2):
                slope = slope_lanes[:, j * ATT_HEAD_DIM:j * ATT_HEAD_DIM + 1]
                bias_ref[vi, :, j * win:(j + 1) * win] = jnp.where(rel <= ATT_RADIUS, -slope * dist, NEG_INF)

        for blk in range(s // qb):
            qs = blk * qb
            seg, within = divmod(qs, seg_len)
            kw = min(max(within - ATT_RADIUS, 0), seg_len - win)
            ks = seg * seg_len + kw
            vi = shifts.index(kw - within)
            kcat = jnp.concatenate([km_ref[0, ks:ks + win, :], km_ref[1, ks:ks + win, :]], axis=0)
            vcat = jnp.concatenate([vm_ref[0, ks:ks + win, :], vm_ref[1, ks:ks + win, :]], axis=0)
            sc = _dot_nt(q_ref[0, 0, qs:qs + qb, :], kcat) + bias_ref[vi, :, 0:2 * win]
            s0, s1 = sc[:, :win], sc[:, win:]
            m0 = jnp.max(s0, axis=-1, keepdims=True)
            m1 = jnp.max(s1, axis=-1, keepdims=True)
            p0 = jnp.exp(s0 - m0)
            p1 = jnp.exp(s1 - m1)
            l0 = jnp.sum(p0, axis=-1, keepdims=True)
            l1 = jnp.sum(p1, axis=-1, keepdims=True)
            o = _dot(jnp.concatenate([p0, p1], axis=1).astype(BF16), vcat)
            o_blk = o * jnp.where(first, 1.0 / l0, 1.0 / l1)
            l_blk = jnp.where(first, m0 + jnp.log(l0), m1 + jnp.log(l1))
            if dil == 1:
                dst = pl.ds(qs, qb)
            else:
                dst = pl.ds(within * dil + seg, qb, stride=dil)
            ob_ref[bi, dst, :] = o_blk
            lb_ref[bi, dst, :] = l_blk

    branch(0, DILATIONS[0], q1, k1, v1)
    branch(1, DILATIONS[1], q4, k4, v4)
    branch(2, DILATIONS[2], q16, k16, v16)

    rows = 256
    for i in range(s // rows):
        sl = slice(i * rows, (i + 1) * rows)
        l0, l1, l2 = lb_ref[0, sl, :], lb_ref[1, sl, :], lb_ref[2, sl, :]
        m = jnp.maximum(jnp.maximum(l0, l1), l2)
        w0, w1, w2 = jnp.exp(l0 - m), jnp.exp(l1 - m), jnp.exp(l2 - m)
        num = w0 * ob_ref[0, sl, :] + w1 * ob_ref[1, sl, :] + w2 * ob_ref[2, sl, :]
        o_ref[0, sl, :] = (num / (w0 + w1 + w2)).astype(BF16)


def _dilated_attention(qkv1, qkv4, qkv16, slopes):
    _, bsz, s, _ = qkv1.shape
    npairs = ATT_WIDTH // ATT_PAIR
    spec = lambda c: pl.BlockSpec((1, 1, s, ATT_PAIR), lambda bi, hp, c=c: (c, bi, 0, hp))
    in_specs = [spec(0), spec(1), spec(2)] * 3 + [pl.BlockSpec((1, 1, ATT_PAIR), lambda bi, hp: (hp, 0, 0))]
    return pl.pallas_call(
        _att_kernel,
        grid=(bsz, npairs),
        in_specs=in_specs,
        out_specs=pl.BlockSpec((1, s, ATT_PAIR), lambda bi, hp: (bi, 0, hp)),
        out_shape=jax.ShapeDtypeStruct((bsz, s, ATT_WIDTH), BF16),
        scratch_shapes=[pltpu.VMEM((2, s, ATT_PAIR), BF16), pltpu.VMEM((2, s, ATT_PAIR), BF16),
                        pltpu.VMEM((3, ATT_QBLOCK, 2 * ATT_WINDOW), F32),
                        pltpu.VMEM((3, s, ATT_PAIR), F32), pltpu.VMEM((3, s, ATT_PAIR), F32)],
        compiler_params=_cparams("parallel", "parallel"),
        name="dilated_attention",
    )(qkv1, qkv1, qkv1, qkv4, qkv4, qkv4, qkv16, qkv16, qkv16, slopes)


def _outproj_kernel(h_ref, yssd_ref, z_ref, yatt_ref, nw_ref, wo_ref, g_ref, b_ref, o_ref):
    y = yssd_ref[...].astype(F32) * _silu(z_ref[...].astype(F32))
    yn = y * lax.rsqrt(jnp.mean(y * y, -1, keepdims=True) + RMS_EPS) * nw_ref[...]
    mix = _dot(yn.astype(BF16), wo_ref[0:SSD_WIDTH, :]) + _dot(yatt_ref[...], wo_ref[SSD_WIDTH:, :])
    o_ref[...] = _ln_rows(DN_ALPHA * h_ref[...] + mix, g_ref[...], b_ref[...])


def _outproj(h, yssd, z, yatt, norm_w, w_out, g, b, tm=512):
    t, d = h.shape
    const = lambda *_: (0, 0)
    wspec = lambda w: pl.BlockSpec(w.shape, const)
    row = lambda n: pl.BlockSpec((tm, n), lambda i: (i, 0))
    return pl.pallas_call(
        _outproj_kernel,
        grid=(t // tm,),
        in_specs=[row(d), row(SSD_WIDTH), row(SSD_WIDTH), row(ATT_WIDTH),
                  wspec(norm_w), wspec(w_out), wspec(g), wspec(b)],
        out_specs=row(d),
        out_shape=jax.ShapeDtypeStruct((t, d), F32),
        compiler_params=_cparams("parallel"),
        name="outproj_ln",
    )(h, yssd, z, yatt, norm_w, w_out, g, b)


def _alibi_slopes():
    sl = np.array([2.0 ** (-8.0 * (h + 1) / ATT_HEADS) for h in range(ATT_HEADS)], dtype=np.float32)
    return jnp.asarray(np.repeat(sl, ATT_HEAD_DIM).reshape(ATT_WIDTH // ATT_PAIR, 1, ATT_PAIR))


def _mixer_params(w_in, conv_w, conv_b, dt_bias_f, dt_bias_b, a_log_f, a_log_b, d_skip):
    r = SSD_HEADS_PER_GROUP
    wz = w_in[:, 0:XBC_OFF].astype(BF16)
    wxbc = w_in[:, XBC_OFF:DT_OFF].astype(BF16)
    wqkv = w_in[:, Q_OFF:IN_COLS].astype(BF16)
    wdt_f = w_in[:, DT_OFF:DT_OFF + SSD_HEADS].reshape(D_MODEL, SSD_GROUPS, r)
    wdt_b = w_in[:, DT_OFF + SSD_HEADS:Q_OFF].reshape(D_MODEL, SSD_GROUPS, r)
    lane_pad = ((0, 0), (0, 0), (0, LANES - 2 * r))
    wdt = jnp.pad(jnp.concatenate([wdt_f, wdt_b], -1), lane_pad).reshape(D_MODEL, SSD_GROUPS * LANES).astype(BF16)

    def per_group_lanes(f, b):
        v = jnp.concatenate([f.reshape(SSD_GROUPS, 1, r), b.reshape(SSD_GROUPS, 1, r)], -1)
        return jnp.pad(v.astype(F32), lane_pad)

    dtb = per_group_lanes(dt_bias_f, dt_bias_b)
    a_f = -jnp.exp(a_log_f.astype(F32))
    a_b = -jnp.exp(a_log_b.astype(F32))

    def per_head_lanes(n):
        v = jnp.stack([a_f.reshape(SSD_GROUPS, r), a_b.reshape(SSD_GROUPS, r)], axis=1)
        return jnp.repeat(v, n, axis=-1).reshape(SSD_GROUPS, 2, 1, r * n)

    a = (per_group_lanes(a_f, a_b), per_head_lanes(SSD_HEAD_DIM), per_head_lanes(LANES))

    def group_cols(w):
        xs = w[..., 0:SSD_WIDTH].reshape(w.shape[:-1] + (SSD_GROUPS, GROUP_X))
        bs = w[..., SSD_WIDTH:SSD_WIDTH + GN].reshape(w.shape[:-1] + (SSD_GROUPS, SSD_STATE))
        cs = w[..., SSD_WIDTH + GN:].reshape(w.shape[:-1] + (SSD_GROUPS, SSD_STATE))
        return jnp.moveaxis(jnp.concatenate([xs, bs, cs], -1), -2, 0)

    cw = group_cols(conv_w.astype(F32))
    cb = group_cols(conv_b.astype(F32)[None])
    dsk = jnp.repeat(d_skip.astype(F32), SSD_HEAD_DIM).reshape(SSD_GROUPS, 1, GROUP_X)
    return wz, wxbc, wdt, wqkv, cw, cb, dtb, a, dsk


def _mixer_layer(x, ln_in_g, ln_in_b, w_in, conv_w, conv_b, dt_bias_f, dt_bias_b, a_log_f, a_log_b,
                 d_skip, ssd_norm_w, w_out_mix, ln1_g, ln1_b):
    bsz, s, d = x.shape
    vec = lambda v: v.astype(F32).reshape(1, -1)
    wz, wxbc, wdt, wqkv, cw, cb, dtb, a, dsk = _mixer_params(
        w_in, conv_w, conv_b, dt_bias_f, dt_bias_b, a_log_f, a_log_b, d_skip)
    h, z, xbc, dt, qkv1, qkv4, qkv16 = _inproj(x, vec(ln_in_g), vec(ln_in_b), wz, wxbc, wdt, wqkv)
    yssd = _ssd(xbc, dt, cw, cb, dtb, a, dsk)
    yatt = _dilated_attention(qkv1, qkv4.reshape(qkv1.shape), qkv16.reshape(qkv1.shape), _alibi_slopes())
    t = bsz * s
    return _outproj(h.reshape(t, d), yssd.reshape(t, SSD_WIDTH), z.reshape(t, SSD_WIDTH),
                    yatt.reshape(t, ATT_WIDTH), vec(ssd_norm_w), w_out_mix.astype(BF16), vec(ln1_g), vec(ln1_b))


def _memkv_kernel(mem_ref, wk_ref, wv_ref, k_ref, v_ref):
    m = mem_ref[0].astype(BF16)
    k_ref[0] = _dot(m, wk_ref[...]).astype(BF16)
    v_ref[0] = _dot(m, wv_ref[...]).astype(BF16)


def _memkv(mem, wk, wv):
    bsz, m, d = mem.shape
    const = lambda *_: (0, 0)
    blk = pl.BlockSpec((1, m, d), lambda bi: (bi, 0, 0))
    return pl.pallas_call(
        _memkv_kernel,
        grid=(bsz,),
        in_specs=[blk, pl.BlockSpec(wk.shape, const), pl.BlockSpec(wv.shape, const)],
        out_specs=(blk, blk),
        out_shape=(jax.ShapeDtypeStruct((bsz, m, d), BF16),) * 2,
        compiler_params=_cparams("parallel"),
        name="mem_kv",
    )(mem, wk, wv)


def _memattn_kernel(h_ref, k_ref, v_ref, wq_ref, wo_ref, g_ref, b_ref, o_ref, ow_ref):
    h = h_ref[0]
    q = (_dot(h.astype(BF16), wq_ref[...]) * (MEM_HEAD_DIM ** -0.5)).astype(BF16)
    xa = None
    for hd in range(MEM_HEADS):
        cols = slice(hd * MEM_HEAD_DIM, (hd + 1) * MEM_HEAD_DIM)
        sc = _dot_nt(q[:, cols], k_ref[0, :, cols])
        p = jnp.exp(sc - jnp.max(sc, axis=-1, keepdims=True))
        o = _dot(p.astype(BF16), v_ref[0, :, cols]) / jnp.sum(p, axis=-1, keepdims=True)
        part = _dot(o.astype(BF16), wo_ref[cols, :])
        xa = part if xa is None else xa + part
    out = _ln_rows(DN_ALPHA * h + xa, g_ref[...], b_ref[...])
    o_ref[0] = out
    ow_ref[0] = _pack_bf16_pairs(out)


def _memattn(h, k, v, wq, wo, g, b, tm=512):
    bsz, s, d = h.shape
    m = k.shape[1]
    const = lambda *_: (0, 0)
    wspec = lambda w: pl.BlockSpec(w.shape, const)
    row = pl.BlockSpec((1, tm, d), lambda bi, i: (bi, i, 0))
    kv = pl.BlockSpec((1, m, d), lambda bi, i: (bi, 0, 0))
    return pl.pallas_call(
        _memattn_kernel,
        grid=(bsz, s // tm),
        in_specs=[row, kv, kv, wspec(wq), wspec(wo), wspec(g), wspec(b)],
        out_specs=(row, pl.BlockSpec((1, tm, d // 2), lambda bi, i: (bi, i, 0))),
        out_shape=(jax.ShapeDtypeStruct((bsz, s, d), F32), jax.ShapeDtypeStruct((bsz, s, d // 2), I32)),
        compiler_params=_cparams("parallel", "parallel"),
        name="mem_attn_ln",
    )(h, k, v, wq, wo, g, b)


def _first_max(work, idx_iota, sentinel):
    m = jnp.max(work, axis=0, keepdims=True)
    idx = jnp.min(jnp.where(work == m, idx_iota, sentinel), axis=0, keepdims=True)
    return m, idx


def _router_kernel(h_ref, wrt_ref, bias_ref, eidx_ref, gate_ref, rank_ref, cnt_ref, run_ref):
    tm = h_ref.shape[0]
    e = N_EXPERTS

    @pl.when(pl.program_id(0) == 0)
    def _():
        run_ref[...] = jnp.zeros_like(run_ref)

    logits = lax.dot_general(wrt_ref[...], h_ref[...], (((1,), (1,)), ((), ())),
                             precision=lax.Precision.HIGHEST, preferred_element_type=F32)
    scores = jax.nn.sigmoid(logits)
    choice = scores + bias_ref[...]
    row = lax.broadcasted_iota(I32, (e, tm), 0)
    grow = lax.broadcasted_iota(I32, (GROUP_SIZE, tm), 0)
    gs = []
    for g in range(ROUTE_GROUPS):
        blk = choice[g * GROUP_SIZE:(g + 1) * GROUP_SIZE, :]
        m1, i1 = _first_max(blk, grow, GROUP_SIZE)
        m2 = jnp.max(jnp.where(grow == i1, -jnp.inf, blk), axis=0, keepdims=True)
        gs.append(m1 + m2)
    work = jnp.concatenate(gs, axis=0)
    giota = lax.broadcasted_iota(I32, (ROUTE_GROUPS, tm), 0)
    gmask = jnp.zeros((ROUTE_GROUPS, tm), jnp.bool_)
    for _ in range(TOPK_GROUPS):
        _, gi = _first_max(work, giota, ROUTE_GROUPS)
        hit = giota == gi
        gmask = jnp.logical_or(gmask, hit)
        work = jnp.where(hit, -jnp.inf, work)
    work = jnp.concatenate(
        [jnp.where(gmask[g:g + 1, :], choice[g * GROUP_SIZE:(g + 1) * GROUP_SIZE, :], NEG_INF)
         for g in range(ROUTE_GROUPS)], axis=0)
    chosen = jnp.zeros((e, tm), jnp.bool_)
    idxs, svals = [], []
    for _ in range(TOP_K):
        _, ei = _first_max(work, row, e)
        hit = row == ei
        svals.append(jnp.sum(jnp.where(hit, scores, 0.0), axis=0, keepdims=True))
        idxs.append(ei)
        chosen = jnp.logical_or(chosen, hit)
        work = jnp.where(hit, -jnp.inf, work)
    eidx = jnp.concatenate(idxs, axis=0)
    sv = jnp.concatenate(svals, axis=0)
    eidx_ref[...] = eidx
    gate_ref[...] = sv / jnp.sum(sv, axis=0, keepdims=True) * ROUTED_SCALE
    upper = (lax.broadcasted_iota(I32, (tm, tm), 0) < lax.broadcasted_iota(I32, (tm, tm), 1))
    mt = chosen.astype(F32)
    cum = _dot(mt.astype(BF16), upper.astype(F32).astype(BF16)) + run_ref[...]
    rank_ref[...] = jnp.concatenate(
        [jnp.sum(jnp.where(row == idxs[k], cum, 0.0), axis=0, keepdims=True) for k in range(TOP_K)],
        axis=0).astype(I32)
    run_ref[...] += jnp.sum(mt, axis=1, keepdims=True)
    cnt_ref[...] = run_ref[...]


def _router(h, w_router_t, bias_col, tm=512):
    t, d = h.shape
    const = lambda *_: (0, 0)
    col = lambda dt: pl.BlockSpec((TOP_K, tm), lambda i: (0, i))
    return pl.pallas_call(
        _router_kernel,
        grid=(t // tm,),
        in_specs=[pl.BlockSpec((tm, d), lambda i: (i, 0)), pl.BlockSpec(w_router_t.shape, const),
                  pl.BlockSpec(bias_col.shape, const)],
        out_specs=(col(I32), col(F32), col(I32), pl.BlockSpec((N_EXPERTS, 1), const)),
        out_shape=(jax.ShapeDtypeStruct((TOP_K, t), I32), jax.ShapeDtypeStruct((TOP_K, t), F32),
                   jax.ShapeDtypeStruct((TOP_K, t), I32), jax.ShapeDtypeStruct((N_EXPERTS, 1), F32)),
        scratch_shapes=[pltpu.VMEM((N_EXPERTS, 1), F32)],
        compiler_params=_cparams("arbitrary"),
        name="router_topk",
    )(h, w_router_t, bias_col)


def _dispatch_kernel(cnt_ref, eidx_ref, rank_ref, dest_ref, bstart_ref, nb_ref):
    tm = eidx_ref.shape[1]
    e = N_EXPERTS
    nb = jnp.floor((cnt_ref[...] + (MOE_BLOCK - 1)) * (1.0 / MOE_BLOCK))
    strict = (lax.broadcasted_iota(I32, (e, e), 1) < lax.broadcasted_iota(I32, (e, e), 0))
    bstart = _dot(strict.astype(F32).astype(BF16), jnp.broadcast_to(nb, (e, LANES)).astype(BF16))[:, 0:1]
    pstart = bstart * MOE_BLOCK
    row = lax.broadcasted_iota(I32, (e, tm), 0)
    eidx = eidx_ref[...]
    rank = rank_ref[...]
    dest_ref[...] = jnp.concatenate(
        [jnp.sum(jnp.where(row == eidx[k:k + 1, :], pstart, 0.0), axis=0, keepdims=True).astype(I32)
         + rank[k:k + 1, :] for k in range(TOP_K)], axis=0)
    bstart_ref[...] = bstart.astype(I32)
    nb_ref[...] = nb.astype(I32)


def _dispatch(cnt, eidx, rank, tm=2048):
    t = eidx.shape[1]
    const = lambda *_: (0, 0)
    col = pl.BlockSpec((TOP_K, tm), lambda i: (0, i))
    per_expert = pl.BlockSpec((N_EXPERTS, 1), const)
    return pl.pallas_call(
        _dispatch_kernel,
        grid=(t // tm,),
        in_specs=[per_expert, col, col],
        out_specs=(col, per_expert, per_expert),
        out_shape=(jax.ShapeDtypeStruct((TOP_K, t), I32), jax.ShapeDtypeStruct((N_EXPERTS, 1), I32),
                   jax.ShapeDtypeStruct((N_EXPERTS, 1), I32)),
        compiler_params=_cparams("arbitrary"),
        name="dispatch_slots",
    )(cnt, eidx, rank)


def _experts_kernel(bstart_ref, nb_ref, xs_hbm, wg_ref, wu_ref, wd_ref, ys_hbm,
                    xbuf, ybuf, wg_b, wu_b, wd_b, in_sem, out_sem):
    e = pl.program_id(0)
    last_e = pl.num_programs(0) - 1
    first_block = bstart_ref[e]
    nused = bstart_ref[last_e] + nb_ref[last_e]

    def block_rows(blk):
        return pl.ds(pl.multiple_of(blk * MOE_BLOCK, MOE_BLOCK), MOE_BLOCK)

    def in_copy(blk, slot):
        return pltpu.make_async_copy(xs_hbm.at[block_rows(blk)], xbuf.at[slot], in_sem.at[slot])

    def out_copy(blk, slot):
        return pltpu.make_async_copy(ybuf.at[slot], ys_hbm.at[block_rows(blk)], out_sem.at[slot])

    @pl.when(jnp.logical_and(e == 0, nused > 0))
    def _():
        in_copy(0, 0).start()

    wg_b[...] = wg_ref[0].astype(BF16)
    wu_b[...] = wu_ref[0].astype(BF16)
    wd_b[...] = wd_ref[0].astype(BF16)

    def body(j, carry):
        blk = first_block + j
        slot = blk % 2
        in_copy(blk, slot).wait()

        @pl.when(blk + 1 < nused)
        def _():
            in_copy(blk + 1, 1 - slot).start()

        @pl.when(blk >= 2)
        def _():
            out_copy(blk - 2, slot).wait()

        x = _unpack_bf16_pairs(xbuf[slot]).astype(BF16)
        g = _dot(x, wg_b[...])
        u = _dot(x, wu_b[...])
        a = (_silu(g) * u).astype(BF16)
        ybuf[slot] = _pack_bf16_pairs(_dot(a, wd_b[...]))
        out_copy(blk, slot).start()
        return carry

    lax.fori_loop(0, nb_ref[e], body, 0)

    @pl.when(e == last_e)
    def _():
        for back in (1, 2):
            @pl.when(nused >= back)
            def _():
                out_copy(nused - back, (nused - back) % 2).wait()


def _experts(bstart, nb, xs, w_gate, w_up, w_down):
    n_pad, words = xs.shape
    n_exp, d, f = w_gate.shape
    wspec = lambda w: pl.BlockSpec((1,) + w.shape[1:], lambda e, bs, nb: (e, 0, 0))
    hbm = pl.BlockSpec(memory_space=pl.ANY)
    return pl.pallas_call(
        _experts_kernel,
        grid_spec=pltpu.PrefetchScalarGridSpec(
            num_scalar_prefetch=2,
            grid=(n_exp,),
            in_specs=[hbm, wspec(w_gate), wspec(w_up), wspec(w_down)],
            out_specs=hbm,
            scratch_shapes=[
                pltpu.VMEM((2, MOE_BLOCK, words), I32), pltpu.VMEM((2, MOE_BLOCK, words), I32),
                pltpu.VMEM((d, f), BF16), pltpu.VMEM((d, f), BF16), pltpu.VMEM((f, d), BF16),
                pltpu.SemaphoreType.DMA((2,)), pltpu.SemaphoreType.DMA((2,)),
            ],
        ),
        out_shape=jax.ShapeDtypeStruct((n_pad, words), I32),
        compiler_params=_cparams("arbitrary"),
        name="routed_experts",
    )(bstart, nb, xs, w_gate, w_up, w_down)


def _final_kernel(h_ref, yg_ref, gate_ref, wsg_ref, wsu_ref, wsd_ref, g_ref, b_ref, o_ref):
    h = h_ref[...]
    d = h.shape[1]
    hb = h.astype(BF16)
    ff = _dot((_silu(_dot(hb, wsg_ref[...])) * _dot(hb, wsu_ref[...])).astype(BF16), wsd_ref[...])
    gate = gate_ref[...]
    for k in range(TOP_K):
        ff = ff + gate[:, k:k + 1] * _unpack_bf16_pairs(yg_ref[k])
    o_ref[...] = _ln_rows(DN_ALPHA * h + ff, g_ref[...], b_ref[...])


def _final(h, yg, gate, wsg, wsu, wsd, g, b, tm=256):
    t, d = h.shape
    const = lambda *_: (0, 0)
    wspec = lambda w: pl.BlockSpec(w.shape, const)
    row = lambda n: pl.BlockSpec((tm, n), lambda i: (i, 0))
    return pl.pallas_call(
        _final_kernel,
        grid=(t // tm,),
        in_specs=[row(d), pl.BlockSpec((TOP_K, tm, d // 2), lambda i: (0, i, 0)), row(TOP_K), wspec(wsg), wspec(wsu), wspec(wsd), wspec(g), wspec(b)],
        out_specs=row(d),
        out_shape=jax.ShapeDtypeStruct((t, d), F32),
        compiler_params=_cparams("parallel"),
        name="shared_combine_ln",
    )(h, yg, gate, wsg, wsu, wsd, g, b)


SC_CORES = 2
SC_SUBCORES = 16
SC_WORKERS = SC_CORES * SC_SUBCORES
SC_WINDOW = 64


def _sc_scatter_rows(x_words, dest_kmajor, n_rows):
    t, w = x_words.shape
    per_worker = t // SC_WORKERS
    nchunk = per_worker // SC_WINDOW
    mesh = plsc.VectorSubcoreMesh(core_axis_name="c", subcore_axis_name="s")

    @functools.partial(
        pl.kernel, mesh=mesh,
        out_type=jax.ShapeDtypeStruct((n_rows, w), I32),
        scratch_types=[pltpu.VMEM((SC_WINDOW,), I32), pltpu.VMEM((SC_WINDOW, w), I32), pltpu.SemaphoreType.DMA],
    )
    def scatter(x_hbm, dest_hbm, out_hbm, idx_v, rows_v, sem):
        wid = lax.axis_index("s") * SC_CORES + lax.axis_index("c")
        base = wid * per_worker

        @pl.loop(0, nchunk)
        def _(i):
            t0 = base + i * SC_WINDOW
            pltpu.sync_copy(x_hbm.at[pl.ds(t0, SC_WINDOW)], rows_v)
            for k in range(TOP_K):
                pltpu.sync_copy(dest_hbm.at[pl.ds(k * t + t0, SC_WINDOW)], idx_v)
                pltpu.async_copy(rows_v, out_hbm.at[idx_v], sem).wait()

    return scatter(x_words, dest_kmajor)


def _sc_gather_rows(table, idx):
    a = idx.shape[0]
    w = table.shape[1]
    per_worker = a // SC_WORKERS
    nchunk = per_worker // SC_WINDOW
    mesh = plsc.VectorSubcoreMesh(core_axis_name="c", subcore_axis_name="s")

    @functools.partial(
        pl.kernel, mesh=mesh,
        out_type=jax.ShapeDtypeStruct((a, w), I32),
        scratch_types=[pltpu.VMEM((SC_WINDOW,), I32), pltpu.VMEM((SC_WINDOW, w), I32), pltpu.SemaphoreType.DMA],
    )
    def gather(table_hbm, idx_hbm, out_hbm, idx_v, rows_v, sem):
        wid = lax.axis_index("s") * SC_CORES + lax.axis_index("c")
        base = wid * per_worker

        @pl.loop(0, nchunk)
        def _(i):
            off = base + i * SC_WINDOW
            pltpu.sync_copy(idx_hbm.at[pl.ds(off, SC_WINDOW)], idx_v)
            pltpu.async_copy(table_hbm.at[idx_v], rows_v, sem).wait()
            pltpu.sync_copy(rows_v, out_hbm.at[pl.ds(off, SC_WINDOW)])

    return gather(table, idx)


def _moe_layer(h2, h2_words, w_router, router_bias, w_gate, w_up, w_down, ws_gate, ws_up, ws_down, ln3_g, ln3_b):
    t, d = h2.shape
    vec = lambda v: v.astype(F32).reshape(1, -1)
    eidx, gate, rank, cnt = _router(h2, w_router.astype(F32).T, router_bias.astype(F32).reshape(-1, 1))
    n_blocks = t * TOP_K // MOE_BLOCK + N_EXPERTS
    dest, bstart, nb = _dispatch(cnt, eidx, rank)
    dest_flat = dest.reshape(TOP_K * t)
    xs = _sc_scatter_rows(h2_words, dest_flat, n_blocks * MOE_BLOCK)
    ys = _experts(bstart.reshape(N_EXPERTS), nb.reshape(N_EXPERTS), xs, w_gate, w_up, w_down)
    yg = _sc_gather_rows(ys, dest_flat).reshape(TOP_K, t, d // 2)
    return _final(h2, yg, gate.T, ws_gate.astype(BF16), ws_up.astype(BF16), ws_down.astype(BF16),
                  vec(ln3_g), vec(ln3_b))


def kernel(x, mem, ln_in_g, ln_in_b, w_in, conv_w, conv_b, dt_bias_f, dt_bias_b, a_log_f, a_log_b, d_skip, ssd_norm_w, w_out_mix, ln1_g, ln1_b, wq_mem, wk_mem, wv_mem, wo_mem, ln2_g, ln2_b, w_router, router_bias, w_gate, w_up, w_down, ws_gate, ws_up, ws_down, ln3_g, ln3_b):
    l = 0
    h1 = _mixer_layer(x, ln_in_g, ln_in_b, w_in[l], conv_w[l], conv_b[l], dt_bias_f[l], dt_bias_b[l],
                      a_log_f[l], a_log_b[l], d_skip[l], ssd_norm_w[l], w_out_mix[l], ln1_g[l], ln1_b[l])
    bsz, s, d = x.shape
    vec = lambda v: v.astype(F32).reshape(1, -1)
    km, vm = _memkv(mem, wk_mem[l].astype(BF16), wv_mem[l].astype(BF16))
    h2, h2_words = _memattn(h1.reshape(bsz, s, d), km, vm, wq_mem[l].astype(BF16), wo_mem[l].astype(BF16),
                            vec(ln2_g[l]), vec(ln2_b[l]))
    out = _moe_layer(h2.reshape(bsz * s, d), h2_words.reshape(bsz * s, d // 2), w_router[l], router_bias[l], w_gate[l], w_up[l], w_down[l],
                     ws_gate[l], ws_up[l], ws_down[l], ln3_g[l], ln3_b[l])
    return out.reshape(x.shape)
```

```python
import functools
import math

import numpy as np
import jax
import jax.numpy as jnp
from jax import lax
from jax.experimental import pallas as pl
from jax.experimental.pallas import tpu as pltpu
from jax.experimental.pallas import tpu_sc as plsc

F32 = jnp.float32
BF16 = jnp.bfloat16
I32 = jnp.int32

D_MODEL = 1024
SSD_WIDTH = 1024
SSD_HEAD_DIM = 64
SSD_HEADS = 16
SSD_STATE = 128
SSD_GROUPS = 4
SSD_HEADS_PER_GROUP = SSD_HEADS // SSD_GROUPS
SSD_CONV = 5
SSD_CHUNK = 128
ATT_WIDTH = 1024
ATT_HEAD_DIM = 64
ATT_HEADS = 16
ATT_RADIUS = 64
DILATIONS = (1, 4, 16)
MEM_HEADS = 4
MEM_HEAD_DIM = 256
N_EXPERTS = 256
ROUTE_GROUPS = 8
GROUP_SIZE = N_EXPERTS // ROUTE_GROUPS
TOPK_GROUPS = 4
TOP_K = 8
EXPERT_DIM = 256
ROUTED_SCALE = 2.5
MOE_BLOCK = 256
DN_ALPHA = 2.0 ** 0.25
LN_EPS = 1e-5
RMS_EPS = 1e-5
NEG_INF = -1e30
GN = SSD_GROUPS * SSD_STATE
CONV_CH = SSD_WIDTH + 2 * GN
XBC_OFF = SSD_WIDTH
DT_OFF = XBC_OFF + CONV_CH
Q_OFF = DT_OFF + 2 * SSD_HEADS
IN_COLS = Q_OFF + 3 * ATT_WIDTH

LANES = 128
VMEM_LIMIT_BYTES = 56 * 1024 * 1024


def _cparams(*sem):
    return pltpu.CompilerParams(dimension_semantics=sem, vmem_limit_bytes=VMEM_LIMIT_BYTES)


def _ln_rows(x, g, b):
    mu = jnp.mean(x, -1, keepdims=True)
    xc = x - mu
    var = jnp.mean(xc * xc, -1, keepdims=True)
    return xc * lax.rsqrt(var + LN_EPS) * g + b


def _dot(a, b):
    return jnp.dot(a, b, preferred_element_type=F32)


def _dot_nt(a, b):
    return lax.dot_general(a, b, (((1,), (1,)), ((), ())), preferred_element_type=F32)


def _silu(x):
    return x * jax.nn.sigmoid(x)


def _pack_bf16_pairs(x):
    n = x.shape[1] // 2
    u = lax.bitcast_convert_type(x.astype(BF16).astype(F32), jnp.uint32)
    word = (u[:, :n] >> 16) | (u[:, n:] & jnp.uint32(0xFFFF0000))
    return lax.bitcast_convert_type(word, I32)


def _unpack_bf16_pairs(w):
    u = lax.bitcast_convert_type(w, jnp.uint32)
    lo = lax.bitcast_convert_type(u << 16, F32)
    hi = lax.bitcast_convert_type(u & jnp.uint32(0xFFFF0000), F32)
    return jnp.concatenate([lo, hi], axis=1)


def _inproj_kernel(x_ref, g_ref, b_ref, wz_ref, wxbc_ref, wdt_ref, wqkv_ref,
                   h_ref, z_ref, xbc_ref, dt_ref, qkv1_ref, qkv4_ref, qkv16_ref, stage_ref):
    tm = x_ref.shape[1]
    h = _ln_rows(x_ref[0], g_ref[...], b_ref[...])
    h_ref[0] = h
    hb = h.astype(BF16)
    z_ref[0] = _dot(hb, wz_ref[...]).astype(BF16)
    for c in range(CONV_CH // D_MODEL):
        cols = slice(c * D_MODEL, (c + 1) * D_MODEL)
        xbc_ref[0, :, cols] = _dot(hb, wxbc_ref[:, cols]).astype(BF16)
    dt_ref[0] = _dot(hb, wdt_ref[...])
    for c in range(3):
        cols = slice(c * ATT_WIDTH, (c + 1) * ATT_WIDTH)
        res = _dot(hb, wqkv_ref[:, cols])
        if c == 0:
            res = res * (ATT_HEAD_DIM ** -0.5)
        qkv1_ref[c, 0] = res.astype(BF16)
        for j in range(ATT_WIDTH // LANES):
            lanes = slice(j * LANES, (j + 1) * LANES)
            stage_ref[j] = res[:, lanes]
            for r in range(4):
                qkv4_ref[c, 0, r, :, lanes] = stage_ref[j, pl.ds(r, tm // 4, stride=4), :].astype(BF16)
            for r in range(16):
                qkv16_ref[c, 0, r, :, lanes] = stage_ref[j, pl.ds(r, tm // 16, stride=16), :].astype(BF16)


def _inproj(x, g, b, wz, wxbc, wdt, wqkv, tm=256):
    bsz, s, d = x.shape
    nt = s // tm
    const = lambda *_: (0, 0)
    wspec = lambda w: pl.BlockSpec(w.shape, const)
    row = lambda n: pl.BlockSpec((1, tm, n), lambda bi, i: (bi, i, 0))
    out_shape = (
        jax.ShapeDtypeStruct((bsz, s, d), F32),
        jax.ShapeDtypeStruct((bsz, s, SSD_WIDTH), BF16),
        jax.ShapeDtypeStruct((bsz, s, CONV_CH), BF16),
        jax.ShapeDtypeStruct((bsz, s, SSD_GROUPS * LANES), F32),
        jax.ShapeDtypeStruct((3, bsz, s, ATT_WIDTH), BF16),
        jax.ShapeDtypeStruct((3, bsz, 4, s // 4, ATT_WIDTH), BF16),
        jax.ShapeDtypeStruct((3, bsz, 16, s // 16, ATT_WIDTH), BF16),
    )
    out_specs = (
        row(d), row(SSD_WIDTH), row(CONV_CH), row(SSD_GROUPS * LANES),
        pl.BlockSpec((3, 1, tm, ATT_WIDTH), lambda bi, i: (0, bi, i, 0)),
        pl.BlockSpec((3, 1, 4, tm // 4, ATT_WIDTH), lambda bi, i: (0, bi, 0, i, 0)),
        pl.BlockSpec((3, 1, 16, tm // 16, ATT_WIDTH), lambda bi, i: (0, bi, 0, i, 0)),
    )
    return pl.pallas_call(
        _inproj_kernel,
        grid=(bsz, nt),
        in_specs=[row(d), wspec(g), wspec(b), wspec(wz), wspec(wxbc), wspec(wdt), wspec(wqkv)],
        out_specs=out_specs,
        out_shape=out_shape,
        scratch_shapes=[pltpu.VMEM((ATT_WIDTH // LANES, tm, LANES), F32)],
        compiler_params=_cparams("parallel", "parallel"),
        name="ln_inproj",
    )(x, g, b, wz, wxbc, wdt, wqkv)


GROUP_X = SSD_HEADS_PER_GROUP * SSD_HEAD_DIM
GROUP_CH = GROUP_X + 2 * SSD_STATE
CONV_ROWS = 256
PAD_ROWS = 8
SSD_UNROLL = 2


def _softplus(x):
    return jnp.maximum(x, 0.0) + jnp.log(1.0 + jnp.exp(-jnp.abs(x)))


def _ssd_constants():
    q = SSD_CHUNK
    r = SSD_HEADS_PER_GROUP
    i = np.arange(q)
    tris = [(i[:, None] >= i[None, :]), (i[:, None] <= i[None, :])]
    tri_cat = np.stack([np.concatenate([t, t], axis=1) for t in tris])
    trit_cat = np.stack([np.concatenate([t.T, t.T], axis=0) for t in tris])

    src = np.arange(2 * LANES)[:, None] % LANES
    head = np.arange(GROUP_X)[None, :] // SSD_HEAD_DIM
    sel = np.stack([src == head + off for off in (0, r)])
    as_bf16 = lambda m: jnp.asarray(m.astype(np.float32), dtype=BF16)
    return as_bf16(tri_cat), as_bf16(trit_cat), as_bf16(sel)


def _ssd_kernel(x_ref, b_ref, c_ref, dt_ref, cw_ref, cb_ref, dtb_ref, a_ref, ax_ref, dsk_ref,
                tric_ref, tritc_ref, selx_ref, y_ref, pad_ref, act_ref, yacc_ref, st_ref):
    s = x_ref.shape[1]
    q = SSD_CHUNK
    nchunks = s // q

    zeros = jnp.zeros((PAD_ROWS, GROUP_CH), F32)
    pad_ref[0:PAD_ROWS, :] = zeros
    pad_ref[s + PAD_ROWS:s + 2 * PAD_ROWS, :] = zeros
    pad_ref[PAD_ROWS:s + PAD_ROWS, 0:GROUP_X] = x_ref[0].astype(F32)
    pad_ref[PAD_ROWS:s + PAD_ROWS, GROUP_X:GROUP_X + SSD_STATE] = b_ref[0].astype(F32)
    pad_ref[PAD_ROWS:s + PAD_ROWS, GROUP_X + SSD_STATE:GROUP_CH] = c_ref[0].astype(F32)
    for i in range(s // CONV_ROWS):
        acc = jnp.broadcast_to(cb_ref[0], (CONV_ROWS, GROUP_CH))
        for k in range(SSD_CONV):
            r0 = PAD_ROWS + i * CONV_ROWS + k - SSD_CONV // 2
            acc = acc + cw_ref[0, k:k + 1, :] * pad_ref[r0:r0 + CONV_ROWS, :]
        act_ref[i * CONV_ROWS:(i + 1) * CONV_ROWS, :] = _silu(acc)

    row_i = lax.broadcasted_iota(I32, (q, q), 0)
    col_i = lax.broadcasted_iota(I32, (q, q), 1)
    lane = lax.broadcasted_iota(I32, (q, GROUP_X), 1)
    low_half = lax.broadcasted_iota(I32, (q, LANES), 1) < SSD_HEAD_DIM

    def split(v):
        hi = v.astype(BF16)
        return hi, (v - hi.astype(F32)).astype(BF16)

    st_ref[...] = jnp.zeros_like(st_ref)

    def chunk(ci, reverse):
        di = 1 if reverse else 0
        off = SSD_HEADS_PER_GROUP if reverse else 0
        tri = (row_i <= col_i) if reverse else (row_i >= col_i)
        k = dict(tri_cat=tric_ref[di], trit_cat=tritc_ref[di], sel_x=selx_ref[di])
        c = (nchunks - 1 - ci) if reverse else ci
        rows = pl.ds(pl.multiple_of(c * q, q), q)
        xa = act_ref[rows, 0:GROUP_X]
        bm = act_ref[rows, GROUP_X:GROUP_X + SSD_STATE]
        cm = act_ref[rows, GROUP_X + SSD_STATE:GROUP_CH]
        dt = _softplus(dt_ref[0, rows, :] + dtb_ref[0])
        dcat = jnp.concatenate(split(dt), axis=1)
        dt_x = _dot(dcat, k["sel_x"])
        cs_x = _dot(k["tri_cat"], jnp.concatenate(split(dt_x * ax_ref[0, di]), axis=0))
        cs_w = []
        for pair in range(GROUP_X // LANES):
            blk = cs_x[:, pair * LANES:(pair + 1) * LANES]
            swapped = pltpu.roll(blk, SSD_HEAD_DIM, axis=1)
            cs_w += [jnp.where(low_half, blk, swapped), jnp.where(low_half, swapped, blk)]
        dta_t = (dt * a_ref[0]).T[off:off + 8, :]
        cs_t = _dot(jnp.concatenate(split(dta_t), axis=1), k["trit_cat"])
        tot = cs_x[0:1, :] if reverse else cs_x[q - 1:q, :]
        cm_b = cm.astype(BF16)
        cbm = _dot_nt(cm_b, bm.astype(BF16))
        xdt = xa * dt_x
        xdt_b = xdt.astype(BF16)
        ydiag = None
        for r in range(SSD_HEADS_PER_GROUP):
            seg = cs_w[r] - cs_t[r:r + 1, :]
            w = (jnp.exp(jnp.where(tri, seg, NEG_INF)) * cbm).astype(BF16)
            yr = _dot(w, xdt_b)
            ydiag = yr if r == 0 else jnp.where(lane >= r * SSD_HEAD_DIM, yr, ydiag)
        xs_b = (xdt * jnp.exp(tot - cs_x)).astype(BF16)
        states = _dot(bm.T.astype(BF16), xs_b)
        hprev = st_ref[di]
        yoff = _dot(cm_b, hprev.astype(BF16)) * jnp.exp(cs_x)
        st_ref[di] = hprev * jnp.exp(tot) + states
        yacc_ref[di, rows, :] = ydiag + yoff

    def body(ci, carry):
        chunk(ci, False)
        chunk(ci, True)
        return carry

    lax.fori_loop(0, nchunks, body, 0, unroll=SSD_UNROLL)
    y_ref[0] = (yacc_ref[0] + yacc_ref[1] + act_ref[:, 0:GROUP_X] * dsk_ref[0]).astype(BF16)


def _ssd(xbc, dt, conv_w_g, conv_b_g, dt_bias_g, a_g, dskip_g):
    bsz, s, _ = xbc.shape
    a_lanes, a_x = a_g
    grp = lambda n: pl.BlockSpec((1,) + n, lambda bi, gi: (gi,) + (0,) * len(n))
    in_specs = [
        pl.BlockSpec((1, s, GROUP_X), lambda bi, gi: (bi, 0, gi)),
        pl.BlockSpec((1, s, SSD_STATE), lambda bi, gi: (bi, 0, SSD_WIDTH // SSD_STATE + gi)),
        pl.BlockSpec((1, s, SSD_STATE), lambda bi, gi: (bi, 0, (SSD_WIDTH + GN) // SSD_STATE + gi)),
        pl.BlockSpec((1, s, LANES), lambda bi, gi: (bi, 0, gi)),
        grp((SSD_CONV, GROUP_CH)), grp((1, GROUP_CH)), grp((1, LANES)), grp((1, LANES)),
        grp((2, 1, GROUP_X)), grp((1, GROUP_X)),
    ]
    consts = _ssd_constants()
    in_specs += [pl.BlockSpec(c.shape, lambda bi, gi: (0, 0, 0)) for c in consts]
    return pl.pallas_call(
        _ssd_kernel,
        grid=(bsz, SSD_GROUPS),
        in_specs=in_specs,
        out_specs=pl.BlockSpec((1, s, GROUP_X), lambda bi, gi: (bi, 0, gi)),
        out_shape=jax.ShapeDtypeStruct((bsz, s, SSD_WIDTH), BF16),
        scratch_shapes=[
            pltpu.VMEM((s + 2 * PAD_ROWS, GROUP_CH), F32),
            pltpu.VMEM((s, GROUP_CH), F32),
            pltpu.VMEM((2, s, GROUP_X), F32),
            pltpu.VMEM((2, SSD_STATE, GROUP_X), F32),
        ],
        compiler_params=_cparams("parallel", "parallel"),
        name="ssd_scan",
    )(xbc, xbc, xbc, dt, conv_w_g, conv_b_g, dt_bias_g, a_lanes, a_x, dskip_g, *consts)


ATT_QBLOCK = 128
ATT_PAIR = 2 * ATT_HEAD_DIM
ATT_WINDOW = 2 * ATT_QBLOCK


def _att_kernel(q1, k1, v1, q4, k4, v4, q16, k16, v16, slope_ref, o_ref, km_ref, vm_ref, bias_ref, ob_ref, lb_ref):
    s = o_ref.shape[1]
    qb = ATT_QBLOCK
    first = lax.broadcasted_iota(I32, (qb, ATT_PAIR), 1) < ATT_HEAD_DIM
    first_all = lax.broadcasted_iota(I32, (s, ATT_PAIR), 1) < ATT_HEAD_DIM
    slope_lanes = slope_ref[0]

    def branch(bi, dil, q_ref, k_ref, v_ref):
        seg_len = s // dil
        win = min(ATT_WINDOW, seg_len)
        kk = k_ref[0, 0]
        vv = v_ref[0, 0]
        zero = jnp.zeros_like(kk)
        km_ref[0] = jnp.where(first_all, kk, zero)
        km_ref[1] = jnp.where(first_all, zero, kk)
        vm_ref[0] = jnp.where(first_all, vv, zero)
        vm_ref[1] = jnp.where(first_all, zero, vv)
        shifts = sorted({min(max(w0 - ATT_RADIUS, 0), seg_len - win) - w0 for w0 in range(0, seg_len, qb)},
                        reverse=True)
        rel0 = lax.broadcasted_iota(I32, (qb, win), 1) - lax.broadcasted_iota(I32, (qb, win), 0)
        for vi, shift in enumerate(shifts):
            rel = jnp.abs(rel0 + shift)
            dist = (dil * rel).astype(F32)
            for j in range(2):
                slope = slope_lanes[:, j * ATT_HEAD_DIM:j * ATT_HEAD_DIM + 1]
                bias_ref[vi, :, j * win:(j + 1) * win] = jnp.where(rel <= ATT_RADIUS, -slope * dist, NEG_INF)

        for blk in range(s // qb):
            qs = blk * qb
            seg, within = divmod(qs, seg_len)
            kw = min(max(within - ATT_RADIUS, 0), seg_len - win)
            ks = seg * seg_len + kw
            vi = shifts.index(kw - within)
            kcat = jnp.concatenate([km_ref[0, ks:ks + win, :], km_ref[1, ks:ks + win, :]], axis=0)
            vcat = jnp.concatenate([vm_ref[0, ks:ks + win, :], vm_ref[1, ks:ks + win, :]], axis=0)
            sc = _dot_nt(q_ref[0, 0, qs:qs + qb, :], kcat) + bias_ref[vi, :, 0:2 * win]
            s0, s1 = sc[:, :win], sc[:, win:]
            m0 = jnp.max(s0, axis=-1, keepdims=True)
            m1 = jnp.max(s1, axis=-1, keepdims=True)
            p0 = jnp.exp(s0 - m0)
            p1 = jnp.exp(s1 - m1)
            l0 = jnp.sum(p0, axis=-1, keepdims=True)
            l1 = jnp.sum(p1, axis=-1, keepdims=True)
            o = _dot(jnp.concatenate([p0, p1], axis=1).astype(BF16), vcat)
            o_blk = o * jnp.where(first, 1.0 / l0, 1.0 / l1)
            l_blk = jnp.where(first, m0 + jnp.log(l0), m1 + jnp.log(l1))
            if dil == 1:
                dst = pl.ds(qs, qb)
            else:
                dst = pl.ds(within * dil + seg, qb, stride=dil)
            ob_ref[bi, dst, :] = o_blk
            lb_ref[bi, dst, :] = l_blk

    branch(0, DILATIONS[0], q1, k1, v1)
    branch(1, DILATIONS[1], q4, k4, v4)
    branch(2, DILATIONS[2], q16, k16, v16)

    rows = 256
    for i in range(s // rows):
        sl = slice(i * rows, (i + 1) * rows)
        l0, l1, l2 = lb_ref[0, sl, :], lb_ref[1, sl, :], lb_ref[2, sl, :]
        m = jnp.maximum(jnp.maximum(l0, l1), l2)
        w0, w1, w2 = jnp.exp(l0 - m), jnp.exp(l1 - m), jnp.exp(l2 - m)
        num = w0 * ob_ref[0, sl, :] + w1 * ob_ref[1, sl, :] + w2 * ob_ref[2, sl, :]
        o_ref[0, sl, :] = (num / (w0 + w1 + w2)).astype(BF16)


def _dilated_attention(qkv1, qkv4, qkv16, slopes):
    _, bsz, s, _ = qkv1.shape
    npairs = ATT_WIDTH // ATT_PAIR
    spec = lambda c: pl.BlockSpec((1, 1, s, ATT_PAIR), lambda bi, hp, c=c: (c, bi, 0, hp))
    in_specs = [spec(0), spec(1), spec(2)] * 3 + [pl.BlockSpec((1, 1, ATT_PAIR), lambda bi, hp: (hp, 0, 0))]
    return pl.pallas_call(
        _att_kernel,
        grid=(bsz, npairs),
        in_specs=in_specs,
        out_specs=pl.BlockSpec((1, s, ATT_PAIR), lambda bi, hp: (bi, 0, hp)),
        out_shape=jax.ShapeDtypeStruct((bsz, s, ATT_WIDTH), BF16),
        scratch_shapes=[pltpu.VMEM((2, s, ATT_PAIR), BF16), pltpu.VMEM((2, s, ATT_PAIR), BF16),
                        pltpu.VMEM((3, ATT_QBLOCK, 2 * ATT_WINDOW), F32),
                        pltpu.VMEM((3, s, ATT_PAIR), F32), pltpu.VMEM((3, s, ATT_PAIR), F32)],
        compiler_params=_cparams("parallel", "parallel"),
        name="dilated_attention",
    )(qkv1, qkv1, qkv1, qkv4, qkv4, qkv4, qkv16, qkv16, qkv16, slopes)


def _outproj_kernel(h_ref, yssd_ref, z_ref, yatt_ref, nw_ref, wo_ref, g_ref, b_ref, o_ref):
    y = yssd_ref[...].astype(F32) * _silu(z_ref[...].astype(F32))
    yn = y * lax.rsqrt(jnp.mean(y * y, -1, keepdims=True) + RMS_EPS) * nw_ref[...]
    mix = _dot(yn.astype(BF16), wo_ref[0:SSD_WIDTH, :]) + _dot(yatt_ref[...], wo_ref[SSD_WIDTH:, :])
    o_ref[...] = _ln_rows(DN_ALPHA * h_ref[...] + mix, g_ref[...], b_ref[...])


def _outproj(h, yssd, z, yatt, norm_w, w_out, g, b, tm=512):
    t, d = h.shape
    const = lambda *_: (0, 0)
    wspec = lambda w: pl.BlockSpec(w.shape, const)
    row = lambda n: pl.BlockSpec((tm, n), lambda i: (i, 0))
    return pl.pallas_call(
        _outproj_kernel,
        grid=(t // tm,),
        in_specs=[row(d), row(SSD_WIDTH), row(SSD_WIDTH), row(ATT_WIDTH),
                  wspec(norm_w), wspec(w_out), wspec(g), wspec(b)],
        out_specs=row(d),
        out_shape=jax.ShapeDtypeStruct((t, d), F32),
        compiler_params=_cparams("parallel"),
        name="outproj_ln",
    )(h, yssd, z, yatt, norm_w, w_out, g, b)


def _alibi_slopes():
    sl = np.array([2.0 ** (-8.0 * (h + 1) / ATT_HEADS) for h in range(ATT_HEADS)], dtype=np.float32)
    return jnp.asarray(np.repeat(sl, ATT_HEAD_DIM).reshape(ATT_WIDTH // ATT_PAIR, 1, ATT_PAIR))


def _mixer_params(w_in, conv_w, conv_b, dt_bias_f, dt_bias_b, a_log_f, a_log_b, d_skip):
    r = SSD_HEADS_PER_GROUP
    wz = w_in[:, 0:XBC_OFF].astype(BF16)
    wxbc = w_in[:, XBC_OFF:DT_OFF].astype(BF16)
    wqkv = w_in[:, Q_OFF:IN_COLS].astype(BF16)
    wdt_f = w_in[:, DT_OFF:DT_OFF + SSD_HEADS].reshape(D_MODEL, SSD_GROUPS, r)
    wdt_b = w_in[:, DT_OFF + SSD_HEADS:Q_OFF].reshape(D_MODEL, SSD_GROUPS, r)
    lane_pad = ((0, 0), (0, 0), (0, LANES - 2 * r))
    wdt = jnp.pad(jnp.concatenate([wdt_f, wdt_b], -1), lane_pad).reshape(D_MODEL, SSD_GROUPS * LANES).astype(BF16)

    def per_group_lanes(f, b):
        v = jnp.concatenate([f.reshape(SSD_GROUPS, 1, r), b.reshape(SSD_GROUPS, 1, r)], -1)
        return jnp.pad(v.astype(F32), lane_pad)

    dtb = per_group_lanes(dt_bias_f, dt_bias_b)
    a_f = -jnp.exp(a_log_f.astype(F32))
    a_b = -jnp.exp(a_log_b.astype(F32))

    def per_head_lanes(n):
        v = jnp.stack([a_f.reshape(SSD_GROUPS, r), a_b.reshape(SSD_GROUPS, r)], axis=1)
        return jnp.repeat(v, n, axis=-1).reshape(SSD_GROUPS, 2, 1, r * n)

    a = (per_group_lanes(a_f, a_b), per_head_lanes(SSD_HEAD_DIM))

    def group_cols(w):
        xs = w[..., 0:SSD_WIDTH].reshape(w.shape[:-1] + (SSD_GROUPS, GROUP_X))
        bs = w[..., SSD_WIDTH:SSD_WIDTH + GN].reshape(w.shape[:-1] + (SSD_GROUPS, SSD_STATE))
        cs = w[..., SSD_WIDTH + GN:].reshape(w.shape[:-1] + (SSD_GROUPS, SSD_STATE))
        return jnp.moveaxis(jnp.concatenate([xs, bs, cs], -1), -2, 0)

    cw = group_cols(conv_w.astype(F32))
    cb = group_cols(conv_b.astype(F32)[None])
    dsk = jnp.repeat(d_skip.astype(F32), SSD_HEAD_DIM).reshape(SSD_GROUPS, 1, GROUP_X)
    return wz, wxbc, wdt, wqkv, cw, cb, dtb, a, dsk


def _mixer_layer(x, ln_in_g, ln_in_b, w_in, conv_w, conv_b, dt_bias_f, dt_bias_b, a_log_f, a_log_b,
                 d_skip, ssd_norm_w, w_out_mix, ln1_g, ln1_b):
    bsz, s, d = x.shape
    vec = lambda v: v.astype(F32).reshape(1, -1)
    wz, wxbc, wdt, wqkv, cw, cb, dtb, a, dsk = _mixer_params(
        w_in, conv_w, conv_b, dt_bias_f, dt_bias_b, a_log_f, a_log_b, d_skip)
    h, z, xbc, dt, qkv1, qkv4, qkv16 = _inproj(x, vec(ln_in_g), vec(ln_in_b), wz, wxbc, wdt, wqkv)
    yssd = _ssd(xbc, dt, cw, cb, dtb, a, dsk)
    yatt = _dilated_attention(qkv1, qkv4.reshape(qkv1.shape), qkv16.reshape(qkv1.shape), _alibi_slopes())
    t = bsz * s
    return _outproj(h.reshape(t, d), yssd.reshape(t, SSD_WIDTH), z.reshape(t, SSD_WIDTH),
                    yatt.reshape(t, ATT_WIDTH), vec(ssd_norm_w), w_out_mix.astype(BF16), vec(ln1_g), vec(ln1_b))


def _memkv_kernel(mem_ref, wk_ref, wv_ref, k_ref, v_ref):
    m = mem_ref[0].astype(BF16)
    k_ref[0] = _dot(m, wk_ref[...]).astype(BF16)
    v_ref[0] = _dot(m, wv_ref[...]).astype(BF16)


def _memkv(mem, wk, wv):
    bsz, m, d = mem.shape
    const = lambda *_: (0, 0)
    blk = pl.BlockSpec((1, m, d), lambda bi: (bi, 0, 0))
    return pl.pallas_call(
        _memkv_kernel,
        grid=(bsz,),
        in_specs=[blk, pl.BlockSpec(wk.shape, const), pl.BlockSpec(wv.shape, const)],
        out_specs=(blk, blk),
        out_shape=(jax.ShapeDtypeStruct((bsz, m, d), BF16),) * 2,
        compiler_params=_cparams("parallel"),
        name="mem_kv",
    )(mem, wk, wv)


def _memattn_kernel(h_ref, k_ref, v_ref, wq_ref, wo_ref, g_ref, b_ref, o_ref, ow_ref):
    h = h_ref[0]
    q = (_dot(h.astype(BF16), wq_ref[...]) * (MEM_HEAD_DIM ** -0.5)).astype(BF16)
    xa = None
    for hd in range(MEM_HEADS):
        cols = slice(hd * MEM_HEAD_DIM, (hd + 1) * MEM_HEAD_DIM)
        sc = _dot_nt(q[:, cols], k_ref[0, :, cols])
        p = jnp.exp(sc - jnp.max(sc, axis=-1, keepdims=True))
        o = _dot(p.astype(BF16), v_ref[0, :, cols]) / jnp.sum(p, axis=-1, keepdims=True)
        part = _dot(o.astype(BF16), wo_ref[cols, :])
        xa = part if xa is None else xa + part
    out = _ln_rows(DN_ALPHA * h + xa, g_ref[...], b_ref[...])
    o_ref[0] = out
    ow_ref[0] = _pack_bf16_pairs(out)


def _memattn(h, k, v, wq, wo, g, b, tm=512):
    bsz, s, d = h.shape
    m = k.shape[1]
    const = lambda *_: (0, 0)
    wspec = lambda w: pl.BlockSpec(w.shape, const)
    row = pl.BlockSpec((1, tm, d), lambda bi, i: (bi, i, 0))
    kv = pl.BlockSpec((1, m, d), lambda bi, i: (bi, 0, 0))
    return pl.pallas_call(
        _memattn_kernel,
        grid=(bsz, s // tm),
        in_specs=[row, kv, kv, wspec(wq), wspec(wo), wspec(g), wspec(b)],
        out_specs=(row, pl.BlockSpec((1, tm, d // 2), lambda bi, i: (bi, i, 0))),
        out_shape=(jax.ShapeDtypeStruct((bsz, s, d), F32), jax.ShapeDtypeStruct((bsz, s, d // 2), I32)),
        compiler_params=_cparams("parallel", "parallel"),
        name="mem_attn_ln",
    )(h, k, v, wq, wo, g, b)


def _first_max(work, idx_iota, sentinel):
    m = jnp.max(work, axis=0, keepdims=True)
    idx = jnp.min(jnp.where(work == m, idx_iota, sentinel), axis=0, keepdims=True)
    return m, idx


def _router_kernel(h_ref, wrt_ref, bias_ref, eidx_ref, gate_ref, rank_ref, cnt_ref, run_ref):
    tm = h_ref.shape[0]
    e = N_EXPERTS

    @pl.when(pl.program_id(0) == 0)
    def _():
        run_ref[...] = jnp.zeros_like(run_ref)

    logits = lax.dot_general(wrt_ref[...], h_ref[...], (((1,), (1,)), ((), ())),
                             precision=lax.Precision.HIGHEST, preferred_element_type=F32)
    scores = jax.nn.sigmoid(logits)
    choice = scores + bias_ref[...]
    row = lax.broadcasted_iota(I32, (e, tm), 0)
    grow = lax.broadcasted_iota(I32, (GROUP_SIZE, tm), 0)
    gs = []
    for g in range(ROUTE_GROUPS):
        blk = choice[g * GROUP_SIZE:(g + 1) * GROUP_SIZE, :]
        m1, i1 = _first_max(blk, grow, GROUP_SIZE)
        m2 = jnp.max(jnp.where(grow == i1, -jnp.inf, blk), axis=0, keepdims=True)
        gs.append(m1 + m2)
    work = jnp.concatenate(gs, axis=0)
    giota = lax.broadcasted_iota(I32, (ROUTE_GROUPS, tm), 0)
    gmask = jnp.zeros((ROUTE_GROUPS, tm), jnp.bool_)
    for _ in range(TOPK_GROUPS):
        _, gi = _first_max(work, giota, ROUTE_GROUPS)
        hit = giota == gi
        gmask = jnp.logical_or(gmask, hit)
        work = jnp.where(hit, -jnp.inf, work)
    work = jnp.concatenate(
        [jnp.where(gmask[g:g + 1, :], choice[g * GROUP_SIZE:(g + 1) * GROUP_SIZE, :], NEG_INF)
         for g in range(ROUTE_GROUPS)], axis=0)
    chosen = jnp.zeros((e, tm), jnp.bool_)
    idxs, svals = [], []
    for _ in range(TOP_K):
        _, ei = _first_max(work, row, e)
        hit = row == ei
        svals.append(jnp.sum(jnp.where(hit, scores, 0.0), axis=0, keepdims=True))
        idxs.append(ei)
        chosen = jnp.logical_or(chosen, hit)
        work = jnp.where(hit, -jnp.inf, work)
    eidx = jnp.concatenate(idxs, axis=0)
    sv = jnp.concatenate(svals, axis=0)
    eidx_ref[...] = eidx
    gate_ref[...] = sv / jnp.sum(sv, axis=0, keepdims=True) * ROUTED_SCALE
    upper = (lax.broadcasted_iota(I32, (tm, tm), 0) < lax.broadcasted_iota(I32, (tm, tm), 1))
    mt = chosen.astype(F32)
    cum = _dot(mt.astype(BF16), upper.astype(F32).astype(BF16)) + run_ref[...]
    rank_ref[...] = jnp.concatenate(
        [jnp.sum(jnp.where(row == idxs[k], cum, 0.0), axis=0, keepdims=True) for k in range(TOP_K)],
        axis=0).astype(I32)
    run_ref[...] += jnp.sum(mt, axis=1, keepdims=True)
    cnt_ref[...] = run_ref[...]


def _router(h, w_router_t, bias_col, tm=512):
    t, d = h.shape
    const = lambda *_: (0, 0)
    col = lambda dt: pl.BlockSpec((TOP_K, tm), lambda i: (0, i))
    return pl.pallas_call(
        _router_kernel,
        grid=(t // tm,),
        in_specs=[pl.BlockSpec((tm, d), lambda i: (i, 0)), pl.BlockSpec(w_router_t.shape, const),
                  pl.BlockSpec(bias_col.shape, const)],
        out_specs=(col(I32), col(F32), col(I32), pl.BlockSpec((N_EXPERTS, 1), const)),
        out_shape=(jax.ShapeDtypeStruct((TOP_K, t), I32), jax.ShapeDtypeStruct((TOP_K, t), F32),
                   jax.ShapeDtypeStruct((TOP_K, t), I32), jax.ShapeDtypeStruct((N_EXPERTS, 1), F32)),
        scratch_shapes=[pltpu.VMEM((N_EXPERTS, 1), F32)],
        compiler_params=_cparams("arbitrary"),
        name="router_topk",
    )(h, w_router_t, bias_col)


def _dispatch_kernel(cnt_ref, eidx_ref, rank_ref, dest_ref, bstart_ref, nb_ref):
    tm = eidx_ref.shape[1]
    e = N_EXPERTS
    nb = jnp.floor((cnt_ref[...] + (MOE_BLOCK - 1)) * (1.0 / MOE_BLOCK))
    strict = (lax.broadcasted_iota(I32, (e, e), 1) < lax.broadcasted_iota(I32, (e, e), 0))
    bstart = _dot(strict.astype(F32).astype(BF16), jnp.broadcast_to(nb, (e, LANES)).astype(BF16))[:, 0:1]
    pstart = bstart * MOE_BLOCK
    row = lax.broadcasted_iota(I32, (e, tm), 0)
    eidx = eidx_ref[...]
    rank = rank_ref[...]
    dest_ref[...] = jnp.concatenate(
        [jnp.sum(jnp.where(row == eidx[k:k + 1, :], pstart, 0.0), axis=0, keepdims=True).astype(I32)
         + rank[k:k + 1, :] for k in range(TOP_K)], axis=0)
    bstart_ref[...] = bstart.astype(I32)
    nb_ref[...] = nb.astype(I32)


def _dispatch(cnt, eidx, rank, tm=2048):
    t = eidx.shape[1]
    const = lambda *_: (0, 0)
    col = pl.BlockSpec((TOP_K, tm), lambda i: (0, i))
    per_expert = pl.BlockSpec((N_EXPERTS, 1), const)
    return pl.pallas_call(
        _dispatch_kernel,
        grid=(t // tm,),
        in_specs=[per_expert, col, col],
        out_specs=(col, per_expert, per_expert),
        out_shape=(jax.ShapeDtypeStruct((TOP_K, t), I32), jax.ShapeDtypeStruct((N_EXPERTS, 1), I32),
                   jax.ShapeDtypeStruct((N_EXPERTS, 1), I32)),
        compiler_params=_cparams("arbitrary"),
        name="dispatch_slots",
    )(cnt, eidx, rank)


EXPERT_ROW_BUFFERS = 4


def _experts_kernel(bstart_ref, nb_ref, xs_hbm, wg_ref, wu_ref, wd_ref, ys_hbm,
                    xbuf, ybuf, wg_b, wu_b, wd_b, in_sem, out_sem):
    e = pl.program_id(0)
    last_e = pl.num_programs(0) - 1
    first_block = bstart_ref[e]
    nused = bstart_ref[last_e] + nb_ref[last_e]
    nbuf = xbuf.shape[0]

    def block_rows(blk):
        return pl.ds(pl.multiple_of(blk * MOE_BLOCK, MOE_BLOCK), MOE_BLOCK)

    def in_copy(blk, slot):
        return pltpu.make_async_copy(xs_hbm.at[block_rows(blk)], xbuf.at[slot], in_sem.at[slot])

    def out_copy(blk, slot):
        return pltpu.make_async_copy(ybuf.at[slot], ys_hbm.at[block_rows(blk)], out_sem.at[slot])

    @pl.when(e == 0)
    def _():
        for ahead in range(nbuf - 1):
            @pl.when(ahead < nused)
            def _():
                in_copy(ahead, ahead).start()

    wg_b[...] = wg_ref[0].astype(BF16)
    wu_b[...] = wu_ref[0].astype(BF16)
    wd_b[...] = wd_ref[0].astype(BF16)

    def body(j, carry):
        blk = first_block + j
        slot = blk % nbuf
        in_copy(blk, slot).wait()
        nxt = blk + nbuf - 1

        @pl.when(nxt < nused)
        def _():
            in_copy(nxt, nxt % nbuf).start()

        @pl.when(blk >= nbuf)
        def _():
            out_copy(blk - nbuf, slot).wait()

        x = _unpack_bf16_pairs(xbuf[slot]).astype(BF16)
        g = _dot(x, wg_b[...])
        u = _dot(x, wu_b[...])
        a = (_silu(g) * u).astype(BF16)
        ybuf[slot] = _pack_bf16_pairs(_dot(a, wd_b[...]))
        out_copy(blk, slot).start()
        return carry

    lax.fori_loop(0, nb_ref[e], body, 0)

    @pl.when(e == last_e)
    def _():
        for back in range(1, nbuf + 1):
            @pl.when(nused >= back)
            def _():
                out_copy(nused - back, (nused - back) % nbuf).wait()


def _experts(bstart, nb, xs, w_gate, w_up, w_down):
    n_pad, words = xs.shape
    n_exp, d, f = w_gate.shape
    wspec = lambda w: pl.BlockSpec((1,) + w.shape[1:], lambda e, bs, nb: (e, 0, 0))
    hbm = pl.BlockSpec(memory_space=pl.ANY)
    return pl.pallas_call(
        _experts_kernel,
        grid_spec=pltpu.PrefetchScalarGridSpec(
            num_scalar_prefetch=2,
            grid=(n_exp,),
            in_specs=[hbm, wspec(w_gate), wspec(w_up), wspec(w_down)],
            out_specs=hbm,
            scratch_shapes=[
                pltpu.VMEM((EXPERT_ROW_BUFFERS, MOE_BLOCK, words), I32),
                pltpu.VMEM((EXPERT_ROW_BUFFERS, MOE_BLOCK, words), I32),
                pltpu.VMEM((d, f), BF16), pltpu.VMEM((d, f), BF16), pltpu.VMEM((f, d), BF16),
                pltpu.SemaphoreType.DMA((EXPERT_ROW_BUFFERS,)), pltpu.SemaphoreType.DMA((EXPERT_ROW_BUFFERS,)),
            ],
        ),
        out_shape=jax.ShapeDtypeStruct((n_pad, words), I32),
        compiler_params=_cparams("arbitrary"),
        name="routed_experts",
    )(bstart, nb, xs, w_gate, w_up, w_down)


def _final_kernel(h_ref, yg_ref, gate_ref, wsg_ref, wsu_ref, wsd_ref, g_ref, b_ref, o_ref):
    h = h_ref[...]
    d = h.shape[1]
    hb = h.astype(BF16)
    ff = _dot((_silu(_dot(hb, wsg_ref[...])) * _dot(hb, wsu_ref[...])).astype(BF16), wsd_ref[...])
    gate = gate_ref[...]
    for k in range(TOP_K):
        ff = ff + gate[:, k:k + 1] * _unpack_bf16_pairs(yg_ref[k])
    o_ref[...] = _ln_rows(DN_ALPHA * h + ff, g_ref[...], b_ref[...])


def _final(h, yg, gate, wsg, wsu, wsd, g, b, tm=256):
    t, d = h.shape
    const = lambda *_: (0, 0)
    wspec = lambda w: pl.BlockSpec(w.shape, const)
    row = lambda n: pl.BlockSpec((tm, n), lambda i: (i, 0))
    return pl.pallas_call(
        _final_kernel,
        grid=(t // tm,),
        in_specs=[row(d), pl.BlockSpec((TOP_K, tm, d // 2), lambda i: (0, i, 0)), row(TOP_K), wspec(wsg), wspec(wsu), wspec(wsd), wspec(g), wspec(b)],
        out_specs=row(d),
        out_shape=jax.ShapeDtypeStruct((t, d), F32),
        compiler_params=_cparams("parallel"),
        name="shared_combine_ln",
    )(h, yg, gate, wsg, wsu, wsd, g, b)


SC_CORES = 2
SC_SUBCORES = 16
SC_WORKERS = SC_CORES * SC_SUBCORES
SC_WINDOW = 64


def _sc_scatter_rows(x_words, dest_kmajor, n_rows):
    t, w = x_words.shape
    per_worker = t // SC_WORKERS
    nchunk = per_worker // SC_WINDOW
    mesh = plsc.VectorSubcoreMesh(core_axis_name="c", subcore_axis_name="s")

    @functools.partial(
        pl.kernel, mesh=mesh,
        out_type=jax.ShapeDtypeStruct((n_rows, w), I32),
        scratch_types=[pltpu.VMEM((SC_WINDOW,), I32), pltpu.VMEM((SC_WINDOW, w), I32), pltpu.SemaphoreType.DMA],
    )
    def scatter(x_hbm, dest_hbm, out_hbm, idx_v, rows_v, sem):
        wid = lax.axis_index("s") * SC_CORES + lax.axis_index("c")
        base = wid * per_worker

        @pl.loop(0, nchunk)
        def _(i):
            t0 = base + i * SC_WINDOW
            pltpu.sync_copy(x_hbm.at[pl.ds(t0, SC_WINDOW)], rows_v)
            for k in range(TOP_K):
                pltpu.sync_copy(dest_hbm.at[pl.ds(k * t + t0, SC_WINDOW)], idx_v)
                pltpu.async_copy(rows_v, out_hbm.at[idx_v], sem).wait()

    return scatter(x_words, dest_kmajor)


def _sc_gather_rows(table, idx):
    a = idx.shape[0]
    w = table.shape[1]
    per_worker = a // SC_WORKERS
    nchunk = per_worker // SC_WINDOW
    mesh = plsc.VectorSubcoreMesh(core_axis_name="c", subcore_axis_name="s")

    @functools.partial(
        pl.kernel, mesh=mesh,
        out_type=jax.ShapeDtypeStruct((a, w), I32),
        scratch_types=[pltpu.VMEM((SC_WINDOW,), I32), pltpu.VMEM((SC_WINDOW, w), I32), pltpu.SemaphoreType.DMA],
    )
    def gather(table_hbm, idx_hbm, out_hbm, idx_v, rows_v, sem):
        wid = lax.axis_index("s") * SC_CORES + lax.axis_index("c")
        base = wid * per_worker

        @pl.loop(0, nchunk)
        def _(i):
            off = base + i * SC_WINDOW
            pltpu.sync_copy(idx_hbm.at[pl.ds(off, SC_WINDOW)], idx_v)
            pltpu.async_copy(table_hbm.at[idx_v], rows_v, sem).wait()
            pltpu.sync_copy(rows_v, out_hbm.at[pl.ds(off, SC_WINDOW)])

    return gather(table, idx)


def _moe_layer(h2, h2_words, w_router, router_bias, w_gate, w_up, w_down, ws_gate, ws_up, ws_down, ln3_g, ln3_b):
    t, d = h2.shape
    vec = lambda v: v.astype(F32).reshape(1, -1)
    eidx, gate, rank, cnt = _router(h2, w_router.astype(F32).T, router_bias.astype(F32).reshape(-1, 1))
    n_blocks = t * TOP_K // MOE_BLOCK + N_EXPERTS
    dest, bstart, nb = _dispatch(cnt, eidx, rank)
    dest_flat = dest.reshape(TOP_K * t)
    xs = _sc_scatter_rows(h2_words, dest_flat, n_blocks * MOE_BLOCK)
    ys = _experts(bstart.reshape(N_EXPERTS), nb.reshape(N_EXPERTS), xs, w_gate, w_up, w_down)
    yg = _sc_gather_rows(ys, dest_flat).reshape(TOP_K, t, d // 2)
    return _final(h2, yg, gate.T, ws_gate.astype(BF16), ws_up.astype(BF16), ws_down.astype(BF16),
                  vec(ln3_g), vec(ln3_b))


def kernel(x, mem, ln_in_g, ln_in_b, w_in, conv_w, conv_b, dt_bias_f, dt_bias_b, a_log_f, a_log_b, d_skip, ssd_norm_w, w_out_mix, ln1_g, ln1_b, wq_mem, wk_mem, wv_mem, wo_mem, ln2_g, ln2_b, w_router, router_bias, w_gate, w_up, w_down, ws_gate, ws_up, ws_down, ln3_g, ln3_b):
    l = 0
    h1 = _mixer_layer(x, ln_in_g, ln_in_b, w_in[l], conv_w[l], conv_b[l], dt_bias_f[l], dt_bias_b[l],
                      a_log_f[l], a_log_b[l], d_skip[l], ssd_norm_w[l], w_out_mix[l], ln1_g[l], ln1_b[l])
    bsz, s, d = x.shape
    vec = lambda v: v.astype(F32).reshape(1, -1)
    km, vm = _memkv(mem, wk_mem[l].astype(BF16), wv_mem[l].astype(BF16))
    h2, h2_words = _memattn(h1.reshape(bsz, s, d), km, vm, wq_mem[l].astype(BF16), wo_mem[l].astype(BF16),
                            vec(ln2_g[l]), vec(ln2_b[l]))
    out = _moe_layer(h2.reshape(bsz * s, d), h2_words.reshape(bsz * s, d // 2), w_router[l], router_bias[l], w_gate[l], w_up[l], w_down[l],
                     ws_gate[l], ws_up[l], ws_down[l], ln3_g[l], ln3_b[l])
    return out.reshape(x.shape)
```

```python
import functools
import math

import numpy as np
import jax
import jax.numpy as jnp
from jax import lax
from jax.experimental import pallas as pl
from jax.experimental.pallas import tpu as pltpu
from jax.experimental.pallas import tpu_sc as plsc

F32 = jnp.float32
BF16 = jnp.bfloat16
I32 = jnp.int32

D_MODEL = 1024
SSD_WIDTH = 1024
SSD_HEAD_DIM = 64
SSD_HEADS = 16
SSD_STATE = 128
SSD_GROUPS = 4
SSD_HEADS_PER_GROUP = SSD_HEADS // SSD_GROUPS
SSD_CONV = 5
SSD_CHUNK = 128
ATT_WIDTH = 1024
ATT_HEAD_DIM = 64
ATT_HEADS = 16
ATT_RADIUS = 64
DILATIONS = (1, 4, 16)
MEM_HEADS = 4
MEM_HEAD_DIM = 256
N_EXPERTS = 256
ROUTE_GROUPS = 8
GROUP_SIZE = N_EXPERTS // ROUTE_GROUPS
TOPK_GROUPS = 4
TOP_K = 8
EXPERT_DIM = 256
ROUTED_SCALE = 2.5
MOE_BLOCK = 256
DN_ALPHA = 2.0 ** 0.25
LN_EPS = 1e-5
RMS_EPS = 1e-5
NEG_INF = -1e30
GN = SSD_GROUPS * SSD_STATE
CONV_CH = SSD_WIDTH + 2 * GN
XBC_OFF = SSD_WIDTH
DT_OFF = XBC_OFF + CONV_CH
Q_OFF = DT_OFF + 2 * SSD_HEADS
IN_COLS = Q_OFF + 3 * ATT_WIDTH

LANES = 128
VMEM_LIMIT_BYTES = 56 * 1024 * 1024


def _cparams(*sem):
    return pltpu.CompilerParams(dimension_semantics=sem, vmem_limit_bytes=VMEM_LIMIT_BYTES)


def _ln_rows(x, g, b):
    mu = jnp.mean(x, -1, keepdims=True)
    xc = x - mu
    var = jnp.mean(xc * xc, -1, keepdims=True)
    return xc * lax.rsqrt(var + LN_EPS) * g + b


def _dot(a, b):
    return jnp.dot(a, b, preferred_element_type=F32)


def _dot_nt(a, b):
    return lax.dot_general(a, b, (((1,), (1,)), ((), ())), preferred_element_type=F32)


def _silu(x):
    return x * jax.nn.sigmoid(x)


def _pack_bf16_pairs(x):
    n = x.shape[1] // 2
    u = lax.bitcast_convert_type(x.astype(BF16).astype(F32), jnp.uint32)
    word = (u[:, :n] >> 16) | (u[:, n:] & jnp.uint32(0xFFFF0000))
    return lax.bitcast_convert_type(word, I32)


def _unpack_bf16_pairs(w):
    u = lax.bitcast_convert_type(w, jnp.uint32)
    lo = lax.bitcast_convert_type(u << 16, F32)
    hi = lax.bitcast_convert_type(u & jnp.uint32(0xFFFF0000), F32)
    return jnp.concatenate([lo, hi], axis=1)


def _inproj_kernel(x_ref, g_ref, b_ref, wz_ref, wxbc_ref, wdt_ref, wqkv_ref,
                   h_ref, z_ref, xbc_ref, dt_ref, qkv1_ref, qkv4_ref, qkv16_ref, stage_ref):
    tm = x_ref.shape[1]
    h = _ln_rows(x_ref[0], g_ref[...], b_ref[...])
    h_ref[0] = h
    hb = h.astype(BF16)
    z_ref[0] = _dot(hb, wz_ref[...]).astype(BF16)
    for c in range(CONV_CH // D_MODEL):
        cols = slice(c * D_MODEL, (c + 1) * D_MODEL)
        xbc_ref[0, :, cols] = _dot(hb, wxbc_ref[:, cols]).astype(BF16)
    dt_ref[0] = _dot(hb, wdt_ref[...])
    for c in range(3):
        cols = slice(c * ATT_WIDTH, (c + 1) * ATT_WIDTH)
        res = _dot(hb, wqkv_ref[:, cols])
        if c == 0:
            res = res * (ATT_HEAD_DIM ** -0.5)
        qkv1_ref[c, 0] = res.astype(BF16)
        for j in range(ATT_WIDTH // LANES):
            lanes = slice(j * LANES, (j + 1) * LANES)
            stage_ref[j] = res[:, lanes]
            for r in range(4):
                qkv4_ref[c, 0, r, :, lanes] = stage_ref[j, pl.ds(r, tm // 4, stride=4), :].astype(BF16)
            for r in range(16):
                qkv16_ref[c, 0, r, :, lanes] = stage_ref[j, pl.ds(r, tm // 16, stride=16), :].astype(BF16)


def _inproj(x, g, b, wz, wxbc, wdt, wqkv, tm=256):
    bsz, s, d = x.shape
    nt = s // tm
    const = lambda *_: (0, 0)
    wspec = lambda w: pl.BlockSpec(w.shape, const)
    row = lambda n: pl.BlockSpec((1, tm, n), lambda bi, i: (bi, i, 0))
    out_shape = (
        jax.ShapeDtypeStruct((bsz, s, d), F32),
        jax.ShapeDtypeStruct((bsz, s, SSD_WIDTH), BF16),
        jax.ShapeDtypeStruct((bsz, s, CONV_CH), BF16),
        jax.ShapeDtypeStruct((bsz, s, SSD_GROUPS * LANES), F32),
        jax.ShapeDtypeStruct((3, bsz, s, ATT_WIDTH), BF16),
        jax.ShapeDtypeStruct((3, bsz, 4, s // 4, ATT_WIDTH), BF16),
        jax.ShapeDtypeStruct((3, bsz, 16, s // 16, ATT_WIDTH), BF16),
    )
    out_specs = (
        row(d), row(SSD_WIDTH), row(CONV_CH), row(SSD_GROUPS * LANES),
        pl.BlockSpec((3, 1, tm, ATT_WIDTH), lambda bi, i: (0, bi, i, 0)),
        pl.BlockSpec((3, 1, 4, tm // 4, ATT_WIDTH), lambda bi, i: (0, bi, 0, i, 0)),
        pl.BlockSpec((3, 1, 16, tm // 16, ATT_WIDTH), lambda bi, i: (0, bi, 0, i, 0)),
    )
    return pl.pallas_call(
        _inproj_kernel,
        grid=(bsz, nt),
        in_specs=[row(d), wspec(g), wspec(b), wspec(wz), wspec(wxbc), wspec(wdt), wspec(wqkv)],
        out_specs=out_specs,
        out_shape=out_shape,
        scratch_shapes=[pltpu.VMEM((ATT_WIDTH // LANES, tm, LANES), F32)],
        compiler_params=_cparams("parallel", "parallel"),
        name="ln_inproj",
    )(x, g, b, wz, wxbc, wdt, wqkv)


GROUP_X = SSD_HEADS_PER_GROUP * SSD_HEAD_DIM
GROUP_CH = GROUP_X + 2 * SSD_STATE
CONV_ROWS = 256
PAD_ROWS = 8

def _softplus(x):
    return jnp.maximum(x, 0.0) + jnp.log(1.0 + jnp.exp(-jnp.abs(x)))


def _ssd_constants():
    q = SSD_CHUNK
    r = SSD_HEADS_PER_GROUP
    i = np.arange(q)
    tris = [(i[:, None] >= i[None, :]), (i[:, None] <= i[None, :])]
    tri_cat = np.stack([np.concatenate([t, t], axis=1) for t in tris])
    trit_cat = np.stack([np.concatenate([t.T, t.T], axis=0) for t in tris])

    src = np.arange(2 * LANES)[:, None] % LANES
    head = np.arange(GROUP_X)[None, :] // SSD_HEAD_DIM
    sel = np.stack([src == head + off for off in (0, r)])
    as_bf16 = lambda m: jnp.asarray(m.astype(np.float32), dtype=BF16)
    return as_bf16(tri_cat), as_bf16(trit_cat), as_bf16(sel)


def _ssd_kernel(x_ref, b_ref, c_ref, dt_ref, cw_ref, cb_ref, dtb_ref, a_ref, ax_ref, dsk_ref,
                tric_ref, tritc_ref, selx_ref, y_ref, pad_ref, act_ref, yacc_ref, st_ref):
    s = x_ref.shape[1]
    q = SSD_CHUNK
    nchunks = s // q

    zeros = jnp.zeros((PAD_ROWS, GROUP_CH), F32)
    pad_ref[0:PAD_ROWS, :] = zeros
    pad_ref[s + PAD_ROWS:s + 2 * PAD_ROWS, :] = zeros
    pad_ref[PAD_ROWS:s + PAD_ROWS, 0:GROUP_X] = x_ref[0].astype(F32)
    pad_ref[PAD_ROWS:s + PAD_ROWS, GROUP_X:GROUP_X + SSD_STATE] = b_ref[0].astype(F32)
    pad_ref[PAD_ROWS:s + PAD_ROWS, GROUP_X + SSD_STATE:GROUP_CH] = c_ref[0].astype(F32)
    for i in range(s // CONV_ROWS):
        acc = jnp.broadcast_to(cb_ref[0], (CONV_ROWS, GROUP_CH))
        for k in range(SSD_CONV):
            r0 = PAD_ROWS + i * CONV_ROWS + k - SSD_CONV // 2
            acc = acc + cw_ref[0, k:k + 1, :] * pad_ref[r0:r0 + CONV_ROWS, :]
        act_ref[i * CONV_ROWS:(i + 1) * CONV_ROWS, :] = _silu(acc)

    row_i = lax.broadcasted_iota(I32, (q, q), 0)
    col_i = lax.broadcasted_iota(I32, (q, q), 1)
    lane = lax.broadcasted_iota(I32, (q, GROUP_X), 1)
    low_half = lax.broadcasted_iota(I32, (q, LANES), 1) < SSD_HEAD_DIM

    def split(v):
        hi = v.astype(BF16)
        return hi, (v - hi.astype(F32)).astype(BF16)

    st_ref[...] = jnp.zeros_like(st_ref)

    def chunk(ci, reverse):
        di = 1 if reverse else 0
        off = SSD_HEADS_PER_GROUP if reverse else 0
        tri = (row_i <= col_i) if reverse else (row_i >= col_i)
        k = dict(tri_cat=tric_ref[di], trit_cat=tritc_ref[di], sel_x=selx_ref[di])
        c = (nchunks - 1 - ci) if reverse else ci
        rows = pl.ds(c * q, q)
        xa = act_ref[rows, 0:GROUP_X]
        bm = act_ref[rows, GROUP_X:GROUP_X + SSD_STATE]
        cm = act_ref[rows, GROUP_X + SSD_STATE:GROUP_CH]
        dt = _softplus(dt_ref[0, rows, :] + dtb_ref[0])
        dcat = jnp.concatenate(split(dt), axis=1)
        dt_x = _dot(dcat, k["sel_x"])
        cs_x = _dot(k["tri_cat"], jnp.concatenate(split(dt_x * ax_ref[0, di]), axis=0))
        cs_w = []
        for pair in range(GROUP_X // LANES):
            blk = cs_x[:, pair * LANES:(pair + 1) * LANES]
            swapped = pltpu.roll(blk, SSD_HEAD_DIM, axis=1)
            cs_w += [jnp.where(low_half, blk, swapped), jnp.where(low_half, swapped, blk)]
        dta_t = (dt * a_ref[0]).T[off:off + 8, :]
        cs_t = _dot(jnp.concatenate(split(dta_t), axis=1), k["trit_cat"])
        tot = cs_x[0:1, :] if reverse else cs_x[q - 1:q, :]
        cm_b = cm.astype(BF16)
        cbm = _dot_nt(cm_b, bm.astype(BF16))
        xdt = xa * dt_x
        xdt_b = xdt.astype(BF16)
        ydiag = None
        for r in range(SSD_HEADS_PER_GROUP):
            seg = cs_w[r] - cs_t[r:r + 1, :]
            w = (jnp.exp(jnp.where(tri, seg, NEG_INF)) * cbm).astype(BF16)
            yr = _dot(w, xdt_b)
            ydiag = yr if r == 0 else jnp.where(lane >= r * SSD_HEAD_DIM, yr, ydiag)
        xs_b = (xdt * jnp.exp(tot - cs_x)).astype(BF16)
        states = _dot(bm.T.astype(BF16), xs_b)
        hprev = st_ref[di]
        yoff = _dot(cm_b, hprev.astype(BF16)) * jnp.exp(cs_x)
        st_ref[di] = hprev * jnp.exp(tot) + states
        yacc_ref[di, rows, :] = ydiag + yoff

    for ci in range(nchunks):
        chunk(ci, False)
        chunk(ci, True)
    y_ref[0] = (yacc_ref[0] + yacc_ref[1] + act_ref[:, 0:GROUP_X] * dsk_ref[0]).astype(BF16)


def _ssd(xbc, dt, conv_w_g, conv_b_g, dt_bias_g, a_g, dskip_g):
    bsz, s, _ = xbc.shape
    a_lanes, a_x = a_g
    grp = lambda n: pl.BlockSpec((1,) + n, lambda bi, gi: (gi,) + (0,) * len(n))
    in_specs = [
        pl.BlockSpec((1, s, GROUP_X), lambda bi, gi: (bi, 0, gi)),
        pl.BlockSpec((1, s, SSD_STATE), lambda bi, gi: (bi, 0, SSD_WIDTH // SSD_STATE + gi)),
        pl.BlockSpec((1, s, SSD_STATE), lambda bi, gi: (bi, 0, (SSD_WIDTH + GN) // SSD_STATE + gi)),
        pl.BlockSpec((1, s, LANES), lambda bi, gi: (bi, 0, gi)),
        grp((SSD_CONV, GROUP_CH)), grp((1, GROUP_CH)), grp((1, LANES)), grp((1, LANES)),
        grp((2, 1, GROUP_X)), grp((1, GROUP_X)),
    ]
    consts = _ssd_constants()
    in_specs += [pl.BlockSpec(c.shape, lambda bi, gi: (0, 0, 0)) for c in consts]
    return pl.pallas_call(
        _ssd_kernel,
        grid=(bsz, SSD_GROUPS),
        in_specs=in_specs,
        out_specs=pl.BlockSpec((1, s, GROUP_X), lambda bi, gi: (bi, 0, gi)),
        out_shape=jax.ShapeDtypeStruct((bsz, s, SSD_WIDTH), BF16),
        scratch_shapes=[
            pltpu.VMEM((s + 2 * PAD_ROWS, GROUP_CH), F32),
            pltpu.VMEM((s, GROUP_CH), F32),
            pltpu.VMEM((2, s, GROUP_X), F32),
            pltpu.VMEM((2, SSD_STATE, GROUP_X), F32),
        ],
        compiler_params=_cparams("parallel", "parallel"),
        name="ssd_scan",
    )(xbc, xbc, xbc, dt, conv_w_g, conv_b_g, dt_bias_g, a_lanes, a_x, dskip_g, *consts)


ATT_QBLOCK = 128
ATT_PAIR = 2 * ATT_HEAD_DIM
ATT_WINDOW = 2 * ATT_QBLOCK


def _att_kernel(q1, k1, v1, q4, k4, v4, q16, k16, v16, slope_ref, o_ref, km_ref, vm_ref, bias_ref, ob_ref, lb_ref):
    s = o_ref.shape[1]
    qb = ATT_QBLOCK
    first = lax.broadcasted_iota(I32, (qb, ATT_PAIR), 1) < ATT_HEAD_DIM
    first_all = lax.broadcasted_iota(I32, (s, ATT_PAIR), 1) < ATT_HEAD_DIM
    slope_lanes = slope_ref[0]
    ind = first_all.astype(F32).astype(BF16)

    def branch(bi, dil, q_ref, k_ref, v_ref):
        seg_len = s // dil
        win = min(ATT_WINDOW, seg_len)
        kk = k_ref[0, 0]
        vv = v_ref[0, 0]
        zero = jnp.zeros_like(kk)
        km_ref[0] = jnp.where(first_all, kk, zero)
        km_ref[1] = jnp.where(first_all, zero, kk)
        vm_ref[0] = jnp.concatenate([jnp.where(first_all, vv, zero), ind], axis=1)
        vm_ref[1] = jnp.concatenate([jnp.where(first_all, zero, vv), 1 - ind], axis=1)
        shifts = sorted({min(max(w0 - ATT_RADIUS, 0), seg_len - win) - w0 for w0 in range(0, seg_len, qb)},
                        reverse=True)
        rel0 = lax.broadcasted_iota(I32, (qb, win), 1) - lax.broadcasted_iota(I32, (qb, win), 0)
        for vi, shift in enumerate(shifts):
            rel = jnp.abs(rel0 + shift)
            dist = (dil * rel).astype(F32)
            for j in range(2):
                slope = slope_lanes[:, j * ATT_HEAD_DIM:j * ATT_HEAD_DIM + 1]
                bias_ref[vi, :, j * win:(j + 1) * win] = jnp.where(rel <= ATT_RADIUS, -slope * dist, NEG_INF)

        for blk in range(s // qb):
            qs = blk * qb
            seg, within = divmod(qs, seg_len)
            kw = min(max(within - ATT_RADIUS, 0), seg_len - win)
            ks = seg * seg_len + kw
            vi = shifts.index(kw - within)
            kcat = jnp.concatenate([km_ref[0, ks:ks + win, :], km_ref[1, ks:ks + win, :]], axis=0)
            vcat = jnp.concatenate([vm_ref[0, ks:ks + win, :], vm_ref[1, ks:ks + win, :]], axis=0)
            sc = _dot_nt(q_ref[0, 0, qs:qs + qb, :], kcat) + bias_ref[vi, :, 0:2 * win]
            s0, s1 = sc[:, :win], sc[:, win:]
            m0 = jnp.max(s0, axis=-1, keepdims=True)
            m1 = jnp.max(s1, axis=-1, keepdims=True)
            p0 = jnp.exp(s0 - m0)
            p1 = jnp.exp(s1 - m1)
            ol = _dot(jnp.concatenate([p0, p1], axis=1).astype(BF16), vcat)
            l = ol[:, ATT_PAIR:]
            o_blk = ol[:, :ATT_PAIR] / l
            l_blk = jnp.where(first, m0, m1) + jnp.log(l)
            if dil == 1:
                dst = pl.ds(qs, qb)
            else:
                dst = pl.ds(within * dil + seg, qb, stride=dil)
            ob_ref[bi, dst, :] = o_blk
            lb_ref[bi, dst, :] = l_blk

    branch(0, DILATIONS[0], q1, k1, v1)
    branch(1, DILATIONS[1], q4, k4, v4)
    branch(2, DILATIONS[2], q16, k16, v16)

    rows = 256
    for i in range(s // rows):
        sl = slice(i * rows, (i + 1) * rows)
        l0, l1, l2 = lb_ref[0, sl, :], lb_ref[1, sl, :], lb_ref[2, sl, :]
        m = jnp.maximum(jnp.maximum(l0, l1), l2)
        w0, w1, w2 = jnp.exp(l0 - m), jnp.exp(l1 - m), jnp.exp(l2 - m)
        num = w0 * ob_ref[0, sl, :] + w1 * ob_ref[1, sl, :] + w2 * ob_ref[2, sl, :]
        o_ref[0, sl, :] = (num / (w0 + w1 + w2)).astype(BF16)


def _dilated_attention(qkv1, qkv4, qkv16, slopes):
    _, bsz, s, _ = qkv1.shape
    npairs = ATT_WIDTH // ATT_PAIR
    spec = lambda c: pl.BlockSpec((1, 1, s, ATT_PAIR), lambda bi, hp, c=c: (c, bi, 0, hp))
    in_specs = [spec(0), spec(1), spec(2)] * 3 + [pl.BlockSpec((1, 1, ATT_PAIR), lambda bi, hp: (hp, 0, 0))]
    return pl.pallas_call(
        _att_kernel,
        grid=(bsz, npairs),
        in_specs=in_specs,
        out_specs=pl.BlockSpec((1, s, ATT_PAIR), lambda bi, hp: (bi, 0, hp)),
        out_shape=jax.ShapeDtypeStruct((bsz, s, ATT_WIDTH), BF16),
        scratch_shapes=[pltpu.VMEM((2, s, ATT_PAIR), BF16), pltpu.VMEM((2, s, 2 * ATT_PAIR), BF16),
                        pltpu.VMEM((3, ATT_QBLOCK, 2 * ATT_WINDOW), F32),
                        pltpu.VMEM((3, s, ATT_PAIR), F32), pltpu.VMEM((3, s, ATT_PAIR), F32)],
        compiler_params=_cparams("parallel", "parallel"),
        name="dilated_attention",
    )(qkv1, qkv1, qkv1, qkv4, qkv4, qkv4, qkv16, qkv16, qkv16, slopes)


def _outproj_kernel(h_ref, yssd_ref, z_ref, yatt_ref, nw_ref, wo_ref, g_ref, b_ref, o_ref):
    y = yssd_ref[...].astype(F32) * _silu(z_ref[...].astype(F32))
    yn = y * lax.rsqrt(jnp.mean(y * y, -1, keepdims=True) + RMS_EPS) * nw_ref[...]
    mix = _dot(yn.astype(BF16), wo_ref[0:SSD_WIDTH, :]) + _dot(yatt_ref[...], wo_ref[SSD_WIDTH:, :])
    o_ref[...] = _ln_rows(DN_ALPHA * h_ref[...] + mix, g_ref[...], b_ref[...])


def _outproj(h, yssd, z, yatt, norm_w, w_out, g, b, tm=512):
    t, d = h.shape
    const = lambda *_: (0, 0)
    wspec = lambda w: pl.BlockSpec(w.shape, const)
    row = lambda n: pl.BlockSpec((tm, n), lambda i: (i, 0))
    return pl.pallas_call(
        _outproj_kernel,
        grid=(t // tm,),
        in_specs=[row(d), row(SSD_WIDTH), row(SSD_WIDTH), row(ATT_WIDTH),
                  wspec(norm_w), wspec(w_out), wspec(g), wspec(b)],
        out_specs=row(d),
        out_shape=jax.ShapeDtypeStruct((t, d), F32),
        compiler_params=_cparams("parallel"),
        name="outproj_ln",
    )(h, yssd, z, yatt, norm_w, w_out, g, b)


def _alibi_slopes():
    sl = np.array([2.0 ** (-8.0 * (h + 1) / ATT_HEADS) for h in range(ATT_HEADS)], dtype=np.float32)
    return jnp.asarray(np.repeat(sl, ATT_HEAD_DIM).reshape(ATT_WIDTH // ATT_PAIR, 1, ATT_PAIR))


def _mixer_params(w_in, conv_w, conv_b, dt_bias_f, dt_bias_b, a_log_f, a_log_b, d_skip):
    r = SSD_HEADS_PER_GROUP
    wz = w_in[:, 0:XBC_OFF].astype(BF16)
    wxbc = w_in[:, XBC_OFF:DT_OFF].astype(BF16)
    wqkv = w_in[:, Q_OFF:IN_COLS].astype(BF16)
    wdt_f = w_in[:, DT_OFF:DT_OFF + SSD_HEADS].reshape(D_MODEL, SSD_GROUPS, r)
    wdt_b = w_in[:, DT_OFF + SSD_HEADS:Q_OFF].reshape(D_MODEL, SSD_GROUPS, r)
    lane_pad = ((0, 0), (0, 0), (0, LANES - 2 * r))
    wdt = jnp.pad(jnp.concatenate([wdt_f, wdt_b], -1), lane_pad).reshape(D_MODEL, SSD_GROUPS * LANES).astype(BF16)

    def per_group_lanes(f, b):
        v = jnp.concatenate([f.reshape(SSD_GROUPS, 1, r), b.reshape(SSD_GROUPS, 1, r)], -1)
        return jnp.pad(v.astype(F32), lane_pad)

    dtb = per_group_lanes(dt_bias_f, dt_bias_b)
    a_f = -jnp.exp(a_log_f.astype(F32))
    a_b = -jnp.exp(a_log_b.astype(F32))

    def per_head_lanes(n):
        v = jnp.stack([a_f.reshape(SSD_GROUPS, r), a_b.reshape(SSD_GROUPS, r)], axis=1)
        return jnp.repeat(v, n, axis=-1).reshape(SSD_GROUPS, 2, 1, r * n)

    a = (per_group_lanes(a_f, a_b), per_head_lanes(SSD_HEAD_DIM))

    def group_cols(w):
        xs = w[..., 0:SSD_WIDTH].reshape(w.shape[:-1] + (SSD_GROUPS, GROUP_X))
        bs = w[..., SSD_WIDTH:SSD_WIDTH + GN].reshape(w.shape[:-1] + (SSD_GROUPS, SSD_STATE))
        cs = w[..., SSD_WIDTH + GN:].reshape(w.shape[:-1] + (SSD_GROUPS, SSD_STATE))
        return jnp.moveaxis(jnp.concatenate([xs, bs, cs], -1), -2, 0)

    cw = group_cols(conv_w.astype(F32))
    cb = group_cols(conv_b.astype(F32)[None])
    dsk = jnp.repeat(d_skip.astype(F32), SSD_HEAD_DIM).reshape(SSD_GROUPS, 1, GROUP_X)
    return wz, wxbc, wdt, wqkv, cw, cb, dtb, a, dsk


def _mixer_layer(x, ln_in_g, ln_in_b, w_in, conv_w, conv_b, dt_bias_f, dt_bias_b, a_log_f, a_log_b,
                 d_skip, ssd_norm_w, w_out_mix, ln1_g, ln1_b):
    bsz, s, d = x.shape
    vec = lambda v: v.astype(F32).reshape(1, -1)
    wz, wxbc, wdt, wqkv, cw, cb, dtb, a, dsk = _mixer_params(
        w_in, conv_w, conv_b, dt_bias_f, dt_bias_b, a_log_f, a_log_b, d_skip)
    h, z, xbc, dt, qkv1, qkv4, qkv16 = _inproj(x, vec(ln_in_g), vec(ln_in_b), wz, wxbc, wdt, wqkv)
    yssd = _ssd(xbc, dt, cw, cb, dtb, a, dsk)
    yatt = _dilated_attention(qkv1, qkv4.reshape(qkv1.shape), qkv16.reshape(qkv1.shape), _alibi_slopes())
    t = bsz * s
    return _outproj(h.reshape(t, d), yssd.reshape(t, SSD_WIDTH), z.reshape(t, SSD_WIDTH),
                    yatt.reshape(t, ATT_WIDTH), vec(ssd_norm_w), w_out_mix.astype(BF16), vec(ln1_g), vec(ln1_b))


def _memkv_kernel(mem_ref, wk_ref, wv_ref, k_ref, v_ref):
    m = mem_ref[0].astype(BF16)
    k_ref[0] = _dot(m, wk_ref[...]).astype(BF16)
    v_ref[0] = _dot(m, wv_ref[...]).astype(BF16)


def _memkv(mem, wk, wv):
    bsz, m, d = mem.shape
    const = lambda *_: (0, 0)
    blk = pl.BlockSpec((1, m, d), lambda bi: (bi, 0, 0))
    return pl.pallas_call(
        _memkv_kernel,
        grid=(bsz,),
        in_specs=[blk, pl.BlockSpec(wk.shape, const), pl.BlockSpec(wv.shape, const)],
        out_specs=(blk, blk),
        out_shape=(jax.ShapeDtypeStruct((bsz, m, d), BF16),) * 2,
        compiler_params=_cparams("parallel"),
        name="mem_kv",
    )(mem, wk, wv)


def _memattn_kernel(h_ref, k_ref, v_ref, wq_ref, wo_ref, g_ref, b_ref, o_ref, ow_ref):
    h = h_ref[0]
    q = (_dot(h.astype(BF16), wq_ref[...]) * (MEM_HEAD_DIM ** -0.5)).astype(BF16)
    xa = None
    for hd in range(MEM_HEADS):
        cols = slice(hd * MEM_HEAD_DIM, (hd + 1) * MEM_HEAD_DIM)
        sc = _dot_nt(q[:, cols], k_ref[0, :, cols])
        p = jnp.exp(sc - jnp.max(sc, axis=-1, keepdims=True))
        o = _dot(p.astype(BF16), v_ref[0, :, cols]) / jnp.sum(p, axis=-1, keepdims=True)
        part = _dot(o.astype(BF16), wo_ref[cols, :])
        xa = part if xa is None else xa + part
    out = _ln_rows(DN_ALPHA * h + xa, g_ref[...], b_ref[...])
    o_ref[0] = out
    ow_ref[0] = _pack_bf16_pairs(out)


def _memattn(h, k, v, wq, wo, g, b, tm=512):
    bsz, s, d = h.shape
    m = k.shape[1]
    const = lambda *_: (0, 0)
    wspec = lambda w: pl.BlockSpec(w.shape, const)
    row = pl.BlockSpec((1, tm, d), lambda bi, i: (bi, i, 0))
    kv = pl.BlockSpec((1, m, d), lambda bi, i: (bi, 0, 0))
    return pl.pallas_call(
        _memattn_kernel,
        grid=(bsz, s // tm),
        in_specs=[row, kv, kv, wspec(wq), wspec(wo), wspec(g), wspec(b)],
        out_specs=(row, pl.BlockSpec((1, tm, d // 2), lambda bi, i: (bi, i, 0))),
        out_shape=(jax.ShapeDtypeStruct((bsz, s, d), F32), jax.ShapeDtypeStruct((bsz, s, d // 2), I32)),
        compiler_params=_cparams("parallel", "parallel"),
        name="mem_attn_ln",
    )(h, k, v, wq, wo, g, b)


def _first_max(work, idx_iota, sentinel):
    m = jnp.max(work, axis=0, keepdims=True)
    idx = jnp.min(jnp.where(work == m, idx_iota, sentinel), axis=0, keepdims=True)
    return m, idx


def _router_kernel(h_ref, wrt_ref, bias_ref, eidx_ref, gate_ref, rank_ref, cnt_ref, run_ref):
    tm = h_ref.shape[0]
    e = N_EXPERTS

    @pl.when(pl.program_id(0) == 0)
    def _():
        run_ref[...] = jnp.zeros_like(run_ref)

    logits = lax.dot_general(wrt_ref[...], h_ref[...], (((1,), (1,)), ((), ())),
                             precision=lax.Precision.HIGHEST, preferred_element_type=F32)
    scores = jax.nn.sigmoid(logits)
    choice = scores + bias_ref[...]
    row = lax.broadcasted_iota(I32, (e, tm), 0)
    grow = lax.broadcasted_iota(I32, (GROUP_SIZE, tm), 0)
    gs = []
    for g in range(ROUTE_GROUPS):
        blk = choice[g * GROUP_SIZE:(g + 1) * GROUP_SIZE, :]
        m1, i1 = _first_max(blk, grow, GROUP_SIZE)
        m2 = jnp.max(jnp.where(grow == i1, -jnp.inf, blk), axis=0, keepdims=True)
        gs.append(m1 + m2)
    work = jnp.concatenate(gs, axis=0)
    giota = lax.broadcasted_iota(I32, (ROUTE_GROUPS, tm), 0)
    gmask = jnp.zeros((ROUTE_GROUPS, tm), jnp.bool_)
    for _ in range(TOPK_GROUPS):
        _, gi = _first_max(work, giota, ROUTE_GROUPS)
        hit = giota == gi
        gmask = jnp.logical_or(gmask, hit)
        work = jnp.where(hit, -jnp.inf, work)
    work = jnp.concatenate(
        [jnp.where(gmask[g:g + 1, :], choice[g * GROUP_SIZE:(g + 1) * GROUP_SIZE, :], NEG_INF)
         for g in range(ROUTE_GROUPS)], axis=0)
    chosen = jnp.zeros((e, tm), jnp.bool_)
    idxs, svals = [], []
    for _ in range(TOP_K):
        _, ei = _first_max(work, row, e)
        hit = row == ei
        svals.append(jnp.sum(jnp.where(hit, scores, 0.0), axis=0, keepdims=True))
        idxs.append(ei)
        chosen = jnp.logical_or(chosen, hit)
        work = jnp.where(hit, -jnp.inf, work)
    eidx = jnp.concatenate(idxs, axis=0)
    sv = jnp.concatenate(svals, axis=0)
    eidx_ref[...] = eidx
    gate_ref[...] = sv / jnp.sum(sv, axis=0, keepdims=True) * ROUTED_SCALE
    upper = (lax.broadcasted_iota(I32, (tm, tm), 0) < lax.broadcasted_iota(I32, (tm, tm), 1))
    mt = chosen.astype(F32)
    cum = _dot(mt.astype(BF16), upper.astype(F32).astype(BF16)) + run_ref[...]
    rank_ref[...] = jnp.concatenate(
        [jnp.sum(jnp.where(row == idxs[k], cum, 0.0), axis=0, keepdims=True) for k in range(TOP_K)],
        axis=0).astype(I32)
    run_ref[...] += jnp.sum(mt, axis=1, keepdims=True)
    cnt_ref[...] = run_ref[...]


def _router(h, w_router_t, bias_col, tm=512):
    t, d = h.shape
    const = lambda *_: (0, 0)
    col = lambda dt: pl.BlockSpec((TOP_K, tm), lambda i: (0, i))
    return pl.pallas_call(
        _router_kernel,
        grid=(t // tm,),
        in_specs=[pl.BlockSpec((tm, d), lambda i: (i, 0)), pl.BlockSpec(w_router_t.shape, const),
                  pl.BlockSpec(bias_col.shape, const)],
        out_specs=(col(I32), col(F32), col(I32), pl.BlockSpec((N_EXPERTS, 1), const)),
        out_shape=(jax.ShapeDtypeStruct((TOP_K, t), I32), jax.ShapeDtypeStruct((TOP_K, t), F32),
                   jax.ShapeDtypeStruct((TOP_K, t), I32), jax.ShapeDtypeStruct((N_EXPERTS, 1), F32)),
        scratch_shapes=[pltpu.VMEM((N_EXPERTS, 1), F32)],
        compiler_params=_cparams("arbitrary"),
        name="router_topk",
    )(h, w_router_t, bias_col)


def _dispatch_kernel(cnt_ref, eidx_ref, rank_ref, dest_ref, bstart_ref, nb_ref):
    tm = eidx_ref.shape[1]
    e = N_EXPERTS
    nb = jnp.floor((cnt_ref[...] + (MOE_BLOCK - 1)) * (1.0 / MOE_BLOCK))
    strict = (lax.broadcasted_iota(I32, (e, e), 1) < lax.broadcasted_iota(I32, (e, e), 0))
    bstart = _dot(strict.astype(F32).astype(BF16), jnp.broadcast_to(nb, (e, LANES)).astype(BF16))[:, 0:1]
    pstart = bstart * MOE_BLOCK
    row = lax.broadcasted_iota(I32, (e, tm), 0)
    eidx = eidx_ref[...]
    rank = rank_ref[...]
    dest_ref[...] = jnp.concatenate(
        [jnp.sum(jnp.where(row == eidx[k:k + 1, :], pstart, 0.0), axis=0, keepdims=True).astype(I32)
         + rank[k:k + 1, :] for k in range(TOP_K)], axis=0)
    bstart_ref[...] = bstart.astype(I32)
    nb_ref[...] = nb.astype(I32)


def _dispatch(cnt, eidx, rank, tm=2048):
    t = eidx.shape[1]
    const = lambda *_: (0, 0)
    col = pl.BlockSpec((TOP_K, tm), lambda i: (0, i))
    per_expert = pl.BlockSpec((N_EXPERTS, 1), const)
    return pl.pallas_call(
        _dispatch_kernel,
        grid=(t // tm,),
        in_specs=[per_expert, col, col],
        out_specs=(col, per_expert, per_expert),
        out_shape=(jax.ShapeDtypeStruct((TOP_K, t), I32), jax.ShapeDtypeStruct((N_EXPERTS, 1), I32),
                   jax.ShapeDtypeStruct((N_EXPERTS, 1), I32)),
        compiler_params=_cparams("arbitrary"),
        name="dispatch_slots",
    )(cnt, eidx, rank)


EXPERT_ROW_BUFFERS = 4


def _experts_kernel(bstart_ref, nb_ref, xs_hbm, wg_ref, wu_ref, wd_ref, ys_hbm,
                    xbuf, ybuf, wg_b, wu_b, wd_b, in_sem, out_sem):
    e = pl.program_id(0)
    last_e = pl.num_programs(0) - 1
    first_block = bstart_ref[e]
    nused = bstart_ref[last_e] + nb_ref[last_e]
    nbuf = xbuf.shape[0]

    def block_rows(blk):
        return pl.ds(pl.multiple_of(blk * MOE_BLOCK, MOE_BLOCK), MOE_BLOCK)

    def in_copy(blk, slot):
        return pltpu.make_async_copy(xs_hbm.at[block_rows(blk)], xbuf.at[slot], in_sem.at[slot])

    def out_copy(blk, slot):
        return pltpu.make_async_copy(ybuf.at[slot], ys_hbm.at[block_rows(blk)], out_sem.at[slot])

    @pl.when(e == 0)
    def _():
        for ahead in range(nbuf - 1):
            @pl.when(ahead < nused)
            def _():
                in_copy(ahead, ahead).start()

    wg_b[...] = wg_ref[0].astype(BF16)
    wu_b[...] = wu_ref[0].astype(BF16)
    wd_b[...] = wd_ref[0].astype(BF16)

    def body(j, carry):
        blk = first_block + j
        slot = blk % nbuf
        in_copy(blk, slot).wait()
        nxt = blk + nbuf - 1

        @pl.when(nxt < nused)
        def _():
            in_copy(nxt, nxt % nbuf).start()

        @pl.when(blk >= nbuf)
        def _():
            out_copy(blk - nbuf, slot).wait()

        x = _unpack_bf16_pairs(xbuf[slot]).astype(BF16)
        g = _dot(x, wg_b[...])
        u = _dot(x, wu_b[...])
        a = (_silu(g) * u).astype(BF16)
        ybuf[slot] = _pack_bf16_pairs(_dot(a, wd_b[...]))
        out_copy(blk, slot).start()
        return carry

    lax.fori_loop(0, nb_ref[e], body, 0)

    @pl.when(e == last_e)
    def _():
        for back in range(1, nbuf + 1):
            @pl.when(nused >= back)
            def _():
                out_copy(nused - back, (nused - back) % nbuf).wait()


def _experts(bstart, nb, xs, w_gate, w_up, w_down):
    n_pad, words = xs.shape
    n_exp, d, f = w_gate.shape
    wspec = lambda w: pl.BlockSpec((1,) + w.shape[1:], lambda e, bs, nb: (e, 0, 0))
    hbm = pl.BlockSpec(memory_space=pl.ANY)
    return pl.pallas_call(
        _experts_kernel,
        grid_spec=pltpu.PrefetchScalarGridSpec(
            num_scalar_prefetch=2,
            grid=(n_exp,),
            in_specs=[hbm, wspec(w_gate), wspec(w_up), wspec(w_down)],
            out_specs=hbm,
            scratch_shapes=[
                pltpu.VMEM((EXPERT_ROW_BUFFERS, MOE_BLOCK, words), I32),
                pltpu.VMEM((EXPERT_ROW_BUFFERS, MOE_BLOCK, words), I32),
                pltpu.VMEM((d, f), BF16), pltpu.VMEM((d, f), BF16), pltpu.VMEM((f, d), BF16),
                pltpu.SemaphoreType.DMA((EXPERT_ROW_BUFFERS,)), pltpu.SemaphoreType.DMA((EXPERT_ROW_BUFFERS,)),
            ],
        ),
        out_shape=jax.ShapeDtypeStruct((n_pad, words), I32),
        compiler_params=_cparams("arbitrary"),
        name="routed_experts",
    )(bstart, nb, xs, w_gate, w_up, w_down)


def _final_kernel(h_ref, yg_ref, gate_ref, wsg_ref, wsu_ref, wsd_ref, g_ref, b_ref, o_ref):
    h = h_ref[...]
    d = h.shape[1]
    hb = h.astype(BF16)
    ff = _dot((_silu(_dot(hb, wsg_ref[...])) * _dot(hb, wsu_ref[...])).astype(BF16), wsd_ref[...])
    gate = gate_ref[...]
    for k in range(TOP_K):
        ff = ff + gate[:, k:k + 1] * _unpack_bf16_pairs(yg_ref[k])
    o_ref[...] = _ln_rows(DN_ALPHA * h + ff, g_ref[...], b_ref[...])


def _final(h, yg, gate, wsg, wsu, wsd, g, b, tm=256):
    t, d = h.shape
    const = lambda *_: (0, 0)
    wspec = lambda w: pl.BlockSpec(w.shape, const)
    row = lambda n: pl.BlockSpec((tm, n), lambda i: (i, 0))
    return pl.pallas_call(
        _final_kernel,
        grid=(t // tm,),
        in_specs=[row(d), pl.BlockSpec((TOP_K, tm, d // 2), lambda i: (0, i, 0)), row(TOP_K), wspec(wsg), wspec(wsu), wspec(wsd), wspec(g), wspec(b)],
        out_specs=row(d),
        out_shape=jax.ShapeDtypeStruct((t, d), F32),
        compiler_params=_cparams("parallel"),
        name="shared_combine_ln",
    )(h, yg, gate, wsg, wsu, wsd, g, b)


SC_CORES = 2
SC_SUBCORES = 16
SC_WORKERS = SC_CORES * SC_SUBCORES
SC_WINDOW = 64


def _sc_scatter_rows(x_words, dest_kmajor, n_rows):
    t, w = x_words.shape
    per_worker = t // SC_WORKERS
    nchunk = per_worker // SC_WINDOW
    mesh = plsc.VectorSubcoreMesh(core_axis_name="c", subcore_axis_name="s")

    @functools.partial(
        pl.kernel, mesh=mesh,
        out_type=jax.ShapeDtypeStruct((n_rows, w), I32),
        scratch_types=[pltpu.VMEM((SC_WINDOW,), I32), pltpu.VMEM((SC_WINDOW, w), I32), pltpu.SemaphoreType.DMA],
    )
    def scatter(x_hbm, dest_hbm, out_hbm, idx_v, rows_v, sem):
        wid = lax.axis_index("s") * SC_CORES + lax.axis_index("c")
        base = wid * per_worker

        @pl.loop(0, nchunk)
        def _(i):
            t0 = base + i * SC_WINDOW
            pltpu.sync_copy(x_hbm.at[pl.ds(t0, SC_WINDOW)], rows_v)
            for k in range(TOP_K):
                pltpu.sync_copy(dest_hbm.at[pl.ds(k * t + t0, SC_WINDOW)], idx_v)
                pltpu.async_copy(rows_v, out_hbm.at[idx_v], sem).wait()

    return scatter(x_words, dest_kmajor)


def _sc_gather_rows(table, idx):
    a = idx.shape[0]
    w = table.shape[1]
    per_worker = a // SC_WORKERS
    nchunk = per_worker // SC_WINDOW
    mesh = plsc.VectorSubcoreMesh(core_axis_name="c", subcore_axis_name="s")

    @functools.partial(
        pl.kernel, mesh=mesh,
        out_type=jax.ShapeDtypeStruct((a, w), I32),
        scratch_types=[pltpu.VMEM((SC_WINDOW,), I32), pltpu.VMEM((SC_WINDOW, w), I32), pltpu.SemaphoreType.DMA],
    )
    def gather(table_hbm, idx_hbm, out_hbm, idx_v, rows_v, sem):
        wid = lax.axis_index("s") * SC_CORES + lax.axis_index("c")
        base = wid * per_worker

        @pl.loop(0, nchunk)
        def _(i):
            off = base + i * SC_WINDOW
            pltpu.sync_copy(idx_hbm.at[pl.ds(off, SC_WINDOW)], idx_v)
            pltpu.async_copy(table_hbm.at[idx_v], rows_v, sem).wait()
            pltpu.sync_copy(rows_v, out_hbm.at[pl.ds(off, SC_WINDOW)])

    return gather(table, idx)


def _moe_layer(h2, h2_words, w_router, router_bias, w_gate, w_up, w_down, ws_gate, ws_up, ws_down, ln3_g, ln3_b):
    t, d = h2.shape
    vec = lambda v: v.astype(F32).reshape(1, -1)
    eidx, gate, rank, cnt = _router(h2, w_router.astype(F32).T, router_bias.astype(F32).reshape(-1, 1))
    n_blocks = t * TOP_K // MOE_BLOCK + N_EXPERTS
    dest, bstart, nb = _dispatch(cnt, eidx, rank)
    dest_flat = dest.reshape(TOP_K * t)
    xs = _sc_scatter_rows(h2_words, dest_flat, n_blocks * MOE_BLOCK)
    ys = _experts(bstart.reshape(N_EXPERTS), nb.reshape(N_EXPERTS), xs, w_gate, w_up, w_down)
    yg = _sc_gather_rows(ys, dest_flat).reshape(TOP_K, t, d // 2)
    return _final(h2, yg, gate.T, ws_gate.astype(BF16), ws_up.astype(BF16), ws_down.astype(BF16),
                  vec(ln3_g), vec(ln3_b))


def kernel(x, mem, ln_in_g, ln_in_b, w_in, conv_w, conv_b, dt_bias_f, dt_bias_b, a_log_f, a_log_b, d_skip, ssd_norm_w, w_out_mix, ln1_g, ln1_b, wq_mem, wk_mem, wv_mem, wo_mem, ln2_g, ln2_b, w_router, router_bias, w_gate, w_up, w_down, ws_gate, ws_up, ws_down, ln3_g, ln3_b):
    l = 0
    h1 = _mixer_layer(x, ln_in_g, ln_in_b, w_in[l], conv_w[l], conv_b[l], dt_bias_f[l], dt_bias_b[l],
                      a_log_f[l], a_log_b[l], d_skip[l], ssd_norm_w[l], w_out_mix[l], ln1_g[l], ln1_b[l])
    bsz, s, d = x.shape
    vec = lambda v: v.astype(F32).reshape(1, -1)
    km, vm = _memkv(mem, wk_mem[l].astype(BF16), wv_mem[l].astype(BF16))
    h2, h2_words = _memattn(h1.reshape(bsz, s, d), km, vm, wq_mem[l].astype(BF16), wo_mem[l].astype(BF16),
                            vec(ln2_g[l]), vec(ln2_b[l]))
    out = _moe_layer(h2.reshape(bsz * s, d), h2_words.reshape(bsz * s, d // 2), w_router[l], router_bias[l], w_gate[l], w_up[l], w_down[l],
                     ws_gate[l], ws_up[l], ws_down[l], ln3_g[l], ln3_b[l])
    return out.reshape(x.shape)
```

```python
import functools
import math

import numpy as np
import jax
import jax.numpy as jnp
from jax import lax
from jax.experimental import pallas as pl
from jax.experimental.pallas import tpu as pltpu
from jax.experimental.pallas import tpu_sc as plsc

F32 = jnp.float32
BF16 = jnp.bfloat16
I32 = jnp.int32

D_MODEL = 1024
SSD_WIDTH = 1024
SSD_HEAD_DIM = 64
SSD_HEADS = 16
SSD_STATE = 128
SSD_GROUPS = 4
SSD_HEADS_PER_GROUP = SSD_HEADS // SSD_GROUPS
SSD_CONV = 5
SSD_CHUNK = 128
ATT_WIDTH = 1024
ATT_HEAD_DIM = 64
ATT_HEADS = 16
ATT_RADIUS = 64
DILATIONS = (1, 4, 16)
MEM_HEADS = 4
MEM_HEAD_DIM = 256
N_EXPERTS = 256
ROUTE_GROUPS = 8
GROUP_SIZE = N_EXPERTS // ROUTE_GROUPS
TOPK_GROUPS = 4
TOP_K = 8
EXPERT_DIM = 256
ROUTED_SCALE = 2.5
MOE_BLOCK = 256
MOE_TOKEN_PARTS = 2
DN_ALPHA = 2.0 ** 0.25
LN_EPS = 1e-5
RMS_EPS = 1e-5
NEG_INF = -1e30
GN = SSD_GROUPS * SSD_STATE
CONV_CH = SSD_WIDTH + 2 * GN
XBC_OFF = SSD_WIDTH
DT_OFF = XBC_OFF + CONV_CH
Q_OFF = DT_OFF + 2 * SSD_HEADS
IN_COLS = Q_OFF + 3 * ATT_WIDTH

LANES = 128
VMEM_LIMIT_BYTES = 56 * 1024 * 1024


def _cparams(*sem):
    return pltpu.CompilerParams(dimension_semantics=sem, vmem_limit_bytes=VMEM_LIMIT_BYTES)


def _ln_rows(x, g, b):
    mu = jnp.mean(x, -1, keepdims=True)
    xc = x - mu
    var = jnp.mean(xc * xc, -1, keepdims=True)
    return xc * lax.rsqrt(var + LN_EPS) * g + b


def _dot(a, b):
    return jnp.dot(a, b, preferred_element_type=F32)


def _dot_nt(a, b):
    return lax.dot_general(a, b, (((1,), (1,)), ((), ())), preferred_element_type=F32)


def _silu(x):
    return x * jax.nn.sigmoid(x)


def _pack_bf16_pairs(x):
    n = x.shape[1] // 2
    u = lax.bitcast_convert_type(x.astype(BF16).astype(F32), jnp.uint32)
    word = (u[:, :n] >> 16) | (u[:, n:] & jnp.uint32(0xFFFF0000))
    return lax.bitcast_convert_type(word, I32)


def _unpack_bf16_pairs(w):
    u = lax.bitcast_convert_type(w, jnp.uint32)
    lo = lax.bitcast_convert_type(u << 16, F32)
    hi = lax.bitcast_convert_type(u & jnp.uint32(0xFFFF0000), F32)
    return jnp.concatenate([lo, hi], axis=1)


def _inproj_kernel(x_ref, g_ref, b_ref, wz_ref, wxbc_ref, wdt_ref, wqkv_ref,
                   h_ref, z_ref, xbc_ref, dt_ref, qkv1_ref, qkv4_ref, qkv16_ref, stage_ref):
    tm = x_ref.shape[1]
    h = _ln_rows(x_ref[0], g_ref[...], b_ref[...])
    h_ref[0] = h
    hb = h.astype(BF16)
    z_ref[0] = _dot(hb, wz_ref[...]).astype(BF16)
    for c in range(CONV_CH // D_MODEL):
        cols = slice(c * D_MODEL, (c + 1) * D_MODEL)
        xbc_ref[0, :, cols] = _dot(hb, wxbc_ref[:, cols]).astype(BF16)
    dt_ref[0] = _dot(hb, wdt_ref[...])
    for c in range(3):
        cols = slice(c * ATT_WIDTH, (c + 1) * ATT_WIDTH)
        res = _dot(hb, wqkv_ref[:, cols])
        if c == 0:
            res = res * (ATT_HEAD_DIM ** -0.5)
        qkv1_ref[c, 0] = res.astype(BF16)
        for j in range(ATT_WIDTH // LANES):
            lanes = slice(j * LANES, (j + 1) * LANES)
            stage_ref[j] = res[:, lanes]
            for r in range(4):
                qkv4_ref[c, 0, r, :, lanes] = stage_ref[j, pl.ds(r, tm // 4, stride=4), :].astype(BF16)
            for r in range(16):
                qkv16_ref[c, 0, r, :, lanes] = stage_ref[j, pl.ds(r, tm // 16, stride=16), :].astype(BF16)


def _inproj(x, g, b, wz, wxbc, wdt, wqkv, tm=256):
    bsz, s, d = x.shape
    nt = s // tm
    const = lambda *_: (0, 0)
    wspec = lambda w: pl.BlockSpec(w.shape, const)
    row = lambda n: pl.BlockSpec((1, tm, n), lambda bi, i: (bi, i, 0))
    out_shape = (
        jax.ShapeDtypeStruct((bsz, s, d), F32),
        jax.ShapeDtypeStruct((bsz, s, SSD_WIDTH), BF16),
        jax.ShapeDtypeStruct((bsz, s, CONV_CH), BF16),
        jax.ShapeDtypeStruct((bsz, s, SSD_GROUPS * LANES), F32),
        jax.ShapeDtypeStruct((3, bsz, s, ATT_WIDTH), BF16),
        jax.ShapeDtypeStruct((3, bsz, 4, s // 4, ATT_WIDTH), BF16),
        jax.ShapeDtypeStruct((3, bsz, 16, s // 16, ATT_WIDTH), BF16),
    )
    out_specs = (
        row(d), row(SSD_WIDTH), row(CONV_CH), row(SSD_GROUPS * LANES),
        pl.BlockSpec((3, 1, tm, ATT_WIDTH), lambda bi, i: (0, bi, i, 0)),
        pl.BlockSpec((3, 1, 4, tm // 4, ATT_WIDTH), lambda bi, i: (0, bi, 0, i, 0)),
        pl.BlockSpec((3, 1, 16, tm // 16, ATT_WIDTH), lambda bi, i: (0, bi, 0, i, 0)),
    )
    return pl.pallas_call(
        _inproj_kernel,
        grid=(bsz, nt),
        in_specs=[row(d), wspec(g), wspec(b), wspec(wz), wspec(wxbc), wspec(wdt), wspec(wqkv)],
        out_specs=out_specs,
        out_shape=out_shape,
        scratch_shapes=[pltpu.VMEM((ATT_WIDTH // LANES, tm, LANES), F32)],
        compiler_params=_cparams("parallel", "parallel"),
        name="ln_inproj",
    )(x, g, b, wz, wxbc, wdt, wqkv)


GROUP_X = SSD_HEADS_PER_GROUP * SSD_HEAD_DIM
GROUP_CH = GROUP_X + 2 * SSD_STATE
CONV_ROWS = 256
PAD_ROWS = 8

def _softplus(x):
    return jnp.maximum(x, 0.0) + jnp.log(1.0 + jnp.exp(-jnp.abs(x)))


def _ssd_constants():
    q = SSD_CHUNK
    r = SSD_HEADS_PER_GROUP
    i = np.arange(q)
    tris = [(i[:, None] >= i[None, :]), (i[:, None] <= i[None, :])]
    tri_cat = np.stack([np.concatenate([t, t], axis=1) for t in tris])
    trit_cat = np.stack([np.concatenate([t.T, t.T], axis=0) for t in tris])

    src = np.arange(2 * LANES)[:, None] % LANES
    head = np.arange(GROUP_X)[None, :] // SSD_HEAD_DIM
    sel = np.stack([src == head + off for off in (0, r)])
    as_bf16 = lambda m: jnp.asarray(m.astype(np.float32), dtype=BF16)
    return as_bf16(tri_cat), as_bf16(trit_cat), as_bf16(sel)


def _ssd_kernel(x_ref, b_ref, c_ref, dt_ref, cw_ref, cb_ref, dtb_ref, a_ref, ax_ref, dsk_ref,
                tric_ref, tritc_ref, selx_ref, y_ref, pad_ref, act_ref, yacc_ref, st_ref):
    s = x_ref.shape[1]
    q = SSD_CHUNK
    nchunks = s // q

    zeros = jnp.zeros((PAD_ROWS, GROUP_CH), F32)
    pad_ref[0:PAD_ROWS, :] = zeros
    pad_ref[s + PAD_ROWS:s + 2 * PAD_ROWS, :] = zeros
    pad_ref[PAD_ROWS:s + PAD_ROWS, 0:GROUP_X] = x_ref[0].astype(F32)
    pad_ref[PAD_ROWS:s + PAD_ROWS, GROUP_X:GROUP_X + SSD_STATE] = b_ref[0].astype(F32)
    pad_ref[PAD_ROWS:s + PAD_ROWS, GROUP_X + SSD_STATE:GROUP_CH] = c_ref[0].astype(F32)
    for i in range(s // CONV_ROWS):
        acc = jnp.broadcast_to(cb_ref[0], (CONV_ROWS, GROUP_CH))
        for k in range(SSD_CONV):
            r0 = PAD_ROWS + i * CONV_ROWS + k - SSD_CONV // 2
            acc = acc + cw_ref[0, k:k + 1, :] * pad_ref[r0:r0 + CONV_ROWS, :]
        act_ref[i * CONV_ROWS:(i + 1) * CONV_ROWS, :] = _silu(acc)

    row_i = lax.broadcasted_iota(I32, (q, q), 0)
    col_i = lax.broadcasted_iota(I32, (q, q), 1)
    lane = lax.broadcasted_iota(I32, (q, GROUP_X), 1)
    low_half = lax.broadcasted_iota(I32, (q, LANES), 1) < SSD_HEAD_DIM

    def split(v):
        hi = v.astype(BF16)
        return hi, (v - hi.astype(F32)).astype(BF16)

    st_ref[...] = jnp.zeros_like(st_ref)

    def chunk(ci, reverse):
        di = 1 if reverse else 0
        off = SSD_HEADS_PER_GROUP if reverse else 0
        tri = (row_i <= col_i) if reverse else (row_i >= col_i)
        k = dict(tri_cat=tric_ref[di], trit_cat=tritc_ref[di], sel_x=selx_ref[di])
        c = (nchunks - 1 - ci) if reverse else ci
        rows = pl.ds(c * q, q)
        xa = act_ref[rows, 0:GROUP_X]
        bm = act_ref[rows, GROUP_X:GROUP_X + SSD_STATE]
        cm = act_ref[rows, GROUP_X + SSD_STATE:GROUP_CH]
        dt = _softplus(dt_ref[0, rows, :] + dtb_ref[0])
        dcat = jnp.concatenate(split(dt), axis=1)
        dt_x = _dot(dcat, k["sel_x"])
        cs_x = _dot(k["tri_cat"], jnp.concatenate(split(dt_x * ax_ref[0, di]), axis=0))
        cs_w = []
        for pair in range(GROUP_X // LANES):
            blk = cs_x[:, pair * LANES:(pair + 1) * LANES]
            swapped = pltpu.roll(blk, SSD_HEAD_DIM, axis=1)
            cs_w += [jnp.where(low_half, blk, swapped), jnp.where(low_half, swapped, blk)]
        dta_t = (dt * a_ref[0]).T[off:off + 8, :]
        cs_t = _dot(jnp.concatenate(split(dta_t), axis=1), k["trit_cat"])
        tot = cs_x[0:1, :] if reverse else cs_x[q - 1:q, :]
        cm_b = cm.astype(BF16)
        cbm = _dot_nt(cm_b, bm.astype(BF16))
        xdt = xa * dt_x
        xdt_b = xdt.astype(BF16)
        ydiag = None
        for r in range(SSD_HEADS_PER_GROUP):
            seg = cs_w[r] - cs_t[r:r + 1, :]
            w = (jnp.exp(jnp.where(tri, seg, NEG_INF)) * cbm).astype(BF16)
            yr = _dot(w, xdt_b)
            ydiag = yr if r == 0 else jnp.where(lane >= r * SSD_HEAD_DIM, yr, ydiag)
        xs_b = (xdt * jnp.exp(tot - cs_x)).astype(BF16)
        states = _dot(bm.T.astype(BF16), xs_b)
        hprev = st_ref[di]
        yoff = _dot(cm_b, hprev.astype(BF16)) * jnp.exp(cs_x)
        st_ref[di] = hprev * jnp.exp(tot) + states
        yacc_ref[di, rows, :] = ydiag + yoff

    for ci in range(nchunks):
        chunk(ci, False)
        chunk(ci, True)
    y_ref[0] = (yacc_ref[0] + yacc_ref[1] + act_ref[:, 0:GROUP_X] * dsk_ref[0]).astype(BF16)


def _ssd(xbc, dt, conv_w_g, conv_b_g, dt_bias_g, a_g, dskip_g):
    bsz, s, _ = xbc.shape
    a_lanes, a_x = a_g
    grp = lambda n: pl.BlockSpec((1,) + n, lambda bi, gi: (gi,) + (0,) * len(n))
    in_specs = [
        pl.BlockSpec((1, s, GROUP_X), lambda bi, gi: (bi, 0, gi)),
        pl.BlockSpec((1, s, SSD_STATE), lambda bi, gi: (bi, 0, SSD_WIDTH // SSD_STATE + gi)),
        pl.BlockSpec((1, s, SSD_STATE), lambda bi, gi: (bi, 0, (SSD_WIDTH + GN) // SSD_STATE + gi)),
        pl.BlockSpec((1, s, LANES), lambda bi, gi: (bi, 0, gi)),
        grp((SSD_CONV, GROUP_CH)), grp((1, GROUP_CH)), grp((1, LANES)), grp((1, LANES)),
        grp((2, 1, GROUP_X)), grp((1, GROUP_X)),
    ]
    consts = _ssd_constants()
    in_specs += [pl.BlockSpec(c.shape, lambda bi, gi: (0, 0, 0)) for c in consts]
    return pl.pallas_call(
        _ssd_kernel,
        grid=(bsz, SSD_GROUPS),
        in_specs=in_specs,
        out_specs=pl.BlockSpec((1, s, GROUP_X), lambda bi, gi: (bi, 0, gi)),
        out_shape=jax.ShapeDtypeStruct((bsz, s, SSD_WIDTH), BF16),
        scratch_shapes=[
            pltpu.VMEM((s + 2 * PAD_ROWS, GROUP_CH), F32),
            pltpu.VMEM((s, GROUP_CH), F32),
            pltpu.VMEM((2, s, GROUP_X), F32),
            pltpu.VMEM((2, SSD_STATE, GROUP_X), F32),
        ],
        compiler_params=_cparams("parallel", "parallel"),
        name="ssd_scan",
    )(xbc, xbc, xbc, dt, conv_w_g, conv_b_g, dt_bias_g, a_lanes, a_x, dskip_g, *consts)


ATT_QBLOCK = 128
ATT_PAIR = 2 * ATT_HEAD_DIM
ATT_WINDOW = 2 * ATT_QBLOCK


def _att_kernel(q1, k1, v1, q4, k4, v4, q16, k16, v16, slope_ref, o_ref, km_ref, vm_ref, bias_ref, ob_ref, lb_ref):
    s = o_ref.shape[1]
    qb = ATT_QBLOCK
    first = lax.broadcasted_iota(I32, (qb, ATT_PAIR), 1) < ATT_HEAD_DIM
    first_all = lax.broadcasted_iota(I32, (s, ATT_PAIR), 1) < ATT_HEAD_DIM
    slope_lanes = slope_ref[0]
    ind = first_all.astype(F32).astype(BF16)

    def branch(bi, dil, q_ref, k_ref, v_ref):
        seg_len = s // dil
        win = min(ATT_WINDOW, seg_len)
        kk = k_ref[0, 0]
        vv = v_ref[0, 0]
        zero = jnp.zeros_like(kk)
        km_ref[0] = jnp.where(first_all, kk, zero)
        km_ref[1] = jnp.where(first_all, zero, kk)
        vm_ref[0] = jnp.concatenate([jnp.where(first_all, vv, zero), ind], axis=1)
        vm_ref[1] = jnp.concatenate([jnp.where(first_all, zero, vv), 1 - ind], axis=1)
        shifts = sorted({min(max(w0 - ATT_RADIUS, 0), seg_len - win) - w0 for w0 in range(0, seg_len, qb)},
                        reverse=True)
        rel0 = lax.broadcasted_iota(I32, (qb, win), 1) - lax.broadcasted_iota(I32, (qb, win), 0)
        for vi, shift in enumerate(shifts):
            rel = jnp.abs(rel0 + shift)
            dist = (dil * rel).astype(F32)
            for j in range(2):
                slope = slope_lanes[:, j * ATT_HEAD_DIM:j * ATT_HEAD_DIM + 1]
                bias_ref[vi, :, j * win:(j + 1) * win] = jnp.where(rel <= ATT_RADIUS, -slope * dist, NEG_INF)

        for blk in range(s // qb):
            qs = blk * qb
            seg, within = divmod(qs, seg_len)
            kw = min(max(within - ATT_RADIUS, 0), seg_len - win)
            ks = seg * seg_len + kw
            vi = shifts.index(kw - within)
            kcat = jnp.concatenate([km_ref[0, ks:ks + win, :], km_ref[1, ks:ks + win, :]], axis=0)
            vcat = jnp.concatenate([vm_ref[0, ks:ks + win, :], vm_ref[1, ks:ks + win, :]], axis=0)
            sc = _dot_nt(q_ref[0, 0, qs:qs + qb, :], kcat) + bias_ref[vi, :, 0:2 * win]
            s0, s1 = sc[:, :win], sc[:, win:]
            m0 = jnp.max(s0, axis=-1, keepdims=True)
            m1 = jnp.max(s1, axis=-1, keepdims=True)
            p0 = jnp.exp(s0 - m0)
            p1 = jnp.exp(s1 - m1)
            ol = _dot(jnp.concatenate([p0, p1], axis=1).astype(BF16), vcat)
            l = ol[:, ATT_PAIR:]
            o_blk = ol[:, :ATT_PAIR] / l
            l_blk = jnp.where(first, m0, m1) + jnp.log(l)
            if dil == 1:
                dst = pl.ds(qs, qb)
            else:
                dst = pl.ds(within * dil + seg, qb, stride=dil)
            ob_ref[bi, dst, :] = o_blk
            lb_ref[bi, dst, :] = l_blk

    branch(0, DILATIONS[0], q1, k1, v1)
    branch(1, DILATIONS[1], q4, k4, v4)
    branch(2, DILATIONS[2], q16, k16, v16)

    rows = 256
    for i in range(s // rows):
        sl = slice(i * rows, (i + 1) * rows)
        l0, l1, l2 = lb_ref[0, sl, :], lb_ref[1, sl, :], lb_ref[2, sl, :]
        m = jnp.maximum(jnp.maximum(l0, l1), l2)
        w0, w1, w2 = jnp.exp(l0 - m), jnp.exp(l1 - m), jnp.exp(l2 - m)
        num = w0 * ob_ref[0, sl, :] + w1 * ob_ref[1, sl, :] + w2 * ob_ref[2, sl, :]
        o_ref[0, sl, :] = (num / (w0 + w1 + w2)).astype(BF16)


def _dilated_attention(qkv1, qkv4, qkv16, slopes):
    _, bsz, s, _ = qkv1.shape
    npairs = ATT_WIDTH // ATT_PAIR
    spec = lambda c: pl.BlockSpec((1, 1, s, ATT_PAIR), lambda bi, hp, c=c: (c, bi, 0, hp))
    in_specs = [spec(0), spec(1), spec(2)] * 3 + [pl.BlockSpec((1, 1, ATT_PAIR), lambda bi, hp: (hp, 0, 0))]
    return pl.pallas_call(
        _att_kernel,
        grid=(bsz, npairs),
        in_specs=in_specs,
        out_specs=pl.BlockSpec((1, s, ATT_PAIR), lambda bi, hp: (bi, 0, hp)),
        out_shape=jax.ShapeDtypeStruct((bsz, s, ATT_WIDTH), BF16),
        scratch_shapes=[pltpu.VMEM((2, s, ATT_PAIR), BF16), pltpu.VMEM((2, s, 2 * ATT_PAIR), BF16),
                        pltpu.VMEM((3, ATT_QBLOCK, 2 * ATT_WINDOW), F32),
                        pltpu.VMEM((3, s, ATT_PAIR), F32), pltpu.VMEM((3, s, ATT_PAIR), F32)],
        compiler_params=_cparams("parallel", "parallel"),
        name="dilated_attention",
    )(qkv1, qkv1, qkv1, qkv4, qkv4, qkv4, qkv16, qkv16, qkv16, slopes)


def _outproj_kernel(h_ref, yssd_ref, z_ref, yatt_ref, nw_ref, wo_ref, g_ref, b_ref, o_ref):
    y = yssd_ref[...].astype(F32) * _silu(z_ref[...].astype(F32))
    yn = y * lax.rsqrt(jnp.mean(y * y, -1, keepdims=True) + RMS_EPS) * nw_ref[...]
    mix = _dot(yn.astype(BF16), wo_ref[0:SSD_WIDTH, :]) + _dot(yatt_ref[...], wo_ref[SSD_WIDTH:, :])
    o_ref[...] = _ln_rows(DN_ALPHA * h_ref[...] + mix, g_ref[...], b_ref[...])


def _outproj(h, yssd, z, yatt, norm_w, w_out, g, b, tm=512):
    t, d = h.shape
    const = lambda *_: (0, 0)
    wspec = lambda w: pl.BlockSpec(w.shape, const)
    row = lambda n: pl.BlockSpec((tm, n), lambda i: (i, 0))
    return pl.pallas_call(
        _outproj_kernel,
        grid=(t // tm,),
        in_specs=[row(d), row(SSD_WIDTH), row(SSD_WIDTH), row(ATT_WIDTH),
                  wspec(norm_w), wspec(w_out), wspec(g), wspec(b)],
        out_specs=row(d),
        out_shape=jax.ShapeDtypeStruct((t, d), F32),
        compiler_params=_cparams("parallel"),
        name="outproj_ln",
    )(h, yssd, z, yatt, norm_w, w_out, g, b)


def _alibi_slopes():
    sl = np.array([2.0 ** (-8.0 * (h + 1) / ATT_HEADS) for h in range(ATT_HEADS)], dtype=np.float32)
    return jnp.asarray(np.repeat(sl, ATT_HEAD_DIM).reshape(ATT_WIDTH // ATT_PAIR, 1, ATT_PAIR))


def _mixer_params(w_in, conv_w, conv_b, dt_bias_f, dt_bias_b, a_log_f, a_log_b, d_skip):
    r = SSD_HEADS_PER_GROUP
    wz = w_in[:, 0:XBC_OFF].astype(BF16)
    wxbc = w_in[:, XBC_OFF:DT_OFF].astype(BF16)
    wqkv = w_in[:, Q_OFF:IN_COLS].astype(BF16)
    wdt_f = w_in[:, DT_OFF:DT_OFF + SSD_HEADS].reshape(D_MODEL, SSD_GROUPS, r)
    wdt_b = w_in[:, DT_OFF + SSD_HEADS:Q_OFF].reshape(D_MODEL, SSD_GROUPS, r)
    lane_pad = ((0, 0), (0, 0), (0, LANES - 2 * r))
    wdt = jnp.pad(jnp.concatenate([wdt_f, wdt_b], -1), lane_pad).reshape(D_MODEL, SSD_GROUPS * LANES).astype(BF16)

    def per_group_lanes(f, b):
        v = jnp.concatenate([f.reshape(SSD_GROUPS, 1, r), b.reshape(SSD_GROUPS, 1, r)], -1)
        return jnp.pad(v.astype(F32), lane_pad)

    dtb = per_group_lanes(dt_bias_f, dt_bias_b)
    a_f = -jnp.exp(a_log_f.astype(F32))
    a_b = -jnp.exp(a_log_b.astype(F32))

    def per_head_lanes(n):
        v = jnp.stack([a_f.reshape(SSD_GROUPS, r), a_b.reshape(SSD_GROUPS, r)], axis=1)
        return jnp.repeat(v, n, axis=-1).reshape(SSD_GROUPS, 2, 1, r * n)

    a = (per_group_lanes(a_f, a_b), per_head_lanes(SSD_HEAD_DIM))

    def group_cols(w):
        xs = w[..., 0:SSD_WIDTH].reshape(w.shape[:-1] + (SSD_GROUPS, GROUP_X))
        bs = w[..., SSD_WIDTH:SSD_WIDTH + GN].reshape(w.shape[:-1] + (SSD_GROUPS, SSD_STATE))
        cs = w[..., SSD_WIDTH + GN:].reshape(w.shape[:-1] + (SSD_GROUPS, SSD_STATE))
        return jnp.moveaxis(jnp.concatenate([xs, bs, cs], -1), -2, 0)

    cw = group_cols(conv_w.astype(F32))
    cb = group_cols(conv_b.astype(F32)[None])
    dsk = jnp.repeat(d_skip.astype(F32), SSD_HEAD_DIM).reshape(SSD_GROUPS, 1, GROUP_X)
    return wz, wxbc, wdt, wqkv, cw, cb, dtb, a, dsk


def _mixer_layer(x, ln_in_g, ln_in_b, w_in, conv_w, conv_b, dt_bias_f, dt_bias_b, a_log_f, a_log_b,
                 d_skip, ssd_norm_w, w_out_mix, ln1_g, ln1_b):
    bsz, s, d = x.shape
    vec = lambda v: v.astype(F32).reshape(1, -1)
    wz, wxbc, wdt, wqkv, cw, cb, dtb, a, dsk = _mixer_params(
        w_in, conv_w, conv_b, dt_bias_f, dt_bias_b, a_log_f, a_log_b, d_skip)
    h, z, xbc, dt, qkv1, qkv4, qkv16 = _inproj(x, vec(ln_in_g), vec(ln_in_b), wz, wxbc, wdt, wqkv)
    yssd = _ssd(xbc, dt, cw, cb, dtb, a, dsk)
    yatt = _dilated_attention(qkv1, qkv4.reshape(qkv1.shape), qkv16.reshape(qkv1.shape), _alibi_slopes())
    t = bsz * s
    return _outproj(h.reshape(t, d), yssd.reshape(t, SSD_WIDTH), z.reshape(t, SSD_WIDTH),
                    yatt.reshape(t, ATT_WIDTH), vec(ssd_norm_w), w_out_mix.astype(BF16), vec(ln1_g), vec(ln1_b))


def _memkv_kernel(mem_ref, wk_ref, wv_ref, k_ref, v_ref):
    m = mem_ref[0].astype(BF16)
    k_ref[0] = _dot(m, wk_ref[...]).astype(BF16)
    v_ref[0] = _dot(m, wv_ref[...]).astype(BF16)


def _memkv(mem, wk, wv):
    bsz, m, d = mem.shape
    const = lambda *_: (0, 0)
    blk = pl.BlockSpec((1, m, d), lambda bi: (bi, 0, 0))
    return pl.pallas_call(
        _memkv_kernel,
        grid=(bsz,),
        in_specs=[blk, pl.BlockSpec(wk.shape, const), pl.BlockSpec(wv.shape, const)],
        out_specs=(blk, blk),
        out_shape=(jax.ShapeDtypeStruct((bsz, m, d), BF16),) * 2,
        compiler_params=_cparams("parallel"),
        name="mem_kv",
    )(mem, wk, wv)


def _memattn_kernel(h_ref, k_ref, v_ref, wq_ref, wo_ref, g_ref, b_ref, o_ref, ow_ref):
    h = h_ref[0]
    q = (_dot(h.astype(BF16), wq_ref[...]) * (MEM_HEAD_DIM ** -0.5)).astype(BF16)
    xa = None
    for hd in range(MEM_HEADS):
        cols = slice(hd * MEM_HEAD_DIM, (hd + 1) * MEM_HEAD_DIM)
        sc = _dot_nt(q[:, cols], k_ref[0, :, cols])
        p = jnp.exp(sc - jnp.max(sc, axis=-1, keepdims=True))
        o = _dot(p.astype(BF16), v_ref[0, :, cols]) / jnp.sum(p, axis=-1, keepdims=True)
        part = _dot(o.astype(BF16), wo_ref[cols, :])
        xa = part if xa is None else xa + part
    out = _ln_rows(DN_ALPHA * h + xa, g_ref[...], b_ref[...])
    o_ref[0] = out
    ow_ref[0] = _pack_bf16_pairs(out)


def _memattn(h, k, v, wq, wo, g, b, tm=512):
    bsz, s, d = h.shape
    m = k.shape[1]
    const = lambda *_: (0, 0)
    wspec = lambda w: pl.BlockSpec(w.shape, const)
    row = pl.BlockSpec((1, tm, d), lambda bi, i: (bi, i, 0))
    kv = pl.BlockSpec((1, m, d), lambda bi, i: (bi, 0, 0))
    return pl.pallas_call(
        _memattn_kernel,
        grid=(bsz, s // tm),
        in_specs=[row, kv, kv, wspec(wq), wspec(wo), wspec(g), wspec(b)],
        out_specs=(row, pl.BlockSpec((1, tm, d // 2), lambda bi, i: (bi, i, 0))),
        out_shape=(jax.ShapeDtypeStruct((bsz, s, d), F32), jax.ShapeDtypeStruct((bsz, s, d // 2), I32)),
        compiler_params=_cparams("parallel", "parallel"),
        name="mem_attn_ln",
    )(h, k, v, wq, wo, g, b)


def _first_max(work, idx_iota, sentinel):
    m = jnp.max(work, axis=0, keepdims=True)
    idx = jnp.min(jnp.where(work == m, idx_iota, sentinel), axis=0, keepdims=True)
    return m, idx


def _router_kernel(h_ref, wrt_ref, bias_ref, eidx_ref, gate_ref, rank_ref, cnt_ref, run_ref):
    tm = h_ref.shape[0]
    e = N_EXPERTS

    @pl.when(pl.program_id(0) == 0)
    def _():
        run_ref[...] = jnp.zeros_like(run_ref)

    logits = lax.dot_general(wrt_ref[...], h_ref[...], (((1,), (1,)), ((), ())),
                             precision=lax.Precision.HIGHEST, preferred_element_type=F32)
    scores = jax.nn.sigmoid(logits)
    choice = scores + bias_ref[...]
    row = lax.broadcasted_iota(I32, (e, tm), 0)
    grow = lax.broadcasted_iota(I32, (GROUP_SIZE, tm), 0)
    gs = []
    for g in range(ROUTE_GROUPS):
        blk = choice[g * GROUP_SIZE:(g + 1) * GROUP_SIZE, :]
        m1, i1 = _first_max(blk, grow, GROUP_SIZE)
        m2 = jnp.max(jnp.where(grow == i1, -jnp.inf, blk), axis=0, keepdims=True)
        gs.append(m1 + m2)
    work = jnp.concatenate(gs, axis=0)
    giota = lax.broadcasted_iota(I32, (ROUTE_GROUPS, tm), 0)
    gmask = jnp.zeros((ROUTE_GROUPS, tm), jnp.bool_)
    for _ in range(TOPK_GROUPS):
        _, gi = _first_max(work, giota, ROUTE_GROUPS)
        hit = giota == gi
        gmask = jnp.logical_or(gmask, hit)
        work = jnp.where(hit, -jnp.inf, work)
    work = jnp.concatenate(
        [jnp.where(gmask[g:g + 1, :], choice[g * GROUP_SIZE:(g + 1) * GROUP_SIZE, :], NEG_INF)
         for g in range(ROUTE_GROUPS)], axis=0)
    chosen = jnp.zeros((e, tm), jnp.bool_)
    idxs, svals = [], []
    for _ in range(TOP_K):
        _, ei = _first_max(work, row, e)
        hit = row == ei
        svals.append(jnp.sum(jnp.where(hit, scores, 0.0), axis=0, keepdims=True))
        idxs.append(ei)
        chosen = jnp.logical_or(chosen, hit)
        work = jnp.where(hit, -jnp.inf, work)
    eidx = jnp.concatenate(idxs, axis=0)
    sv = jnp.concatenate(svals, axis=0)
    eidx_ref[...] = eidx
    gate_ref[...] = sv / jnp.sum(sv, axis=0, keepdims=True) * ROUTED_SCALE
    upper = (lax.broadcasted_iota(I32, (tm, tm), 0) < lax.broadcasted_iota(I32, (tm, tm), 1))
    mt = chosen.astype(F32)
    cum = _dot(mt.astype(BF16), upper.astype(F32).astype(BF16)) + run_ref[...]
    rank_ref[...] = jnp.concatenate(
        [jnp.sum(jnp.where(row == idxs[k], cum, 0.0), axis=0, keepdims=True) for k in range(TOP_K)],
        axis=0).astype(I32)
    run_ref[...] += jnp.sum(mt, axis=1, keepdims=True)
    cnt_ref[...] = run_ref[...]


def _router(h, w_router_t, bias_col, row0, t, tm=512):
    d = h.shape[1]
    const = lambda *_: (0, 0)
    col = lambda dt: pl.BlockSpec((TOP_K, tm), lambda i: (0, i))
    return pl.pallas_call(
        _router_kernel,
        grid=(t // tm,),
        in_specs=[pl.BlockSpec((tm, d), lambda i: (i + row0 // tm, 0)), pl.BlockSpec(w_router_t.shape, const),
                  pl.BlockSpec(bias_col.shape, const)],
        out_specs=(col(I32), col(F32), col(I32), pl.BlockSpec((N_EXPERTS, 1), const)),
        out_shape=(jax.ShapeDtypeStruct((TOP_K, t), I32), jax.ShapeDtypeStruct((TOP_K, t), F32),
                   jax.ShapeDtypeStruct((TOP_K, t), I32), jax.ShapeDtypeStruct((N_EXPERTS, 1), F32)),
        scratch_shapes=[pltpu.VMEM((N_EXPERTS, 1), F32)],
        compiler_params=_cparams("arbitrary"),
        name="router_topk",
    )(h, w_router_t, bias_col)


def _dispatch_kernel(cnt_ref, eidx_ref, rank_ref, dest_ref, bstart_ref, nb_ref):
    tm = eidx_ref.shape[1]
    e = N_EXPERTS
    nb = jnp.floor((cnt_ref[...] + (MOE_BLOCK - 1)) * (1.0 / MOE_BLOCK))
    strict = (lax.broadcasted_iota(I32, (e, e), 1) < lax.broadcasted_iota(I32, (e, e), 0))
    bstart = _dot(strict.astype(F32).astype(BF16), jnp.broadcast_to(nb, (e, LANES)).astype(BF16))[:, 0:1]
    pstart = bstart * MOE_BLOCK
    row = lax.broadcasted_iota(I32, (e, tm), 0)
    eidx = eidx_ref[...]
    rank = rank_ref[...]
    dest_ref[...] = jnp.concatenate(
        [jnp.sum(jnp.where(row == eidx[k:k + 1, :], pstart, 0.0), axis=0, keepdims=True).astype(I32)
         + rank[k:k + 1, :] for k in range(TOP_K)], axis=0)
    bstart_ref[...] = bstart.astype(I32)
    nb_ref[...] = nb.astype(I32)


def _dispatch(cnt, eidx, rank, tm=2048):
    t = eidx.shape[1]
    const = lambda *_: (0, 0)
    col = pl.BlockSpec((TOP_K, tm), lambda i: (0, i))
    per_expert = pl.BlockSpec((N_EXPERTS, 1), const)
    return pl.pallas_call(
        _dispatch_kernel,
        grid=(t // tm,),
        in_specs=[per_expert, col, col],
        out_specs=(col, per_expert, per_expert),
        out_shape=(jax.ShapeDtypeStruct((TOP_K, t), I32), jax.ShapeDtypeStruct((N_EXPERTS, 1), I32),
                   jax.ShapeDtypeStruct((N_EXPERTS, 1), I32)),
        compiler_params=_cparams("arbitrary"),
        name="dispatch_slots",
    )(cnt, eidx, rank)


EXPERT_ROW_BUFFERS = 4


def _experts_kernel(bstart_ref, nb_ref, xs_hbm, wg_ref, wu_ref, wd_ref, ys_hbm,
                    xbuf, ybuf, wg_b, wu_b, wd_b, in_sem, out_sem):
    e = pl.program_id(0)
    last_e = pl.num_programs(0) - 1
    first_block = bstart_ref[e]
    nused = bstart_ref[last_e] + nb_ref[last_e]
    nbuf = xbuf.shape[0]

    def block_rows(blk):
        return pl.ds(pl.multiple_of(blk * MOE_BLOCK, MOE_BLOCK), MOE_BLOCK)

    def in_copy(blk, slot):
        return pltpu.make_async_copy(xs_hbm.at[block_rows(blk)], xbuf.at[slot], in_sem.at[slot])

    def out_copy(blk, slot):
        return pltpu.make_async_copy(ybuf.at[slot], ys_hbm.at[block_rows(blk)], out_sem.at[slot])

    @pl.when(e == 0)
    def _():
        for ahead in range(nbuf - 1):
            @pl.when(ahead < nused)
            def _():
                in_copy(ahead, ahead).start()

    wg_b[...] = wg_ref[0].astype(BF16)
    wu_b[...] = wu_ref[0].astype(BF16)
    wd_b[...] = wd_ref[0].astype(BF16)

    def body(j, carry):
        blk = first_block + j
        slot = blk % nbuf
        in_copy(blk, slot).wait()
        nxt = blk + nbuf - 1

        @pl.when(nxt < nused)
        def _():
            in_copy(nxt, nxt % nbuf).start()

        @pl.when(blk >= nbuf)
        def _():
            out_copy(blk - nbuf, slot).wait()

        x = _unpack_bf16_pairs(xbuf[slot]).astype(BF16)
        g = _dot(x, wg_b[...])
        u = _dot(x, wu_b[...])
        a = (_silu(g) * u).astype(BF16)
        ybuf[slot] = _pack_bf16_pairs(_dot(a, wd_b[...]))
        out_copy(blk, slot).start()
        return carry

    lax.fori_loop(0, nb_ref[e], body, 0)

    @pl.when(e == last_e)
    def _():
        for back in range(1, nbuf + 1):
            @pl.when(nused >= back)
            def _():
                out_copy(nused - back, (nused - back) % nbuf).wait()


def _experts(bstart, nb, xs, w_gate, w_up, w_down):
    n_pad, words = xs.shape
    n_exp, d, f = w_gate.shape
    wspec = lambda w: pl.BlockSpec((1,) + w.shape[1:], lambda e, bs, nb: (e, 0, 0))
    hbm = pl.BlockSpec(memory_space=pl.ANY)
    return pl.pallas_call(
        _experts_kernel,
        grid_spec=pltpu.PrefetchScalarGridSpec(
            num_scalar_prefetch=2,
            grid=(n_exp,),
            in_specs=[hbm, wspec(w_gate), wspec(w_up), wspec(w_down)],
            out_specs=hbm,
            scratch_shapes=[
                pltpu.VMEM((EXPERT_ROW_BUFFERS, MOE_BLOCK, words), I32),
                pltpu.VMEM((EXPERT_ROW_BUFFERS, MOE_BLOCK, words), I32),
                pltpu.VMEM((d, f), BF16), pltpu.VMEM((d, f), BF16), pltpu.VMEM((f, d), BF16),
                pltpu.SemaphoreType.DMA((EXPERT_ROW_BUFFERS,)), pltpu.SemaphoreType.DMA((EXPERT_ROW_BUFFERS,)),
            ],
        ),
        out_shape=jax.ShapeDtypeStruct((n_pad, words), I32),
        compiler_params=_cparams("arbitrary"),
        name="routed_experts",
    )(bstart, nb, xs, w_gate, w_up, w_down)


def _final_kernel(h_ref, yg_ref, gate_ref, wsg_ref, wsu_ref, wsd_ref, g_ref, b_ref, o_ref):
    h = h_ref[...]
    d = h.shape[1]
    hb = h.astype(BF16)
    ff = _dot((_silu(_dot(hb, wsg_ref[...])) * _dot(hb, wsu_ref[...])).astype(BF16), wsd_ref[...])
    gate = gate_ref[...]
    for k in range(TOP_K):
        ff = ff + gate[:, k:k + 1] * _unpack_bf16_pairs(yg_ref[k])
    o_ref[...] = _ln_rows(DN_ALPHA * h + ff, g_ref[...], b_ref[...])


def _final_into_kernel(prev_ref, *refs):
    del prev_ref
    _final_kernel(*refs)


def _final(h, yg, gate, wsg, wsu, wsd, g, b, row0, out_prev=None, tm=256):
    t_all, d = h.shape
    t = yg.shape[1]
    const = lambda *_: (0, 0)
    wspec = lambda w: pl.BlockSpec(w.shape, const)
    part = lambda n: pl.BlockSpec((tm, n), lambda i: (i, 0))
    full = pl.BlockSpec((tm, d), lambda i: (i + row0 // tm, 0))
    in_specs = [full, pl.BlockSpec((TOP_K, tm, d // 2), lambda i: (0, i, 0)), part(TOP_K),
                wspec(wsg), wspec(wsu), wspec(wsd), wspec(g), wspec(b)]
    args = (h, yg, gate, wsg, wsu, wsd, g, b)
    body, aliases = _final_kernel, {}
    if out_prev is not None:
        body, aliases = _final_into_kernel, {0: 0}
        in_specs = [pl.BlockSpec(memory_space=pl.ANY)] + in_specs
        args = (out_prev,) + args
    return pl.pallas_call(
        body,
        grid=(t // tm,),
        in_specs=in_specs,
        out_specs=full,
        out_shape=jax.ShapeDtypeStruct((t_all, d), F32),
        input_output_aliases=aliases,
        compiler_params=_cparams("parallel"),
        name="shared_combine_ln",
    )(*args)


SC_CORES = 2
SC_SUBCORES = 16
SC_WORKERS = SC_CORES * SC_SUBCORES
SC_WINDOW = 64


def _sc_scatter_rows(x_words, dest_kmajor, n_rows, row0=0):
    w = x_words.shape[1]
    t = dest_kmajor.shape[0] // TOP_K
    per_worker = t // SC_WORKERS
    nchunk = per_worker // SC_WINDOW
    mesh = plsc.VectorSubcoreMesh(core_axis_name="c", subcore_axis_name="s")

    @functools.partial(
        pl.kernel, mesh=mesh,
        out_type=jax.ShapeDtypeStruct((n_rows, w), I32),
        scratch_types=[pltpu.VMEM((SC_WINDOW,), I32), pltpu.VMEM((SC_WINDOW, w), I32), pltpu.SemaphoreType.DMA],
    )
    def scatter(x_hbm, dest_hbm, out_hbm, idx_v, rows_v, sem):
        wid = lax.axis_index("s") * SC_CORES + lax.axis_index("c")
        base = wid * per_worker

        @pl.loop(0, nchunk)
        def _(i):
            t0 = base + i * SC_WINDOW
            pltpu.sync_copy(x_hbm.at[pl.ds(row0 + t0, SC_WINDOW)], rows_v)
            for k in range(TOP_K):
                pltpu.sync_copy(dest_hbm.at[pl.ds(k * t + t0, SC_WINDOW)], idx_v)
                pltpu.async_copy(rows_v, out_hbm.at[idx_v], sem).wait()

    return scatter(x_words, dest_kmajor)


def _sc_gather_rows(table, idx):
    a = idx.shape[0]
    w = table.shape[1]
    per_worker = a // SC_WORKERS
    nchunk = per_worker // SC_WINDOW
    mesh = plsc.VectorSubcoreMesh(core_axis_name="c", subcore_axis_name="s")

    @functools.partial(
        pl.kernel, mesh=mesh,
        out_type=jax.ShapeDtypeStruct((a, w), I32),
        scratch_types=[pltpu.VMEM((SC_WINDOW,), I32), pltpu.VMEM((SC_WINDOW, w), I32), pltpu.SemaphoreType.DMA],
    )
    def gather(table_hbm, idx_hbm, out_hbm, idx_v, rows_v, sem):
        wid = lax.axis_index("s") * SC_CORES + lax.axis_index("c")
        base = wid * per_worker

        @pl.loop(0, nchunk)
        def _(i):
            off = base + i * SC_WINDOW
            pltpu.sync_copy(idx_hbm.at[pl.ds(off, SC_WINDOW)], idx_v)
            pltpu.async_copy(table_hbm.at[idx_v], rows_v, sem).wait()
            pltpu.sync_copy(rows_v, out_hbm.at[pl.ds(off, SC_WINDOW)])

    return gather(table, idx)


def _moe_layer(h2, h2_words, w_router, router_bias, w_gate, w_up, w_down, ws_gate, ws_up, ws_down, ln3_g, ln3_b):
    t_all, d = h2.shape
    vec = lambda v: v.astype(F32).reshape(1, -1)
    w_router_t = w_router.astype(F32).T
    bias_col = router_bias.astype(F32).reshape(-1, 1)
    shared = (ws_gate.astype(BF16), ws_up.astype(BF16), ws_down.astype(BF16), vec(ln3_g), vec(ln3_b))
    t = t_all // MOE_TOKEN_PARTS
    n_blocks = t * TOP_K // MOE_BLOCK + N_EXPERTS
    out = None
    for part in range(MOE_TOKEN_PARTS):
        row0 = part * t
        eidx, gate, rank, cnt = _router(h2, w_router_t, bias_col, row0, t)
        dest, bstart, nb = _dispatch(cnt, eidx, rank)
        dest_flat = dest.reshape(TOP_K * t)
        xs = _sc_scatter_rows(h2_words, dest_flat, n_blocks * MOE_BLOCK, row0)
        ys = _experts(bstart.reshape(N_EXPERTS), nb.reshape(N_EXPERTS), xs, w_gate, w_up, w_down)
        yg = _sc_gather_rows(ys, dest_flat).reshape(TOP_K, t, d // 2)
        out = _final(h2, yg, gate.T, *shared, row0, out)
    return out


def kernel(x, mem, ln_in_g, ln_in_b, w_in, conv_w, conv_b, dt_bias_f, dt_bias_b, a_log_f, a_log_b, d_skip, ssd_norm_w, w_out_mix, ln1_g, ln1_b, wq_mem, wk_mem, wv_mem, wo_mem, ln2_g, ln2_b, w_router, router_bias, w_gate, w_up, w_down, ws_gate, ws_up, ws_down, ln3_g, ln3_b):
    l = 0
    h1 = _mixer_layer(x, ln_in_g, ln_in_b, w_in[l], conv_w[l], conv_b[l], dt_bias_f[l], dt_bias_b[l],
                      a_log_f[l], a_log_b[l], d_skip[l], ssd_norm_w[l], w_out_mix[l], ln1_g[l], ln1_b[l])
    bsz, s, d = x.shape
    vec = lambda v: v.astype(F32).reshape(1, -1)
    km, vm = _memkv(mem, wk_mem[l].astype(BF16), wv_mem[l].astype(BF16))
    h2, h2_words = _memattn(h1.reshape(bsz, s, d), km, vm, wq_mem[l].astype(BF16), wo_mem[l].astype(BF16),
                            vec(ln2_g[l]), vec(ln2_b[l]))
    out = _moe_layer(h2.reshape(bsz * s, d), h2_words.reshape(bsz * s, d // 2), w_router[l], router_bias[l], w_gate[l], w_up[l], w_down[l],
                     ws_gate[l], ws_up[l], ws_down[l], ln3_g[l], ln3_b[l])
    return out.reshape(x.shape)
```

```python
import functools
import math

import numpy as np
import jax
import jax.numpy as jnp
from jax import lax
from jax.experimental import pallas as pl
from jax.experimental.pallas import tpu as pltpu
from jax.experimental.pallas import tpu_sc as plsc

F32 = jnp.float32
BF16 = jnp.bfloat16
I32 = jnp.int32

D_MODEL = 1024
SSD_WIDTH = 1024
SSD_HEAD_DIM = 64
SSD_HEADS = 16
SSD_STATE = 128
SSD_GROUPS = 4
SSD_HEADS_PER_GROUP = SSD_HEADS // SSD_GROUPS
SSD_CONV = 5
SSD_CHUNK = 128
ATT_WIDTH = 1024
ATT_HEAD_DIM = 64
ATT_HEADS = 16
ATT_RADIUS = 64
DILATIONS = (1, 4, 16)
MEM_HEADS = 4
MEM_HEAD_DIM = 256
N_EXPERTS = 256
ROUTE_GROUPS = 8
GROUP_SIZE = N_EXPERTS // ROUTE_GROUPS
TOPK_GROUPS = 4
TOP_K = 8
EXPERT_DIM = 256
ROUTED_SCALE = 2.5
MOE_BLOCK = 256
MOE_TOKEN_PARTS = 2
DN_ALPHA = 2.0 ** 0.25
LN_EPS = 1e-5
RMS_EPS = 1e-5
NEG_INF = -1e30
GN = SSD_GROUPS * SSD_STATE
CONV_CH = SSD_WIDTH + 2 * GN
XBC_OFF = SSD_WIDTH
DT_OFF = XBC_OFF + CONV_CH
Q_OFF = DT_OFF + 2 * SSD_HEADS
IN_COLS = Q_OFF + 3 * ATT_WIDTH

LANES = 128
VMEM_LIMIT_BYTES = 56 * 1024 * 1024


def _cparams(*sem):
    return pltpu.CompilerParams(dimension_semantics=sem, vmem_limit_bytes=VMEM_LIMIT_BYTES)


def _ln_rows(x, g, b):
    mu = jnp.mean(x, -1, keepdims=True)
    xc = x - mu
    var = jnp.mean(xc * xc, -1, keepdims=True)
    return xc * lax.rsqrt(var + LN_EPS) * g + b


def _dot(a, b):
    return jnp.dot(a, b, preferred_element_type=F32)


def _dot_nt(a, b):
    return lax.dot_general(a, b, (((1,), (1,)), ((), ())), preferred_element_type=F32)


def _silu(x):
    return x * jax.nn.sigmoid(x)


def _pack_bf16_pairs(x):
    n = x.shape[1] // 2
    u = lax.bitcast_convert_type(x.astype(BF16).astype(F32), jnp.uint32)
    word = (u[:, :n] >> 16) | (u[:, n:] & jnp.uint32(0xFFFF0000))
    return lax.bitcast_convert_type(word, I32)


def _unpack_bf16_pairs(w):
    u = lax.bitcast_convert_type(w, jnp.uint32)
    lo = lax.bitcast_convert_type(u << 16, F32)
    hi = lax.bitcast_convert_type(u & jnp.uint32(0xFFFF0000), F32)
    return jnp.concatenate([lo, hi], axis=1)


def _inproj_kernel(x_ref, g_ref, b_ref, wz_ref, wxbc_ref, wdt_ref, wqkv_ref,
                   h_ref, z_ref, xbc_ref, dt_ref, qkv1_ref, qkv4_ref, qkv16_ref, stage_ref):
    tm = x_ref.shape[1]
    h = _ln_rows(x_ref[0], g_ref[...], b_ref[...])
    h_ref[0] = h
    hb = h.astype(BF16)
    z_ref[0] = _dot(hb, wz_ref[...]).astype(BF16)
    for c in range(CONV_CH // D_MODEL):
        cols = slice(c * D_MODEL, (c + 1) * D_MODEL)
        xbc_ref[0, :, cols] = _dot(hb, wxbc_ref[:, cols]).astype(BF16)
    dt_ref[0] = _dot(hb, wdt_ref[...])
    for c in range(3):
        cols = slice(c * ATT_WIDTH, (c + 1) * ATT_WIDTH)
        res = _dot(hb, wqkv_ref[:, cols])
        if c == 0:
            res = res * (ATT_HEAD_DIM ** -0.5)
        qkv1_ref[c, 0] = res.astype(BF16)
        for j in range(ATT_WIDTH // LANES):
            lanes = slice(j * LANES, (j + 1) * LANES)
            stage_ref[j] = res[:, lanes]
            for r in range(4):
                qkv4_ref[c, 0, r, :, lanes] = stage_ref[j, pl.ds(r, tm // 4, stride=4), :].astype(BF16)
            for r in range(16):
                qkv16_ref[c, 0, r, :, lanes] = stage_ref[j, pl.ds(r, tm // 16, stride=16), :].astype(BF16)


def _inproj(x, g, b, wz, wxbc, wdt, wqkv, tm=256):
    bsz, s, d = x.shape
    nt = s // tm
    const = lambda *_: (0, 0)
    wspec = lambda w: pl.BlockSpec(w.shape, const)
    row = lambda n: pl.BlockSpec((1, tm, n), lambda bi, i: (bi, i, 0))
    out_shape = (
        jax.ShapeDtypeStruct((bsz, s, d), F32),
        jax.ShapeDtypeStruct((bsz, s, SSD_WIDTH), BF16),
        jax.ShapeDtypeStruct((bsz, s, CONV_CH), BF16),
        jax.ShapeDtypeStruct((bsz, s, SSD_GROUPS * LANES), F32),
        jax.ShapeDtypeStruct((3, bsz, s, ATT_WIDTH), BF16),
        jax.ShapeDtypeStruct((3, bsz, 4, s // 4, ATT_WIDTH), BF16),
        jax.ShapeDtypeStruct((3, bsz, 16, s // 16, ATT_WIDTH), BF16),
    )
    out_specs = (
        row(d), row(SSD_WIDTH), row(CONV_CH), row(SSD_GROUPS * LANES),
        pl.BlockSpec((3, 1, tm, ATT_WIDTH), lambda bi, i: (0, bi, i, 0)),
        pl.BlockSpec((3, 1, 4, tm // 4, ATT_WIDTH), lambda bi, i: (0, bi, 0, i, 0)),
        pl.BlockSpec((3, 1, 16, tm // 16, ATT_WIDTH), lambda bi, i: (0, bi, 0, i, 0)),
    )
    return pl.pallas_call(
        _inproj_kernel,
        grid=(bsz, nt),
        in_specs=[row(d), wspec(g), wspec(b), wspec(wz), wspec(wxbc), wspec(wdt), wspec(wqkv)],
        out_specs=out_specs,
        out_shape=out_shape,
        scratch_shapes=[pltpu.VMEM((ATT_WIDTH // LANES, tm, LANES), F32)],
        compiler_params=_cparams("parallel", "parallel"),
        name="ln_inproj",
    )(x, g, b, wz, wxbc, wdt, wqkv)


GROUP_X = SSD_HEADS_PER_GROUP * SSD_HEAD_DIM
GROUP_CH = GROUP_X + 2 * SSD_STATE
CONV_ROWS = 256
PAD_ROWS = 8

def _softplus(x):
    return jnp.maximum(x, 0.0) + jnp.log(1.0 + jnp.exp(-jnp.abs(x)))


def _ssd_constants():
    q = SSD_CHUNK
    r = SSD_HEADS_PER_GROUP
    i = np.arange(q)
    tris = [(i[:, None] >= i[None, :]), (i[:, None] <= i[None, :])]
    tri_cat = np.stack([np.concatenate([t, t], axis=1) for t in tris])
    trit_cat = np.stack([np.concatenate([t.T, t.T], axis=0) for t in tris])

    src = np.arange(2 * LANES)[:, None] % LANES
    head = np.arange(GROUP_X)[None, :] // SSD_HEAD_DIM
    sel = np.stack([src == head + off for off in (0, r)])
    as_bf16 = lambda m: jnp.asarray(m.astype(np.float32), dtype=BF16)
    return as_bf16(tri_cat), as_bf16(trit_cat), as_bf16(sel)


def _ssd_kernel(x_ref, b_ref, c_ref, dt_ref, cw_ref, cb_ref, dtb_ref, a_ref, ax_ref, dsk_ref,
                tric_ref, tritc_ref, selx_ref, y_ref, pad_ref, act_ref, yacc_ref, st_ref):
    s = x_ref.shape[1]
    q = SSD_CHUNK
    nchunks = s // q

    zeros = jnp.zeros((PAD_ROWS, GROUP_CH), F32)
    pad_ref[0:PAD_ROWS, :] = zeros
    pad_ref[s + PAD_ROWS:s + 2 * PAD_ROWS, :] = zeros
    pad_ref[PAD_ROWS:s + PAD_ROWS, 0:GROUP_X] = x_ref[0].astype(F32)
    pad_ref[PAD_ROWS:s + PAD_ROWS, GROUP_X:GROUP_X + SSD_STATE] = b_ref[0].astype(F32)
    pad_ref[PAD_ROWS:s + PAD_ROWS, GROUP_X + SSD_STATE:GROUP_CH] = c_ref[0].astype(F32)
    for i in range(s // CONV_ROWS):
        acc = jnp.broadcast_to(cb_ref[0], (CONV_ROWS, GROUP_CH))
        for k in range(SSD_CONV):
            r0 = PAD_ROWS + i * CONV_ROWS + k - SSD_CONV // 2
            acc = acc + cw_ref[0, k:k + 1, :] * pad_ref[r0:r0 + CONV_ROWS, :]
        act_ref[i * CONV_ROWS:(i + 1) * CONV_ROWS, :] = _silu(acc)

    row_i = lax.broadcasted_iota(I32, (q, q), 0)
    col_i = lax.broadcasted_iota(I32, (q, q), 1)
    lane = lax.broadcasted_iota(I32, (q, GROUP_X), 1)
    low_half = lax.broadcasted_iota(I32, (q, LANES), 1) < SSD_HEAD_DIM

    def split(v):
        hi = v.astype(BF16)
        return hi, (v - hi.astype(F32)).astype(BF16)

    st_ref[...] = jnp.zeros_like(st_ref)

    def chunk(ci, reverse):
        di = 1 if reverse else 0
        off = SSD_HEADS_PER_GROUP if reverse else 0
        tri = (row_i <= col_i) if reverse else (row_i >= col_i)
        k = dict(tri_cat=tric_ref[di], trit_cat=tritc_ref[di], sel_x=selx_ref[di])
        c = (nchunks - 1 - ci) if reverse else ci
        rows = pl.ds(c * q, q)
        xa = act_ref[rows, 0:GROUP_X]
        bm = act_ref[rows, GROUP_X:GROUP_X + SSD_STATE]
        cm = act_ref[rows, GROUP_X + SSD_STATE:GROUP_CH]
        dt = _softplus(dt_ref[0, rows, :] + dtb_ref[0])
        dcat = jnp.concatenate(split(dt), axis=1)
        dt_x = _dot(dcat, k["sel_x"])
        cs_x = _dot(k["tri_cat"], jnp.concatenate(split(dt_x * ax_ref[0, di]), axis=0))
        cs_w = []
        for pair in range(GROUP_X // LANES):
            blk = cs_x[:, pair * LANES:(pair + 1) * LANES]
            swapped = pltpu.roll(blk, SSD_HEAD_DIM, axis=1)
            cs_w += [jnp.where(low_half, blk, swapped), jnp.where(low_half, swapped, blk)]
        dta_t = (dt * a_ref[0]).T[off:off + 8, :]
        cs_t = _dot(jnp.concatenate(split(dta_t), axis=1), k["trit_cat"])
        tot = cs_x[0:1, :] if reverse else cs_x[q - 1:q, :]
        cm_b = cm.astype(BF16)
        cbm = _dot_nt(cm_b, bm.astype(BF16))
        xdt = xa * dt_x
        xdt_b = xdt.astype(BF16)
        ydiag = None
        for r in range(SSD_HEADS_PER_GROUP):
            seg = cs_w[r] - cs_t[r:r + 1, :]
            w = (jnp.exp(jnp.where(tri, seg, NEG_INF)) * cbm).astype(BF16)
            yr = _dot(w, xdt_b)
            ydiag = yr if r == 0 else jnp.where(lane >= r * SSD_HEAD_DIM, yr, ydiag)
        xs_b = (xdt * jnp.exp(tot - cs_x)).astype(BF16)
        states = _dot(bm.T.astype(BF16), xs_b)
        hprev = st_ref[di]
        yoff = _dot(cm_b, hprev.astype(BF16)) * jnp.exp(cs_x)
        st_ref[di] = hprev * jnp.exp(tot) + states
        yacc_ref[di, rows, :] = ydiag + yoff

    for ci in range(nchunks):
        chunk(ci, False)
        chunk(ci, True)
    y_ref[0] = (yacc_ref[0] + yacc_ref[1] + act_ref[:, 0:GROUP_X] * dsk_ref[0]).astype(BF16)


def _ssd(xbc, dt, conv_w_g, conv_b_g, dt_bias_g, a_g, dskip_g):
    bsz, s, _ = xbc.shape
    a_lanes, a_x = a_g
    grp = lambda n: pl.BlockSpec((1,) + n, lambda bi, gi: (gi,) + (0,) * len(n))
    in_specs = [
        pl.BlockSpec((1, s, GROUP_X), lambda bi, gi: (bi, 0, gi)),
        pl.BlockSpec((1, s, SSD_STATE), lambda bi, gi: (bi, 0, SSD_WIDTH // SSD_STATE + gi)),
        pl.BlockSpec((1, s, SSD_STATE), lambda bi, gi: (bi, 0, (SSD_WIDTH + GN) // SSD_STATE + gi)),
        pl.BlockSpec((1, s, LANES), lambda bi, gi: (bi, 0, gi)),
        grp((SSD_CONV, GROUP_CH)), grp((1, GROUP_CH)), grp((1, LANES)), grp((1, LANES)),
        grp((2, 1, GROUP_X)), grp((1, GROUP_X)),
    ]
    consts = _ssd_constants()
    in_specs += [pl.BlockSpec(c.shape, lambda bi, gi: (0, 0, 0)) for c in consts]
    return pl.pallas_call(
        _ssd_kernel,
        grid=(bsz, SSD_GROUPS),
        in_specs=in_specs,
        out_specs=pl.BlockSpec((1, s, GROUP_X), lambda bi, gi: (bi, 0, gi)),
        out_shape=jax.ShapeDtypeStruct((bsz, s, SSD_WIDTH), BF16),
        scratch_shapes=[
            pltpu.VMEM((s + 2 * PAD_ROWS, GROUP_CH), F32),
            pltpu.VMEM((s, GROUP_CH), F32),
            pltpu.VMEM((2, s, GROUP_X), F32),
            pltpu.VMEM((2, SSD_STATE, GROUP_X), F32),
        ],
        compiler_params=_cparams("parallel", "parallel"),
        name="ssd_scan",
    )(xbc, xbc, xbc, dt, conv_w_g, conv_b_g, dt_bias_g, a_lanes, a_x, dskip_g, *consts)


ATT_QBLOCK = 128
ATT_PAIR = 2 * ATT_HEAD_DIM
ATT_WINDOW = 2 * ATT_QBLOCK


def _att_kernel(n_cast, q1, k1, v1, q4, k4, v4, q16, k16, v16, slope_ref, *rest):
    cast_in, (o_ref, *cast_out), (km_ref, vm_ref, bias_ref, ob_ref, lb_ref) = (
        rest[:n_cast], rest[n_cast:2 * n_cast + 1], rest[2 * n_cast + 1:])
    for src, dst in zip(cast_in, cast_out):
        dst[...] = src[...].astype(BF16)
    s = o_ref.shape[1]
    qb = ATT_QBLOCK
    first = lax.broadcasted_iota(I32, (qb, ATT_PAIR), 1) < ATT_HEAD_DIM
    first_all = lax.broadcasted_iota(I32, (s, ATT_PAIR), 1) < ATT_HEAD_DIM
    slope_lanes = slope_ref[0]
    ind = first_all.astype(F32).astype(BF16)

    def branch(bi, dil, q_ref, k_ref, v_ref):
        seg_len = s // dil
        win = min(ATT_WINDOW, seg_len)
        kk = k_ref[0, 0]
        vv = v_ref[0, 0]
        zero = jnp.zeros_like(kk)
        km_ref[0] = jnp.where(first_all, kk, zero)
        km_ref[1] = jnp.where(first_all, zero, kk)
        vm_ref[0] = jnp.concatenate([jnp.where(first_all, vv, zero), ind], axis=1)
        vm_ref[1] = jnp.concatenate([jnp.where(first_all, zero, vv), 1 - ind], axis=1)
        shifts = sorted({min(max(w0 - ATT_RADIUS, 0), seg_len - win) - w0 for w0 in range(0, seg_len, qb)},
                        reverse=True)
        rel0 = lax.broadcasted_iota(I32, (qb, win), 1) - lax.broadcasted_iota(I32, (qb, win), 0)
        for vi, shift in enumerate(shifts):
            rel = jnp.abs(rel0 + shift)
            dist = (dil * rel).astype(F32)
            for j in range(2):
                slope = slope_lanes[:, j * ATT_HEAD_DIM:j * ATT_HEAD_DIM + 1]
                bias_ref[vi, :, j * win:(j + 1) * win] = jnp.where(rel <= ATT_RADIUS, -slope * dist, NEG_INF)

        for blk in range(s // qb):
            qs = blk * qb
            seg, within = divmod(qs, seg_len)
            kw = min(max(within - ATT_RADIUS, 0), seg_len - win)
            ks = seg * seg_len + kw
            vi = shifts.index(kw - within)
            kcat = jnp.concatenate([km_ref[0, ks:ks + win, :], km_ref[1, ks:ks + win, :]], axis=0)
            vcat = jnp.concatenate([vm_ref[0, ks:ks + win, :], vm_ref[1, ks:ks + win, :]], axis=0)
            sc = _dot_nt(q_ref[0, 0, qs:qs + qb, :], kcat) + bias_ref[vi, :, 0:2 * win]
            s0, s1 = sc[:, :win], sc[:, win:]
            m0 = jnp.max(s0, axis=-1, keepdims=True)
            m1 = jnp.max(s1, axis=-1, keepdims=True)
            p0 = jnp.exp(s0 - m0)
            p1 = jnp.exp(s1 - m1)
            ol = _dot(jnp.concatenate([p0, p1], axis=1).astype(BF16), vcat)
            l = ol[:, ATT_PAIR:]
            o_blk = ol[:, :ATT_PAIR] / l
            l_blk = jnp.where(first, m0, m1) + jnp.log(l)
            if dil == 1:
                dst = pl.ds(qs, qb)
            else:
                dst = pl.ds(within * dil + seg, qb, stride=dil)
            ob_ref[bi, dst, :] = o_blk
            lb_ref[bi, dst, :] = l_blk

    branch(0, DILATIONS[0], q1, k1, v1)
    branch(1, DILATIONS[1], q4, k4, v4)
    branch(2, DILATIONS[2], q16, k16, v16)

    rows = 256
    for i in range(s // rows):
        sl = slice(i * rows, (i + 1) * rows)
        l0, l1, l2 = lb_ref[0, sl, :], lb_ref[1, sl, :], lb_ref[2, sl, :]
        m = jnp.maximum(jnp.maximum(l0, l1), l2)
        w0, w1, w2 = jnp.exp(l0 - m), jnp.exp(l1 - m), jnp.exp(l2 - m)
        num = w0 * ob_ref[0, sl, :] + w1 * ob_ref[1, sl, :] + w2 * ob_ref[2, sl, :]
        o_ref[0, sl, :] = (num / (w0 + w1 + w2)).astype(BF16)


def _dilated_attention(qkv1, qkv4, qkv16, slopes, cast=()):
    _, bsz, s, _ = qkv1.shape
    npairs = ATT_WIDTH // ATT_PAIR
    spec = lambda c: pl.BlockSpec((1, 1, s, ATT_PAIR), lambda bi, hp, c=c: (c, bi, 0, hp))
    in_specs = [spec(0), spec(1), spec(2)] * 3 + [pl.BlockSpec((1, 1, ATT_PAIR), lambda bi, hp: (hp, 0, 0))]
    steps = bsz * npairs
    cast_specs = [pl.BlockSpec((w.shape[0] // steps,) + w.shape[1:], lambda bi, hp: (bi * npairs + hp, 0, 0))
                  for w in cast]
    out = pl.pallas_call(
        functools.partial(_att_kernel, len(cast)),
        grid=(bsz, npairs),
        in_specs=in_specs + cast_specs,
        out_specs=[pl.BlockSpec((1, s, ATT_PAIR), lambda bi, hp: (bi, 0, hp))] + cast_specs,
        out_shape=[jax.ShapeDtypeStruct((bsz, s, ATT_WIDTH), BF16)]
        + [jax.ShapeDtypeStruct(w.shape, BF16) for w in cast],
        scratch_shapes=[pltpu.VMEM((2, s, ATT_PAIR), BF16), pltpu.VMEM((2, s, 2 * ATT_PAIR), BF16),
                        pltpu.VMEM((3, ATT_QBLOCK, 2 * ATT_WINDOW), F32),
                        pltpu.VMEM((3, s, ATT_PAIR), F32), pltpu.VMEM((3, s, ATT_PAIR), F32)],
        compiler_params=_cparams("parallel", "parallel"),
        name="dilated_attention",
    )(qkv1, qkv1, qkv1, qkv4, qkv4, qkv4, qkv16, qkv16, qkv16, slopes, *cast)
    return out[0], tuple(out[1:])


def _outproj_kernel(h_ref, yssd_ref, z_ref, yatt_ref, nw_ref, wo_ref, g_ref, b_ref, o_ref):
    y = yssd_ref[...].astype(F32) * _silu(z_ref[...].astype(F32))
    yn = y * lax.rsqrt(jnp.mean(y * y, -1, keepdims=True) + RMS_EPS) * nw_ref[...]
    mix = _dot(yn.astype(BF16), wo_ref[0:SSD_WIDTH, :]) + _dot(yatt_ref[...], wo_ref[SSD_WIDTH:, :])
    o_ref[...] = _ln_rows(DN_ALPHA * h_ref[...] + mix, g_ref[...], b_ref[...])


def _outproj(h, yssd, z, yatt, norm_w, w_out, g, b, tm=512):
    t, d = h.shape
    const = lambda *_: (0, 0)
    wspec = lambda w: pl.BlockSpec(w.shape, const)
    row = lambda n: pl.BlockSpec((tm, n), lambda i: (i, 0))
    return pl.pallas_call(
        _outproj_kernel,
        grid=(t // tm,),
        in_specs=[row(d), row(SSD_WIDTH), row(SSD_WIDTH), row(ATT_WIDTH),
                  wspec(norm_w), wspec(w_out), wspec(g), wspec(b)],
        out_specs=row(d),
        out_shape=jax.ShapeDtypeStruct((t, d), F32),
        compiler_params=_cparams("parallel"),
        name="outproj_ln",
    )(h, yssd, z, yatt, norm_w, w_out, g, b)


def _alibi_slopes():
    sl = np.array([2.0 ** (-8.0 * (h + 1) / ATT_HEADS) for h in range(ATT_HEADS)], dtype=np.float32)
    return jnp.asarray(np.repeat(sl, ATT_HEAD_DIM).reshape(ATT_WIDTH // ATT_PAIR, 1, ATT_PAIR))


def _mixer_params(w_in, conv_w, conv_b, dt_bias_f, dt_bias_b, a_log_f, a_log_b, d_skip):
    r = SSD_HEADS_PER_GROUP
    wz = w_in[:, 0:XBC_OFF].astype(BF16)
    wxbc = w_in[:, XBC_OFF:DT_OFF].astype(BF16)
    wqkv = w_in[:, Q_OFF:IN_COLS].astype(BF16)
    wdt_f = w_in[:, DT_OFF:DT_OFF + SSD_HEADS].reshape(D_MODEL, SSD_GROUPS, r)
    wdt_b = w_in[:, DT_OFF + SSD_HEADS:Q_OFF].reshape(D_MODEL, SSD_GROUPS, r)
    lane_pad = ((0, 0), (0, 0), (0, LANES - 2 * r))
    wdt = jnp.pad(jnp.concatenate([wdt_f, wdt_b], -1), lane_pad).reshape(D_MODEL, SSD_GROUPS * LANES).astype(BF16)

    def per_group_lanes(f, b):
        v = jnp.concatenate([f.reshape(SSD_GROUPS, 1, r), b.reshape(SSD_GROUPS, 1, r)], -1)
        return jnp.pad(v.astype(F32), lane_pad)

    dtb = per_group_lanes(dt_bias_f, dt_bias_b)
    a_f = -jnp.exp(a_log_f.astype(F32))
    a_b = -jnp.exp(a_log_b.astype(F32))

    def per_head_lanes(n):
        v = jnp.stack([a_f.reshape(SSD_GROUPS, r), a_b.reshape(SSD_GROUPS, r)], axis=1)
        return jnp.repeat(v, n, axis=-1).reshape(SSD_GROUPS, 2, 1, r * n)

    a = (per_group_lanes(a_f, a_b), per_head_lanes(SSD_HEAD_DIM))

    def group_cols(w):
        xs = w[..., 0:SSD_WIDTH].reshape(w.shape[:-1] + (SSD_GROUPS, GROUP_X))
        bs = w[..., SSD_WIDTH:SSD_WIDTH + GN].reshape(w.shape[:-1] + (SSD_GROUPS, SSD_STATE))
        cs = w[..., SSD_WIDTH + GN:].reshape(w.shape[:-1] + (SSD_GROUPS, SSD_STATE))
        return jnp.moveaxis(jnp.concatenate([xs, bs, cs], -1), -2, 0)

    cw = group_cols(conv_w.astype(F32))
    cb = group_cols(conv_b.astype(F32)[None])
    dsk = jnp.repeat(d_skip.astype(F32), SSD_HEAD_DIM).reshape(SSD_GROUPS, 1, GROUP_X)
    return wz, wxbc, wdt, wqkv, cw, cb, dtb, a, dsk


def _mixer_layer(x, ln_in_g, ln_in_b, w_in, conv_w, conv_b, dt_bias_f, dt_bias_b, a_log_f, a_log_b,
                 d_skip, ssd_norm_w, w_out_mix, ln1_g, ln1_b, cast=()):
    bsz, s, d = x.shape
    vec = lambda v: v.astype(F32).reshape(1, -1)
    wz, wxbc, wdt, wqkv, cw, cb, dtb, a, dsk = _mixer_params(
        w_in, conv_w, conv_b, dt_bias_f, dt_bias_b, a_log_f, a_log_b, d_skip)
    h, z, xbc, dt, qkv1, qkv4, qkv16 = _inproj(x, vec(ln_in_g), vec(ln_in_b), wz, wxbc, wdt, wqkv)
    yssd = _ssd(xbc, dt, cw, cb, dtb, a, dsk)
    yatt, cast_bf16 = _dilated_attention(qkv1, qkv4.reshape(qkv1.shape), qkv16.reshape(qkv1.shape),
                                         _alibi_slopes(), cast)
    t = bsz * s
    h1 = _outproj(h.reshape(t, d), yssd.reshape(t, SSD_WIDTH), z.reshape(t, SSD_WIDTH),
                  yatt.reshape(t, ATT_WIDTH), vec(ssd_norm_w), w_out_mix.astype(BF16), vec(ln1_g), vec(ln1_b))
    return h1, cast_bf16


def _memkv_kernel(mem_ref, wk_ref, wv_ref, k_ref, v_ref):
    m = mem_ref[0].astype(BF16)
    k_ref[0] = _dot(m, wk_ref[...]).astype(BF16)
    v_ref[0] = _dot(m, wv_ref[...]).astype(BF16)


def _memkv(mem, wk, wv):
    bsz, m, d = mem.shape
    const = lambda *_: (0, 0)
    blk = pl.BlockSpec((1, m, d), lambda bi: (bi, 0, 0))
    return pl.pallas_call(
        _memkv_kernel,
        grid=(bsz,),
        in_specs=[blk, pl.BlockSpec(wk.shape, const), pl.BlockSpec(wv.shape, const)],
        out_specs=(blk, blk),
        out_shape=(jax.ShapeDtypeStruct((bsz, m, d), BF16),) * 2,
        compiler_params=_cparams("parallel"),
        name="mem_kv",
    )(mem, wk, wv)


def _memattn_kernel(h_ref, k_ref, v_ref, wq_ref, wo_ref, g_ref, b_ref, o_ref, ow_ref):
    h = h_ref[0]
    q = (_dot(h.astype(BF16), wq_ref[...]) * (MEM_HEAD_DIM ** -0.5)).astype(BF16)
    xa = None
    for hd in range(MEM_HEADS):
        cols = slice(hd * MEM_HEAD_DIM, (hd + 1) * MEM_HEAD_DIM)
        sc = _dot_nt(q[:, cols], k_ref[0, :, cols])
        p = jnp.exp(sc - jnp.max(sc, axis=-1, keepdims=True))
        o = _dot(p.astype(BF16), v_ref[0, :, cols]) / jnp.sum(p, axis=-1, keepdims=True)
        part = _dot(o.astype(BF16), wo_ref[cols, :])
        xa = part if xa is None else xa + part
    out = _ln_rows(DN_ALPHA * h + xa, g_ref[...], b_ref[...])
    o_ref[0] = out
    ow_ref[0] = _pack_bf16_pairs(out)


def _memattn(h, k, v, wq, wo, g, b, tm=512):
    bsz, s, d = h.shape
    m = k.shape[1]
    const = lambda *_: (0, 0)
    wspec = lambda w: pl.BlockSpec(w.shape, const)
    row = pl.BlockSpec((1, tm, d), lambda bi, i: (bi, i, 0))
    kv = pl.BlockSpec((1, m, d), lambda bi, i: (bi, 0, 0))
    return pl.pallas_call(
        _memattn_kernel,
        grid=(bsz, s // tm),
        in_specs=[row, kv, kv, wspec(wq), wspec(wo), wspec(g), wspec(b)],
        out_specs=(row, pl.BlockSpec((1, tm, d // 2), lambda bi, i: (bi, i, 0))),
        out_shape=(jax.ShapeDtypeStruct((bsz, s, d), F32), jax.ShapeDtypeStruct((bsz, s, d // 2), I32)),
        compiler_params=_cparams("parallel", "parallel"),
        name="mem_attn_ln",
    )(h, k, v, wq, wo, g, b)


def _first_max(work, idx_iota, sentinel):
    m = jnp.max(work, axis=0, keepdims=True)
    idx = jnp.min(jnp.where(work == m, idx_iota, sentinel), axis=0, keepdims=True)
    return m, idx


def _router_kernel(h_ref, wrt_ref, bias_ref, eidx_ref, gate_ref, rank_ref, cnt_ref, run_ref):
    tm = h_ref.shape[0]
    e = N_EXPERTS

    @pl.when(pl.program_id(0) == 0)
    def _():
        run_ref[...] = jnp.zeros_like(run_ref)

    logits = lax.dot_general(wrt_ref[...], h_ref[...], (((1,), (1,)), ((), ())),
                             precision=lax.Precision.HIGHEST, preferred_element_type=F32)
    scores = jax.nn.sigmoid(logits)
    choice = scores + bias_ref[...]
    row = lax.broadcasted_iota(I32, (e, tm), 0)
    grow = lax.broadcasted_iota(I32, (GROUP_SIZE, tm), 0)
    gs = []
    for g in range(ROUTE_GROUPS):
        blk = choice[g * GROUP_SIZE:(g + 1) * GROUP_SIZE, :]
        m1, i1 = _first_max(blk, grow, GROUP_SIZE)
        m2 = jnp.max(jnp.where(grow == i1, -jnp.inf, blk), axis=0, keepdims=True)
        gs.append(m1 + m2)
    work = jnp.concatenate(gs, axis=0)
    giota = lax.broadcasted_iota(I32, (ROUTE_GROUPS, tm), 0)
    gmask = jnp.zeros((ROUTE_GROUPS, tm), jnp.bool_)
    for _ in range(TOPK_GROUPS):
        _, gi = _first_max(work, giota, ROUTE_GROUPS)
        hit = giota == gi
        gmask = jnp.logical_or(gmask, hit)
        work = jnp.where(hit, -jnp.inf, work)
    work = jnp.concatenate(
        [jnp.where(gmask[g:g + 1, :], choice[g * GROUP_SIZE:(g + 1) * GROUP_SIZE, :], NEG_INF)
         for g in range(ROUTE_GROUPS)], axis=0)
    chosen = jnp.zeros((e, tm), jnp.bool_)
    idxs, svals = [], []
    for _ in range(TOP_K):
        _, ei = _first_max(work, row, e)
        hit = row == ei
        svals.append(jnp.sum(jnp.where(hit, scores, 0.0), axis=0, keepdims=True))
        idxs.append(ei)
        chosen = jnp.logical_or(chosen, hit)
        work = jnp.where(hit, -jnp.inf, work)
    eidx = jnp.concatenate(idxs, axis=0)
    sv = jnp.concatenate(svals, axis=0)
    eidx_ref[...] = eidx
    gate_ref[...] = sv / jnp.sum(sv, axis=0, keepdims=True) * ROUTED_SCALE
    upper = (lax.broadcasted_iota(I32, (tm, tm), 0) < lax.broadcasted_iota(I32, (tm, tm), 1))
    mt = chosen.astype(F32)
    cum = _dot(mt.astype(BF16), upper.astype(F32).astype(BF16)) + run_ref[...]
    rank_ref[...] = jnp.concatenate(
        [jnp.sum(jnp.where(row == idxs[k], cum, 0.0), axis=0, keepdims=True) for k in range(TOP_K)],
        axis=0).astype(I32)
    run_ref[...] += jnp.sum(mt, axis=1, keepdims=True)
    cnt_ref[...] = run_ref[...]


def _router(h, w_router_t, bias_col, row0, t, tm=512):
    d = h.shape[1]
    const = lambda *_: (0, 0)
    col = lambda dt: pl.BlockSpec((TOP_K, tm), lambda i: (0, i))
    return pl.pallas_call(
        _router_kernel,
        grid=(t // tm,),
        in_specs=[pl.BlockSpec((tm, d), lambda i: (i + row0 // tm, 0)), pl.BlockSpec(w_router_t.shape, const),
                  pl.BlockSpec(bias_col.shape, const)],
        out_specs=(col(I32), col(F32), col(I32), pl.BlockSpec((N_EXPERTS, 1), const)),
        out_shape=(jax.ShapeDtypeStruct((TOP_K, t), I32), jax.ShapeDtypeStruct((TOP_K, t), F32),
                   jax.ShapeDtypeStruct((TOP_K, t), I32), jax.ShapeDtypeStruct((N_EXPERTS, 1), F32)),
        scratch_shapes=[pltpu.VMEM((N_EXPERTS, 1), F32)],
        compiler_params=_cparams("arbitrary"),
        name="router_topk",
    )(h, w_router_t, bias_col)


def _dispatch_kernel(cnt_ref, eidx_ref, rank_ref, dest_ref, bstart_ref, nb_ref):
    tm = eidx_ref.shape[1]
    e = N_EXPERTS
    nb = jnp.floor((cnt_ref[...] + (MOE_BLOCK - 1)) * (1.0 / MOE_BLOCK))
    strict = (lax.broadcasted_iota(I32, (e, e), 1) < lax.broadcasted_iota(I32, (e, e), 0))
    bstart = _dot(strict.astype(F32).astype(BF16), jnp.broadcast_to(nb, (e, LANES)).astype(BF16))[:, 0:1]
    pstart = bstart * MOE_BLOCK
    row = lax.broadcasted_iota(I32, (e, tm), 0)
    eidx = eidx_ref[...]
    rank = rank_ref[...]
    dest_ref[...] = jnp.concatenate(
        [jnp.sum(jnp.where(row == eidx[k:k + 1, :], pstart, 0.0), axis=0, keepdims=True).astype(I32)
         + rank[k:k + 1, :] for k in range(TOP_K)], axis=0)
    bstart_ref[...] = bstart.astype(I32)
    nb_ref[...] = nb.astype(I32)


def _dispatch(cnt, eidx, rank, tm=2048):
    t = eidx.shape[1]
    const = lambda *_: (0, 0)
    col = pl.BlockSpec((TOP_K, tm), lambda i: (0, i))
    per_expert = pl.BlockSpec((N_EXPERTS, 1), const)
    return pl.pallas_call(
        _dispatch_kernel,
        grid=(t // tm,),
        in_specs=[per_expert, col, col],
        out_specs=(col, per_expert, per_expert),
        out_shape=(jax.ShapeDtypeStruct((TOP_K, t), I32), jax.ShapeDtypeStruct((N_EXPERTS, 1), I32),
                   jax.ShapeDtypeStruct((N_EXPERTS, 1), I32)),
        compiler_params=_cparams("arbitrary"),
        name="dispatch_slots",
    )(cnt, eidx, rank)


EXPERT_ROW_BUFFERS = 4


def _experts_kernel(bstart_ref, nb_ref, xs_hbm, wg_ref, wu_ref, wd_ref, ys_hbm,
                    xbuf, ybuf, in_sem, out_sem):
    e = pl.program_id(0)
    last_e = pl.num_programs(0) - 1
    first_block = bstart_ref[e]
    nused = bstart_ref[last_e] + nb_ref[last_e]
    nbuf = xbuf.shape[0]

    def block_rows(blk):
        return pl.ds(pl.multiple_of(blk * MOE_BLOCK, MOE_BLOCK), MOE_BLOCK)

    def in_copy(blk, slot):
        return pltpu.make_async_copy(xs_hbm.at[block_rows(blk)], xbuf.at[slot], in_sem.at[slot])

    def out_copy(blk, slot):
        return pltpu.make_async_copy(ybuf.at[slot], ys_hbm.at[block_rows(blk)], out_sem.at[slot])

    @pl.when(e == 0)
    def _():
        for ahead in range(nbuf - 1):
            @pl.when(ahead < nused)
            def _():
                in_copy(ahead, ahead).start()

    def body(j, carry):
        blk = first_block + j
        slot = blk % nbuf
        in_copy(blk, slot).wait()
        nxt = blk + nbuf - 1

        @pl.when(nxt < nused)
        def _():
            in_copy(nxt, nxt % nbuf).start()

        @pl.when(blk >= nbuf)
        def _():
            out_copy(blk - nbuf, slot).wait()

        x = _unpack_bf16_pairs(xbuf[slot]).astype(BF16)
        g = _dot(x, wg_ref[0])
        u = _dot(x, wu_ref[0])
        a = (_silu(g) * u).astype(BF16)
        ybuf[slot] = _pack_bf16_pairs(_dot(a, wd_ref[0]))
        out_copy(blk, slot).start()
        return carry

    lax.fori_loop(0, nb_ref[e], body, 0)

    @pl.when(e == last_e)
    def _():
        for back in range(1, nbuf + 1):
            @pl.when(nused >= back)
            def _():
                out_copy(nused - back, (nused - back) % nbuf).wait()


def _experts(bstart, nb, xs, w_gate, w_up, w_down):
    n_pad, words = xs.shape
    n_exp, d, f = w_gate.shape
    wspec = lambda w: pl.BlockSpec((1,) + w.shape[1:], lambda e, bs, nb: (e, 0, 0))
    hbm = pl.BlockSpec(memory_space=pl.ANY)
    return pl.pallas_call(
        _experts_kernel,
        grid_spec=pltpu.PrefetchScalarGridSpec(
            num_scalar_prefetch=2,
            grid=(n_exp,),
            in_specs=[hbm, wspec(w_gate), wspec(w_up), wspec(w_down)],
            out_specs=hbm,
            scratch_shapes=[
                pltpu.VMEM((EXPERT_ROW_BUFFERS, MOE_BLOCK, words), I32),
                pltpu.VMEM((EXPERT_ROW_BUFFERS, MOE_BLOCK, words), I32),
                pltpu.SemaphoreType.DMA((EXPERT_ROW_BUFFERS,)), pltpu.SemaphoreType.DMA((EXPERT_ROW_BUFFERS,)),
            ],
        ),
        out_shape=jax.ShapeDtypeStruct((n_pad, words), I32),
        compiler_params=_cparams("arbitrary"),
        name="routed_experts",
    )(bstart, nb, xs, w_gate, w_up, w_down)


def _final_kernel(h_ref, yg_ref, gate_ref, wsg_ref, wsu_ref, wsd_ref, g_ref, b_ref, o_ref):
    h = h_ref[...]
    d = h.shape[1]
    hb = h.astype(BF16)
    ff = _dot((_silu(_dot(hb, wsg_ref[...])) * _dot(hb, wsu_ref[...])).astype(BF16), wsd_ref[...])
    gate = gate_ref[...]
    for k in range(TOP_K):
        ff = ff + gate[:, k:k + 1] * _unpack_bf16_pairs(yg_ref[k])
    o_ref[...] = _ln_rows(DN_ALPHA * h + ff, g_ref[...], b_ref[...])


def _final_into_kernel(prev_ref, *refs):
    del prev_ref
    _final_kernel(*refs)


def _final(h, yg, gate, wsg, wsu, wsd, g, b, row0, out_prev=None, tm=256):
    t_all, d = h.shape
    t = yg.shape[1]
    const = lambda *_: (0, 0)
    wspec = lambda w: pl.BlockSpec(w.shape, const)
    part = lambda n: pl.BlockSpec((tm, n), lambda i: (i, 0))
    full = pl.BlockSpec((tm, d), lambda i: (i + row0 // tm, 0))
    in_specs = [full, pl.BlockSpec((TOP_K, tm, d // 2), lambda i: (0, i, 0)), part(TOP_K),
                wspec(wsg), wspec(wsu), wspec(wsd), wspec(g), wspec(b)]
    args = (h, yg, gate, wsg, wsu, wsd, g, b)
    body, aliases = _final_kernel, {}
    if out_prev is not None:
        body, aliases = _final_into_kernel, {0: 0}
        in_specs = [pl.BlockSpec(memory_space=pl.ANY)] + in_specs
        args = (out_prev,) + args
    return pl.pallas_call(
        body,
        grid=(t // tm,),
        in_specs=in_specs,
        out_specs=full,
        out_shape=jax.ShapeDtypeStruct((t_all, d), F32),
        input_output_aliases=aliases,
        compiler_params=_cparams("parallel"),
        name="shared_combine_ln",
    )(*args)


SC_CORES = 2
SC_SUBCORES = 16
SC_WORKERS = SC_CORES * SC_SUBCORES
SC_WINDOW = 64


def _sc_scatter_rows(x_words, dest_kmajor, n_rows, row0=0):
    w = x_words.shape[1]
    t = dest_kmajor.shape[0] // TOP_K
    per_worker = t // SC_WORKERS
    nchunk = per_worker // SC_WINDOW
    mesh = plsc.VectorSubcoreMesh(core_axis_name="c", subcore_axis_name="s")

    @functools.partial(
        pl.kernel, mesh=mesh,
        out_type=jax.ShapeDtypeStruct((n_rows, w), I32),
        scratch_types=[pltpu.VMEM((SC_WINDOW,), I32), pltpu.VMEM((SC_WINDOW, w), I32), pltpu.SemaphoreType.DMA],
    )
    def scatter(x_hbm, dest_hbm, out_hbm, idx_v, rows_v, sem):
        wid = lax.axis_index("s") * SC_CORES + lax.axis_index("c")
        base = wid * per_worker

        @pl.loop(0, nchunk)
        def _(i):
            t0 = base + i * SC_WINDOW
            pltpu.sync_copy(x_hbm.at[pl.ds(row0 + t0, SC_WINDOW)], rows_v)
            for k in range(TOP_K):
                pltpu.sync_copy(dest_hbm.at[pl.ds(k * t + t0, SC_WINDOW)], idx_v)
                pltpu.async_copy(rows_v, out_hbm.at[idx_v], sem).wait()

    return scatter(x_words, dest_kmajor)


def _sc_gather_rows(table, idx):
    a = idx.shape[0]
    w = table.shape[1]
    per_worker = a // SC_WORKERS
    nchunk = per_worker // SC_WINDOW
    mesh = plsc.VectorSubcoreMesh(core_axis_name="c", subcore_axis_name="s")

    @functools.partial(
        pl.kernel, mesh=mesh,
        out_type=jax.ShapeDtypeStruct((a, w), I32),
        scratch_types=[pltpu.VMEM((SC_WINDOW,), I32), pltpu.VMEM((SC_WINDOW, w), I32), pltpu.SemaphoreType.DMA],
    )
    def gather(table_hbm, idx_hbm, out_hbm, idx_v, rows_v, sem):
        wid = lax.axis_index("s") * SC_CORES + lax.axis_index("c")
        base = wid * per_worker

        @pl.loop(0, nchunk)
        def _(i):
            off = base + i * SC_WINDOW
            pltpu.sync_copy(idx_hbm.at[pl.ds(off, SC_WINDOW)], idx_v)
            pltpu.async_copy(table_hbm.at[idx_v], rows_v, sem).wait()
            pltpu.sync_copy(rows_v, out_hbm.at[pl.ds(off, SC_WINDOW)])

    return gather(table, idx)


def _moe_layer(h2, h2_words, w_router, router_bias, w_gate, w_up, w_down, ws_gate, ws_up, ws_down, ln3_g, ln3_b):
    t_all, d = h2.shape
    vec = lambda v: v.astype(F32).reshape(1, -1)
    w_router_t = w_router.astype(F32).T
    bias_col = router_bias.astype(F32).reshape(-1, 1)
    shared = (ws_gate.astype(BF16), ws_up.astype(BF16), ws_down.astype(BF16), vec(ln3_g), vec(ln3_b))
    t = t_all // MOE_TOKEN_PARTS
    n_blocks = t * TOP_K // MOE_BLOCK + N_EXPERTS
    out = None
    for part in range(MOE_TOKEN_PARTS):
        row0 = part * t
        eidx, gate, rank, cnt = _router(h2, w_router_t, bias_col, row0, t)
        dest, bstart, nb = _dispatch(cnt, eidx, rank)
        dest_flat = dest.reshape(TOP_K * t)
        xs = _sc_scatter_rows(h2_words, dest_flat, n_blocks * MOE_BLOCK, row0)
        ys = _experts(bstart.reshape(N_EXPERTS), nb.reshape(N_EXPERTS), xs, w_gate, w_up, w_down)
        yg = _sc_gather_rows(ys, dest_flat).reshape(TOP_K, t, d // 2)
        out = _final(h2, yg, gate.T, *shared, row0, out)
    return out


def kernel(x, mem, ln_in_g, ln_in_b, w_in, conv_w, conv_b, dt_bias_f, dt_bias_b, a_log_f, a_log_b, d_skip, ssd_norm_w, w_out_mix, ln1_g, ln1_b, wq_mem, wk_mem, wv_mem, wo_mem, ln2_g, ln2_b, w_router, router_bias, w_gate, w_up, w_down, ws_gate, ws_up, ws_down, ln3_g, ln3_b):
    l = 0
    h1, (wg_bf16, wu_bf16, wd_bf16) = _mixer_layer(
        x, ln_in_g, ln_in_b, w_in[l], conv_w[l], conv_b[l], dt_bias_f[l], dt_bias_b[l], a_log_f[l], a_log_b[l],
        d_skip[l], ssd_norm_w[l], w_out_mix[l], ln1_g[l], ln1_b[l], cast=(w_gate[l], w_up[l], w_down[l]))
    bsz, s, d = x.shape
    vec = lambda v: v.astype(F32).reshape(1, -1)
    km, vm = _memkv(mem, wk_mem[l].astype(BF16), wv_mem[l].astype(BF16))
    h2, h2_words = _memattn(h1.reshape(bsz, s, d), km, vm, wq_mem[l].astype(BF16), wo_mem[l].astype(BF16),
                            vec(ln2_g[l]), vec(ln2_b[l]))
    out = _moe_layer(h2.reshape(bsz * s, d), h2_words.reshape(bsz * s, d // 2), w_router[l], router_bias[l],
                     wg_bf16, wu_bf16, wd_bf16, ws_gate[l], ws_up[l], ws_down[l], ln3_g[l], ln3_b[l])
    return out.reshape(x.shape)
```

```python
import functools
import math

import numpy as np
import jax
import jax.numpy as jnp
from jax import lax
from jax.experimental import pallas as pl
from jax.experimental.pallas import tpu as pltpu
from jax.experimental.pallas import tpu_sc as plsc

F32 = jnp.float32
BF16 = jnp.bfloat16
I32 = jnp.int32

D_MODEL = 1024
SSD_WIDTH = 1024
SSD_HEAD_DIM = 64
SSD_HEADS = 16
SSD_STATE = 128
SSD_GROUPS = 4
SSD_HEADS_PER_GROUP = SSD_HEADS // SSD_GROUPS
SSD_CONV = 5
SSD_CHUNK = 128
ATT_WIDTH = 1024
ATT_HEAD_DIM = 64
ATT_HEADS = 16
ATT_RADIUS = 64
DILATIONS = (1, 4, 16)
MEM_HEADS = 4
MEM_HEAD_DIM = 256
N_EXPERTS = 256
ROUTE_GROUPS = 8
GROUP_SIZE = N_EXPERTS // ROUTE_GROUPS
TOPK_GROUPS = 4
TOP_K = 8
EXPERT_DIM = 256
ROUTED_SCALE = 2.5
MOE_BLOCK = 256
MOE_TOKEN_PARTS = 2
DN_ALPHA = 2.0 ** 0.25
LN_EPS = 1e-5
RMS_EPS = 1e-5
NEG_INF = -1e30
LOG2_E = math.log2(math.e)
GN = SSD_GROUPS * SSD_STATE
CONV_CH = SSD_WIDTH + 2 * GN
XBC_OFF = SSD_WIDTH
DT_OFF = XBC_OFF + CONV_CH
Q_OFF = DT_OFF + 2 * SSD_HEADS
IN_COLS = Q_OFF + 3 * ATT_WIDTH

LANES = 128
VMEM_LIMIT_BYTES = 56 * 1024 * 1024


def _cparams(*sem):
    return pltpu.CompilerParams(dimension_semantics=sem, vmem_limit_bytes=VMEM_LIMIT_BYTES)


def _ln_rows(x, g, b):
    mu = jnp.mean(x, -1, keepdims=True)
    xc = x - mu
    var = jnp.mean(xc * xc, -1, keepdims=True)
    return xc * lax.rsqrt(var + LN_EPS) * g + b


def _dot(a, b):
    return jnp.dot(a, b, preferred_element_type=F32)


def _dot_nt(a, b):
    return lax.dot_general(a, b, (((1,), (1,)), ((), ())), preferred_element_type=F32)


def _silu(x):
    return x * jax.nn.sigmoid(x)


def _pack_bf16_pairs(x):
    n = x.shape[1] // 2
    u = lax.bitcast_convert_type(x.astype(BF16).astype(F32), jnp.uint32)
    word = (u[:, :n] >> 16) | (u[:, n:] & jnp.uint32(0xFFFF0000))
    return lax.bitcast_convert_type(word, I32)


def _unpack_bf16_pairs(w):
    u = lax.bitcast_convert_type(w, jnp.uint32)
    lo = lax.bitcast_convert_type(u << 16, F32)
    hi = lax.bitcast_convert_type(u & jnp.uint32(0xFFFF0000), F32)
    return jnp.concatenate([lo, hi], axis=1)


def _inproj_kernel(x_ref, g_ref, b_ref, wz_ref, wxbc_ref, wdt_ref, wqkv_ref,
                   h_ref, z_ref, xbc_ref, dt_ref, qkv1_ref, qkv4_ref, qkv16_ref, stage_ref):
    tm = x_ref.shape[1]
    h = _ln_rows(x_ref[0], g_ref[...], b_ref[...])
    h_ref[0] = h
    hb = h.astype(BF16)
    z_ref[0] = _dot(hb, wz_ref[...]).astype(BF16)
    for c in range(CONV_CH // D_MODEL):
        cols = slice(c * D_MODEL, (c + 1) * D_MODEL)
        xbc_ref[0, :, cols] = _dot(hb, wxbc_ref[:, cols]).astype(BF16)
    dt_ref[0] = _dot(hb, wdt_ref[...])
    for c in range(3):
        cols = slice(c * ATT_WIDTH, (c + 1) * ATT_WIDTH)
        res = _dot(hb, wqkv_ref[:, cols])
        if c == 0:
            res = res * (ATT_HEAD_DIM ** -0.5 * LOG2_E)
        qkv1_ref[c, 0] = res.astype(BF16)
        for j in range(ATT_WIDTH // LANES):
            lanes = slice(j * LANES, (j + 1) * LANES)
            stage_ref[j] = res[:, lanes]
            for r in range(4):
                qkv4_ref[c, 0, r, :, lanes] = stage_ref[j, pl.ds(r, tm // 4, stride=4), :].astype(BF16)
            for r in range(16):
                qkv16_ref[c, 0, r, :, lanes] = stage_ref[j, pl.ds(r, tm // 16, stride=16), :].astype(BF16)


def _inproj(x, g, b, wz, wxbc, wdt, wqkv, tm=256):
    bsz, s, d = x.shape
    nt = s // tm
    const = lambda *_: (0, 0)
    wspec = lambda w: pl.BlockSpec(w.shape, const)
    row = lambda n: pl.BlockSpec((1, tm, n), lambda bi, i: (bi, i, 0))
    out_shape = (
        jax.ShapeDtypeStruct((bsz, s, d), F32),
        jax.ShapeDtypeStruct((bsz, s, SSD_WIDTH), BF16),
        jax.ShapeDtypeStruct((bsz, s, CONV_CH), BF16),
        jax.ShapeDtypeStruct((bsz, s, SSD_GROUPS * LANES), F32),
        jax.ShapeDtypeStruct((3, bsz, s, ATT_WIDTH), BF16),
        jax.ShapeDtypeStruct((3, bsz, 4, s // 4, ATT_WIDTH), BF16),
        jax.ShapeDtypeStruct((3, bsz, 16, s // 16, ATT_WIDTH), BF16),
    )
    out_specs = (
        row(d), row(SSD_WIDTH), row(CONV_CH), row(SSD_GROUPS * LANES),
        pl.BlockSpec((3, 1, tm, ATT_WIDTH), lambda bi, i: (0, bi, i, 0)),
        pl.BlockSpec((3, 1, 4, tm // 4, ATT_WIDTH), lambda bi, i: (0, bi, 0, i, 0)),
        pl.BlockSpec((3, 1, 16, tm // 16, ATT_WIDTH), lambda bi, i: (0, bi, 0, i, 0)),
    )
    return pl.pallas_call(
        _inproj_kernel,
        grid=(bsz, nt),
        in_specs=[row(d), wspec(g), wspec(b), wspec(wz), wspec(wxbc), wspec(wdt), wspec(wqkv)],
        out_specs=out_specs,
        out_shape=out_shape,
        scratch_shapes=[pltpu.VMEM((ATT_WIDTH // LANES, tm, LANES), F32)],
        compiler_params=_cparams("parallel", "parallel"),
        name="ln_inproj",
    )(x, g, b, wz, wxbc, wdt, wqkv)


GROUP_X = SSD_HEADS_PER_GROUP * SSD_HEAD_DIM
GROUP_CH = GROUP_X + 2 * SSD_STATE
CONV_ROWS = 256
PAD_ROWS = 8

def _softplus(x):
    return jnp.maximum(x, 0.0) + jnp.log(1.0 + jnp.exp(-jnp.abs(x)))


def _ssd_constants():
    q = SSD_CHUNK
    r = SSD_HEADS_PER_GROUP
    i = np.arange(q)
    tris = [(i[:, None] >= i[None, :]), (i[:, None] <= i[None, :])]
    tri_cat = np.stack([np.concatenate([t, t], axis=1) for t in tris])
    trit_cat = np.stack([np.concatenate([t.T, t.T], axis=0) for t in tris])

    src = np.arange(2 * LANES)[:, None] % LANES
    head = np.arange(GROUP_X)[None, :] // SSD_HEAD_DIM
    sel = np.stack([src == head + off for off in (0, r)])
    as_bf16 = lambda m: jnp.asarray(m.astype(np.float32), dtype=BF16)
    return as_bf16(tri_cat), as_bf16(trit_cat), as_bf16(sel)


def _ssd_kernel(x_ref, b_ref, c_ref, dt_ref, cw_ref, cb_ref, dtb_ref, a_ref, ax_ref, dsk_ref,
                tric_ref, tritc_ref, selx_ref, y_ref, pad_ref, act_ref, yacc_ref, st_ref):
    s = x_ref.shape[1]
    q = SSD_CHUNK
    nchunks = s // q

    zeros = jnp.zeros((PAD_ROWS, GROUP_CH), F32)
    pad_ref[0:PAD_ROWS, :] = zeros
    pad_ref[s + PAD_ROWS:s + 2 * PAD_ROWS, :] = zeros
    pad_ref[PAD_ROWS:s + PAD_ROWS, 0:GROUP_X] = x_ref[0].astype(F32)
    pad_ref[PAD_ROWS:s + PAD_ROWS, GROUP_X:GROUP_X + SSD_STATE] = b_ref[0].astype(F32)
    pad_ref[PAD_ROWS:s + PAD_ROWS, GROUP_X + SSD_STATE:GROUP_CH] = c_ref[0].astype(F32)
    for i in range(s // CONV_ROWS):
        acc = jnp.broadcast_to(cb_ref[0], (CONV_ROWS, GROUP_CH))
        for k in range(SSD_CONV):
            r0 = PAD_ROWS + i * CONV_ROWS + k - SSD_CONV // 2
            acc = acc + cw_ref[0, k:k + 1, :] * pad_ref[r0:r0 + CONV_ROWS, :]
        act_ref[i * CONV_ROWS:(i + 1) * CONV_ROWS, :] = _silu(acc)

    row_i = lax.broadcasted_iota(I32, (q, q), 0)
    col_i = lax.broadcasted_iota(I32, (q, q), 1)
    lane = lax.broadcasted_iota(I32, (q, GROUP_X), 1)
    low_half = lax.broadcasted_iota(I32, (q, LANES), 1) < SSD_HEAD_DIM

    def split(v):
        hi = v.astype(BF16)
        return hi, (v - hi.astype(F32)).astype(BF16)

    st_ref[...] = jnp.zeros_like(st_ref)

    def chunk(ci, reverse):
        di = 1 if reverse else 0
        off = SSD_HEADS_PER_GROUP if reverse else 0
        tri = (row_i <= col_i) if reverse else (row_i >= col_i)
        k = dict(tri_cat=tric_ref[di], trit_cat=tritc_ref[di], sel_x=selx_ref[di])
        c = (nchunks - 1 - ci) if reverse else ci
        rows = pl.ds(c * q, q)
        xa = act_ref[rows, 0:GROUP_X]
        bm = act_ref[rows, GROUP_X:GROUP_X + SSD_STATE]
        cm = act_ref[rows, GROUP_X + SSD_STATE:GROUP_CH]
        dt = _softplus(dt_ref[0, rows, :] + dtb_ref[0])
        dcat = jnp.concatenate(split(dt), axis=1)
        dt_x = _dot(dcat, k["sel_x"])
        cs_x = _dot(k["tri_cat"], jnp.concatenate(split(dt_x * ax_ref[0, di]), axis=0))
        cs_w = []
        for pair in range(GROUP_X // LANES):
            blk = cs_x[:, pair * LANES:(pair + 1) * LANES]
            swapped = pltpu.roll(blk, SSD_HEAD_DIM, axis=1)
            cs_w += [jnp.where(low_half, blk, swapped), jnp.where(low_half, swapped, blk)]
        dta_t = (dt * a_ref[0]).T[off:off + 8, :]
        cs_t = _dot(jnp.concatenate(split(dta_t), axis=1), k["trit_cat"])
        tot = cs_x[0:1, :] if reverse else cs_x[q - 1:q, :]
        cm_b = cm.astype(BF16)
        cbm = _dot_nt(cm_b, bm.astype(BF16))
        xdt = xa * dt_x
        xdt_b = xdt.astype(BF16)
        ydiag = None
        for r in range(SSD_HEADS_PER_GROUP):
            seg = cs_w[r] - cs_t[r:r + 1, :]
            w = (jnp.exp(jnp.where(tri, seg, NEG_INF)) * cbm).astype(BF16)
            yr = _dot(w, xdt_b)
            ydiag = yr if r == 0 else jnp.where(lane >= r * SSD_HEAD_DIM, yr, ydiag)
        xs_b = (xdt * jnp.exp(tot - cs_x)).astype(BF16)
        states = _dot(bm.T.astype(BF16), xs_b)
        hprev = st_ref[di]
        yoff = _dot(cm_b, hprev.astype(BF16)) * jnp.exp(cs_x)
        st_ref[di] = hprev * jnp.exp(tot) + states
        yacc_ref[di, rows, :] = ydiag + yoff

    for ci in range(nchunks):
        chunk(ci, False)
        chunk(ci, True)
    y_ref[0] = (yacc_ref[0] + yacc_ref[1] + act_ref[:, 0:GROUP_X] * dsk_ref[0]).astype(BF16)


def _ssd(xbc, dt, conv_w_g, conv_b_g, dt_bias_g, a_g, dskip_g):
    bsz, s, _ = xbc.shape
    a_lanes, a_x = a_g
    grp = lambda n: pl.BlockSpec((1,) + n, lambda bi, gi: (gi,) + (0,) * len(n))
    in_specs = [
        pl.BlockSpec((1, s, GROUP_X), lambda bi, gi: (bi, 0, gi)),
        pl.BlockSpec((1, s, SSD_STATE), lambda bi, gi: (bi, 0, SSD_WIDTH // SSD_STATE + gi)),
        pl.BlockSpec((1, s, SSD_STATE), lambda bi, gi: (bi, 0, (SSD_WIDTH + GN) // SSD_STATE + gi)),
        pl.BlockSpec((1, s, LANES), lambda bi, gi: (bi, 0, gi)),
        grp((SSD_CONV, GROUP_CH)), grp((1, GROUP_CH)), grp((1, LANES)), grp((1, LANES)),
        grp((2, 1, GROUP_X)), grp((1, GROUP_X)),
    ]
    consts = _ssd_constants()
    in_specs += [pl.BlockSpec(c.shape, lambda bi, gi: (0, 0, 0)) for c in consts]
    return pl.pallas_call(
        _ssd_kernel,
        grid=(bsz, SSD_GROUPS),
        in_specs=in_specs,
        out_specs=pl.BlockSpec((1, s, GROUP_X), lambda bi, gi: (bi, 0, gi)),
        out_shape=jax.ShapeDtypeStruct((bsz, s, SSD_WIDTH), BF16),
        scratch_shapes=[
            pltpu.VMEM((s + 2 * PAD_ROWS, GROUP_CH), F32),
            pltpu.VMEM((s, GROUP_CH), F32),
            pltpu.VMEM((2, s, GROUP_X), F32),
            pltpu.VMEM((2, SSD_STATE, GROUP_X), F32),
        ],
        compiler_params=_cparams("parallel", "parallel"),
        name="ssd_scan",
    )(xbc, xbc, xbc, dt, conv_w_g, conv_b_g, dt_bias_g, a_lanes, a_x, dskip_g, *consts)


ATT_QBLOCK = 128
ATT_PAIR = 2 * ATT_HEAD_DIM
ATT_WINDOW = 2 * ATT_QBLOCK


def _att_kernel(n_cast, q1, k1, v1, q4, k4, v4, q16, k16, v16, slope_ref, *rest):
    cast_in, (o_ref, *cast_out), (km_ref, vm_ref, bias_ref, ob_ref, lb_ref) = (
        rest[:n_cast], rest[n_cast:2 * n_cast + 1], rest[2 * n_cast + 1:])
    for src, dst in zip(cast_in, cast_out):
        dst[...] = src[...].astype(BF16)
    s = o_ref.shape[1]
    qb = ATT_QBLOCK
    first = lax.broadcasted_iota(I32, (qb, ATT_PAIR), 1) < ATT_HEAD_DIM
    first_all = lax.broadcasted_iota(I32, (s, ATT_PAIR), 1) < ATT_HEAD_DIM
    slope_lanes = slope_ref[0]
    ind = first_all.astype(F32).astype(BF16)

    def branch(bi, dil, q_ref, k_ref, v_ref):
        seg_len = s // dil
        win = min(ATT_WINDOW, seg_len)
        kk = k_ref[0, 0]
        vv = v_ref[0, 0]
        zero = jnp.zeros_like(kk)
        km_ref[0] = jnp.where(first_all, kk, zero)
        km_ref[1] = jnp.where(first_all, zero, kk)
        vm_ref[0] = jnp.concatenate([jnp.where(first_all, vv, zero), ind], axis=1)
        vm_ref[1] = jnp.concatenate([jnp.where(first_all, zero, vv), 1 - ind], axis=1)
        shifts = sorted({min(max(w0 - ATT_RADIUS, 0), seg_len - win) - w0 for w0 in range(0, seg_len, qb)},
                        reverse=True)
        rel0 = lax.broadcasted_iota(I32, (qb, win), 1) - lax.broadcasted_iota(I32, (qb, win), 0)
        for vi, shift in enumerate(shifts):
            rel = jnp.abs(rel0 + shift)
            dist = (dil * rel).astype(F32)
            for j in range(2):
                slope = slope_lanes[:, j * ATT_HEAD_DIM:j * ATT_HEAD_DIM + 1]
                bias_ref[vi, :, j * win:(j + 1) * win] = jnp.where(rel <= ATT_RADIUS, -(slope * LOG2_E) * dist, NEG_INF)

        for blk in range(s // qb):
            qs = blk * qb
            seg, within = divmod(qs, seg_len)
            kw = min(max(within - ATT_RADIUS, 0), seg_len - win)
            ks = seg * seg_len + kw
            vi = shifts.index(kw - within)
            kcat = jnp.concatenate([km_ref[0, ks:ks + win, :], km_ref[1, ks:ks + win, :]], axis=0)
            vcat = jnp.concatenate([vm_ref[0, ks:ks + win, :], vm_ref[1, ks:ks + win, :]], axis=0)
            sc = _dot_nt(q_ref[0, 0, qs:qs + qb, :], kcat) + bias_ref[vi, :, 0:2 * win]
            s0, s1 = sc[:, :win], sc[:, win:]
            m0 = jnp.max(s0, axis=-1, keepdims=True)
            m1 = jnp.max(s1, axis=-1, keepdims=True)
            p0 = jnp.exp2(s0 - m0)
            p1 = jnp.exp2(s1 - m1)
            ol = _dot(jnp.concatenate([p0, p1], axis=1).astype(BF16), vcat)
            l = ol[:, ATT_PAIR:]
            o_blk = ol[:, :ATT_PAIR] / l
            l_blk = jnp.where(first, m0, m1) + jnp.log2(l)
            if dil == 1:
                dst = pl.ds(qs, qb)
            else:
                dst = pl.ds(within * dil + seg, qb, stride=dil)
            ob_ref[bi, dst, :] = o_blk
            lb_ref[bi, dst, :] = l_blk

    branch(0, DILATIONS[0], q1, k1, v1)
    branch(1, DILATIONS[1], q4, k4, v4)
    branch(2, DILATIONS[2], q16, k16, v16)

    rows = 256
    for i in range(s // rows):
        sl = slice(i * rows, (i + 1) * rows)
        l0, l1, l2 = lb_ref[0, sl, :], lb_ref[1, sl, :], lb_ref[2, sl, :]
        m = jnp.maximum(jnp.maximum(l0, l1), l2)
        w0, w1, w2 = jnp.exp2(l0 - m), jnp.exp2(l1 - m), jnp.exp2(l2 - m)
        num = w0 * ob_ref[0, sl, :] + w1 * ob_ref[1, sl, :] + w2 * ob_ref[2, sl, :]
        o_ref[0, sl, :] = (num / (w0 + w1 + w2)).astype(BF16)


def _dilated_attention(qkv1, qkv4, qkv16, slopes, cast=()):
    _, bsz, s, _ = qkv1.shape
    npairs = ATT_WIDTH // ATT_PAIR
    spec = lambda c: pl.BlockSpec((1, 1, s, ATT_PAIR), lambda bi, hp, c=c: (c, bi, 0, hp))
    in_specs = [spec(0), spec(1), spec(2)] * 3 + [pl.BlockSpec((1, 1, ATT_PAIR), lambda bi, hp: (hp, 0, 0))]
    steps = bsz * npairs
    cast_specs = [pl.BlockSpec((w.shape[0] // steps,) + w.shape[1:], lambda bi, hp: (bi * npairs + hp, 0, 0))
                  for w in cast]
    out = pl.pallas_call(
        functools.partial(_att_kernel, len(cast)),
        grid=(bsz, npairs),
        in_specs=in_specs + cast_specs,
        out_specs=[pl.BlockSpec((1, s, ATT_PAIR), lambda bi, hp: (bi, 0, hp))] + cast_specs,
        out_shape=[jax.ShapeDtypeStruct((bsz, s, ATT_WIDTH), BF16)]
        + [jax.ShapeDtypeStruct(w.shape, BF16) for w in cast],
        scratch_shapes=[pltpu.VMEM((2, s, ATT_PAIR), BF16), pltpu.VMEM((2, s, 2 * ATT_PAIR), BF16),
                        pltpu.VMEM((3, ATT_QBLOCK, 2 * ATT_WINDOW), F32),
                        pltpu.VMEM((3, s, ATT_PAIR), F32), pltpu.VMEM((3, s, ATT_PAIR), F32)],
        compiler_params=_cparams("parallel", "parallel"),
        name="dilated_attention",
    )(qkv1, qkv1, qkv1, qkv4, qkv4, qkv4, qkv16, qkv16, qkv16, slopes, *cast)
    return out[0], tuple(out[1:])


def _outproj_kernel(h_ref, yssd_ref, z_ref, yatt_ref, nw_ref, wo_ref, g_ref, b_ref, o_ref):
    y = yssd_ref[...].astype(F32) * _silu(z_ref[...].astype(F32))
    yn = y * lax.rsqrt(jnp.mean(y * y, -1, keepdims=True) + RMS_EPS) * nw_ref[...]
    mix = _dot(yn.astype(BF16), wo_ref[0:SSD_WIDTH, :]) + _dot(yatt_ref[...], wo_ref[SSD_WIDTH:, :])
    o_ref[...] = _ln_rows(DN_ALPHA * h_ref[...] + mix, g_ref[...], b_ref[...])


def _outproj(h, yssd, z, yatt, norm_w, w_out, g, b, tm=512):
    t, d = h.shape
    const = lambda *_: (0, 0)
    wspec = lambda w: pl.BlockSpec(w.shape, const)
    row = lambda n: pl.BlockSpec((tm, n), lambda i: (i, 0))
    return pl.pallas_call(
        _outproj_kernel,
        grid=(t // tm,),
        in_specs=[row(d), row(SSD_WIDTH), row(SSD_WIDTH), row(ATT_WIDTH),
                  wspec(norm_w), wspec(w_out), wspec(g), wspec(b)],
        out_specs=row(d),
        out_shape=jax.ShapeDtypeStruct((t, d), F32),
        compiler_params=_cparams("parallel"),
        name="outproj_ln",
    )(h, yssd, z, yatt, norm_w, w_out, g, b)


def _alibi_slopes():
    sl = np.array([2.0 ** (-8.0 * (h + 1) / ATT_HEADS) for h in range(ATT_HEADS)], dtype=np.float32)
    return jnp.asarray(np.repeat(sl, ATT_HEAD_DIM).reshape(ATT_WIDTH // ATT_PAIR, 1, ATT_PAIR))


def _mixer_params(w_in, conv_w, conv_b, dt_bias_f, dt_bias_b, a_log_f, a_log_b, d_skip):
    r = SSD_HEADS_PER_GROUP
    wz = w_in[:, 0:XBC_OFF].astype(BF16)
    wxbc = w_in[:, XBC_OFF:DT_OFF].astype(BF16)
    wqkv = w_in[:, Q_OFF:IN_COLS].astype(BF16)
    wdt_f = w_in[:, DT_OFF:DT_OFF + SSD_HEADS].reshape(D_MODEL, SSD_GROUPS, r)
    wdt_b = w_in[:, DT_OFF + SSD_HEADS:Q_OFF].reshape(D_MODEL, SSD_GROUPS, r)
    lane_pad = ((0, 0), (0, 0), (0, LANES - 2 * r))
    wdt = jnp.pad(jnp.concatenate([wdt_f, wdt_b], -1), lane_pad).reshape(D_MODEL, SSD_GROUPS * LANES).astype(BF16)

    def per_group_lanes(f, b):
        v = jnp.concatenate([f.reshape(SSD_GROUPS, 1, r), b.reshape(SSD_GROUPS, 1, r)], -1)
        return jnp.pad(v.astype(F32), lane_pad)

    dtb = per_group_lanes(dt_bias_f, dt_bias_b)
    a_f = -jnp.exp(a_log_f.astype(F32))
    a_b = -jnp.exp(a_log_b.astype(F32))

    def per_head_lanes(n):
        v = jnp.stack([a_f.reshape(SSD_GROUPS, r), a_b.reshape(SSD_GROUPS, r)], axis=1)
        return jnp.repeat(v, n, axis=-1).reshape(SSD_GROUPS, 2, 1, r * n)

    a = (per_group_lanes(a_f, a_b), per_head_lanes(SSD_HEAD_DIM))

    def group_cols(w):
        xs = w[..., 0:SSD_WIDTH].reshape(w.shape[:-1] + (SSD_GROUPS, GROUP_X))
        bs = w[..., SSD_WIDTH:SSD_WIDTH + GN].reshape(w.shape[:-1] + (SSD_GROUPS, SSD_STATE))
        cs = w[..., SSD_WIDTH + GN:].reshape(w.shape[:-1] + (SSD_GROUPS, SSD_STATE))
        return jnp.moveaxis(jnp.concatenate([xs, bs, cs], -1), -2, 0)

    cw = group_cols(conv_w.astype(F32))
    cb = group_cols(conv_b.astype(F32)[None])
    dsk = jnp.repeat(d_skip.astype(F32), SSD_HEAD_DIM).reshape(SSD_GROUPS, 1, GROUP_X)
    return wz, wxbc, wdt, wqkv, cw, cb, dtb, a, dsk


def _mixer_layer(x, ln_in_g, ln_in_b, w_in, conv_w, conv_b, dt_bias_f, dt_bias_b, a_log_f, a_log_b,
                 d_skip, ssd_norm_w, w_out_mix, ln1_g, ln1_b, cast=()):
    bsz, s, d = x.shape
    vec = lambda v: v.astype(F32).reshape(1, -1)
    wz, wxbc, wdt, wqkv, cw, cb, dtb, a, dsk = _mixer_params(
        w_in, conv_w, conv_b, dt_bias_f, dt_bias_b, a_log_f, a_log_b, d_skip)
    h, z, xbc, dt, qkv1, qkv4, qkv16 = _inproj(x, vec(ln_in_g), vec(ln_in_b), wz, wxbc, wdt, wqkv)
    yssd = _ssd(xbc, dt, cw, cb, dtb, a, dsk)
    yatt, cast_bf16 = _dilated_attention(qkv1, qkv4.reshape(qkv1.shape), qkv16.reshape(qkv1.shape),
                                         _alibi_slopes(), cast)
    t = bsz * s
    h1 = _outproj(h.reshape(t, d), yssd.reshape(t, SSD_WIDTH), z.reshape(t, SSD_WIDTH),
                  yatt.reshape(t, ATT_WIDTH), vec(ssd_norm_w), w_out_mix.astype(BF16), vec(ln1_g), vec(ln1_b))
    return h1, cast_bf16


def _memkv_kernel(mem_ref, wk_ref, wv_ref, k_ref, v_ref):
    m = mem_ref[0].astype(BF16)
    k_ref[0] = _dot(m, wk_ref[...]).astype(BF16)
    v_ref[0] = _dot(m, wv_ref[...]).astype(BF16)


def _memkv(mem, wk, wv):
    bsz, m, d = mem.shape
    const = lambda *_: (0, 0)
    blk = pl.BlockSpec((1, m, d), lambda bi: (bi, 0, 0))
    return pl.pallas_call(
        _memkv_kernel,
        grid=(bsz,),
        in_specs=[blk, pl.BlockSpec(wk.shape, const), pl.BlockSpec(wv.shape, const)],
        out_specs=(blk, blk),
        out_shape=(jax.ShapeDtypeStruct((bsz, m, d), BF16),) * 2,
        compiler_params=_cparams("parallel"),
        name="mem_kv",
    )(mem, wk, wv)


def _memattn_kernel(h_ref, k_ref, v_ref, wq_ref, wo_ref, g_ref, b_ref, o_ref, ow_ref):
    h = h_ref[0]
    q = (_dot(h.astype(BF16), wq_ref[...]) * (MEM_HEAD_DIM ** -0.5)).astype(BF16)
    xa = None
    for hd in range(MEM_HEADS):
        cols = slice(hd * MEM_HEAD_DIM, (hd + 1) * MEM_HEAD_DIM)
        sc = _dot_nt(q[:, cols], k_ref[0, :, cols])
        p = jnp.exp(sc - jnp.max(sc, axis=-1, keepdims=True))
        o = _dot(p.astype(BF16), v_ref[0, :, cols]) / jnp.sum(p, axis=-1, keepdims=True)
        part = _dot(o.astype(BF16), wo_ref[cols, :])
        xa = part if xa is None else xa + part
    out = _ln_rows(DN_ALPHA * h + xa, g_ref[...], b_ref[...])
    o_ref[0] = out
    ow_ref[0] = _pack_bf16_pairs(out)


def _memattn(h, k, v, wq, wo, g, b, tm=512):
    bsz, s, d = h.shape
    m = k.shape[1]
    const = lambda *_: (0, 0)
    wspec = lambda w: pl.BlockSpec(w.shape, const)
    row = pl.BlockSpec((1, tm, d), lambda bi, i: (bi, i, 0))
    kv = pl.BlockSpec((1, m, d), lambda bi, i: (bi, 0, 0))
    return pl.pallas_call(
        _memattn_kernel,
        grid=(bsz, s // tm),
        in_specs=[row, kv, kv, wspec(wq), wspec(wo), wspec(g), wspec(b)],
        out_specs=(row, pl.BlockSpec((1, tm, d // 2), lambda bi, i: (bi, i, 0))),
        out_shape=(jax.ShapeDtypeStruct((bsz, s, d), F32), jax.ShapeDtypeStruct((bsz, s, d // 2), I32)),
        compiler_params=_cparams("parallel", "parallel"),
        name="mem_attn_ln",
    )(h, k, v, wq, wo, g, b)


def _first_max(work, idx_iota, sentinel):
    m = jnp.max(work, axis=0, keepdims=True)
    idx = jnp.min(jnp.where(work == m, idx_iota, sentinel), axis=0, keepdims=True)
    return m, idx


def _router_kernel(h_ref, wrt_ref, bias_ref, eidx_ref, gate_ref, rank_ref, cnt_ref, run_ref):
    tm = h_ref.shape[0]
    e = N_EXPERTS

    @pl.when(pl.program_id(0) == 0)
    def _():
        run_ref[...] = jnp.zeros_like(run_ref)

    logits = lax.dot_general(wrt_ref[...], h_ref[...], (((1,), (1,)), ((), ())),
                             precision=lax.Precision.HIGHEST, preferred_element_type=F32)
    scores = jax.nn.sigmoid(logits)
    choice = scores + bias_ref[...]
    row = lax.broadcasted_iota(I32, (e, tm), 0)
    grow = lax.broadcasted_iota(I32, (GROUP_SIZE, tm), 0)
    gs = []
    for g in range(ROUTE_GROUPS):
        blk = choice[g * GROUP_SIZE:(g + 1) * GROUP_SIZE, :]
        m1, i1 = _first_max(blk, grow, GROUP_SIZE)
        m2 = jnp.max(jnp.where(grow == i1, -jnp.inf, blk), axis=0, keepdims=True)
        gs.append(m1 + m2)
    work = jnp.concatenate(gs, axis=0)
    giota = lax.broadcasted_iota(I32, (ROUTE_GROUPS, tm), 0)
    gmask = jnp.zeros((ROUTE_GROUPS, tm), jnp.bool_)
    for _ in range(TOPK_GROUPS):
        _, gi = _first_max(work, giota, ROUTE_GROUPS)
        hit = giota == gi
        gmask = jnp.logical_or(gmask, hit)
        work = jnp.where(hit, -jnp.inf, work)
    work = jnp.concatenate(
        [jnp.where(gmask[g:g + 1, :], choice[g * GROUP_SIZE:(g + 1) * GROUP_SIZE, :], NEG_INF)
         for g in range(ROUTE_GROUPS)], axis=0)
    chosen = jnp.zeros((e, tm), jnp.bool_)
    idxs, svals = [], []
    for _ in range(TOP_K):
        _, ei = _first_max(work, row, e)
        hit = row == ei
        svals.append(jnp.sum(jnp.where(hit, scores, 0.0), axis=0, keepdims=True))
        idxs.append(ei)
        chosen = jnp.logical_or(chosen, hit)
        work = jnp.where(hit, -jnp.inf, work)
    eidx = jnp.concatenate(idxs, axis=0)
    sv = jnp.concatenate(svals, axis=0)
    eidx_ref[...] = eidx
    gate_ref[...] = sv / jnp.sum(sv, axis=0, keepdims=True) * ROUTED_SCALE
    upper = (lax.broadcasted_iota(I32, (tm, tm), 0) < lax.broadcasted_iota(I32, (tm, tm), 1))
    mt = chosen.astype(F32)
    cum = _dot(mt.astype(BF16), upper.astype(F32).astype(BF16)) + run_ref[...]
    rank_ref[...] = jnp.concatenate(
        [jnp.sum(jnp.where(row == idxs[k], cum, 0.0), axis=0, keepdims=True) for k in range(TOP_K)],
        axis=0).astype(I32)
    run_ref[...] += jnp.sum(mt, axis=1, keepdims=True)
    cnt_ref[...] = run_ref[...]


def _router(h, w_router_t, bias_col, row0, t, tm=512):
    d = h.shape[1]
    const = lambda *_: (0, 0)
    col = lambda dt: pl.BlockSpec((TOP_K, tm), lambda i: (0, i))
    return pl.pallas_call(
        _router_kernel,
        grid=(t // tm,),
        in_specs=[pl.BlockSpec((tm, d), lambda i: (i + row0 // tm, 0)), pl.BlockSpec(w_router_t.shape, const),
                  pl.BlockSpec(bias_col.shape, const)],
        out_specs=(col(I32), col(F32), col(I32), pl.BlockSpec((N_EXPERTS, 1), const)),
        out_shape=(jax.ShapeDtypeStruct((TOP_K, t), I32), jax.ShapeDtypeStruct((TOP_K, t), F32),
                   jax.ShapeDtypeStruct((TOP_K, t), I32), jax.ShapeDtypeStruct((N_EXPERTS, 1), F32)),
        scratch_shapes=[pltpu.VMEM((N_EXPERTS, 1), F32)],
        compiler_params=_cparams("arbitrary"),
        name="router_topk",
    )(h, w_router_t, bias_col)


def _dispatch_kernel(cnt_ref, eidx_ref, rank_ref, dest_ref, bstart_ref, nb_ref):
    tm = eidx_ref.shape[1]
    e = N_EXPERTS
    nb = jnp.floor((cnt_ref[...] + (MOE_BLOCK - 1)) * (1.0 / MOE_BLOCK))
    strict = (lax.broadcasted_iota(I32, (e, e), 1) < lax.broadcasted_iota(I32, (e, e), 0))
    bstart = _dot(strict.astype(F32).astype(BF16), jnp.broadcast_to(nb, (e, LANES)).astype(BF16))[:, 0:1]
    pstart = bstart * MOE_BLOCK
    row = lax.broadcasted_iota(I32, (e, tm), 0)
    eidx = eidx_ref[...]
    rank = rank_ref[...]
    dest_ref[...] = jnp.concatenate(
        [jnp.sum(jnp.where(row == eidx[k:k + 1, :], pstart, 0.0), axis=0, keepdims=True).astype(I32)
         + rank[k:k + 1, :] for k in range(TOP_K)], axis=0)
    bstart_ref[...] = bstart.astype(I32)
    nb_ref[...] = nb.astype(I32)


def _dispatch(cnt, eidx, rank, tm=2048):
    t = eidx.shape[1]
    const = lambda *_: (0, 0)
    col = pl.BlockSpec((TOP_K, tm), lambda i: (0, i))
    per_expert = pl.BlockSpec((N_EXPERTS, 1), const)
    return pl.pallas_call(
        _dispatch_kernel,
        grid=(t // tm,),
        in_specs=[per_expert, col, col],
        out_specs=(col, per_expert, per_expert),
        out_shape=(jax.ShapeDtypeStruct((TOP_K, t), I32), jax.ShapeDtypeStruct((N_EXPERTS, 1), I32),
                   jax.ShapeDtypeStruct((N_EXPERTS, 1), I32)),
        compiler_params=_cparams("arbitrary"),
        name="dispatch_slots",
    )(cnt, eidx, rank)


EXPERT_ROW_BUFFERS = 4


def _experts_kernel(bstart_ref, nb_ref, xs_hbm, wg_ref, wu_ref, wd_ref, ys_hbm,
                    xbuf, ybuf, in_sem, out_sem):
    e = pl.program_id(0)
    last_e = pl.num_programs(0) - 1
    first_block = bstart_ref[e]
    nused = bstart_ref[last_e] + nb_ref[last_e]
    nbuf = xbuf.shape[0]

    def block_rows(blk):
        return pl.ds(pl.multiple_of(blk * MOE_BLOCK, MOE_BLOCK), MOE_BLOCK)

    def in_copy(blk, slot):
        return pltpu.make_async_copy(xs_hbm.at[block_rows(blk)], xbuf.at[slot], in_sem.at[slot])

    def out_copy(blk, slot):
        return pltpu.make_async_copy(ybuf.at[slot], ys_hbm.at[block_rows(blk)], out_sem.at[slot])

    @pl.when(e == 0)
    def _():
        for ahead in range(nbuf - 1):
            @pl.when(ahead < nused)
            def _():
                in_copy(ahead, ahead).start()

    def body(j, carry):
        blk = first_block + j
        slot = blk % nbuf
        in_copy(blk, slot).wait()
        nxt = blk + nbuf - 1

        @pl.when(nxt < nused)
        def _():
            in_copy(nxt, nxt % nbuf).start()

        @pl.when(blk >= nbuf)
        def _():
            out_copy(blk - nbuf, slot).wait()

        x = _unpack_bf16_pairs(xbuf[slot]).astype(BF16)
        g = _dot(x, wg_ref[0])
        u = _dot(x, wu_ref[0])
        a = (_silu(g) * u).astype(BF16)
        ybuf[slot] = _pack_bf16_pairs(_dot(a, wd_ref[0]))
        out_copy(blk, slot).start()
        return carry

    lax.fori_loop(0, nb_ref[e], body, 0)

    @pl.when(e == last_e)
    def _():
        for back in range(1, nbuf + 1):
            @pl.when(nused >= back)
            def _():
                out_copy(nused - back, (nused - back) % nbuf).wait()


def _experts(bstart, nb, xs, w_gate, w_up, w_down):
    n_pad, words = xs.shape
    n_exp, d, f = w_gate.shape
    wspec = lambda w: pl.BlockSpec((1,) + w.shape[1:], lambda e, bs, nb: (e, 0, 0))
    hbm = pl.BlockSpec(memory_space=pl.ANY)
    return pl.pallas_call(
        _experts_kernel,
        grid_spec=pltpu.PrefetchScalarGridSpec(
            num_scalar_prefetch=2,
            grid=(n_exp,),
            in_specs=[hbm, wspec(w_gate), wspec(w_up), wspec(w_down)],
            out_specs=hbm,
            scratch_shapes=[
                pltpu.VMEM((EXPERT_ROW_BUFFERS, MOE_BLOCK, words), I32),
                pltpu.VMEM((EXPERT_ROW_BUFFERS, MOE_BLOCK, words), I32),
                pltpu.SemaphoreType.DMA((EXPERT_ROW_BUFFERS,)), pltpu.SemaphoreType.DMA((EXPERT_ROW_BUFFERS,)),
            ],
        ),
        out_shape=jax.ShapeDtypeStruct((n_pad, words), I32),
        compiler_params=_cparams("arbitrary"),
        name="routed_experts",
    )(bstart, nb, xs, w_gate, w_up, w_down)


def _final_kernel(h_ref, yg_ref, gate_ref, wsg_ref, wsu_ref, wsd_ref, g_ref, b_ref, o_ref):
    h = h_ref[...]
    d = h.shape[1]
    hb = h.astype(BF16)
    ff = _dot((_silu(_dot(hb, wsg_ref[...])) * _dot(hb, wsu_ref[...])).astype(BF16), wsd_ref[...])
    gate = gate_ref[...]
    for k in range(TOP_K):
        ff = ff + gate[:, k:k + 1] * _unpack_bf16_pairs(yg_ref[k])
    o_ref[...] = _ln_rows(DN_ALPHA * h + ff, g_ref[...], b_ref[...])


def _final_into_kernel(prev_ref, *refs):
    del prev_ref
    _final_kernel(*refs)


def _final(h, yg, gate, wsg, wsu, wsd, g, b, row0, out_prev=None, tm=256):
    t_all, d = h.shape
    t = yg.shape[1]
    const = lambda *_: (0, 0)
    wspec = lambda w: pl.BlockSpec(w.shape, const)
    part = lambda n: pl.BlockSpec((tm, n), lambda i: (i, 0))
    full = pl.BlockSpec((tm, d), lambda i: (i + row0 // tm, 0))
    in_specs = [full, pl.BlockSpec((TOP_K, tm, d // 2), lambda i: (0, i, 0)), part(TOP_K),
                wspec(wsg), wspec(wsu), wspec(wsd), wspec(g), wspec(b)]
    args = (h, yg, gate, wsg, wsu, wsd, g, b)
    body, aliases = _final_kernel, {}
    if out_prev is not None:
        body, aliases = _final_into_kernel, {0: 0}
        in_specs = [pl.BlockSpec(memory_space=pl.ANY)] + in_specs
        args = (out_prev,) + args
    return pl.pallas_call(
        body,
        grid=(t // tm,),
        in_specs=in_specs,
        out_specs=full,
        out_shape=jax.ShapeDtypeStruct((t_all, d), F32),
        input_output_aliases=aliases,
        compiler_params=_cparams("parallel"),
        name="shared_combine_ln",
    )(*args)


SC_CORES = 2
SC_SUBCORES = 16
SC_WORKERS = SC_CORES * SC_SUBCORES
SC_WINDOW = 64


def _sc_scatter_rows(x_words, dest_kmajor, n_rows, row0=0):
    w = x_words.shape[1]
    t = dest_kmajor.shape[0] // TOP_K
    per_worker = t // SC_WORKERS
    nchunk = per_worker // SC_WINDOW
    mesh = plsc.VectorSubcoreMesh(core_axis_name="c", subcore_axis_name="s")

    @functools.partial(
        pl.kernel, mesh=mesh,
        out_type=jax.ShapeDtypeStruct((n_rows, w), I32),
        scratch_types=[pltpu.VMEM((SC_WINDOW,), I32), pltpu.VMEM((SC_WINDOW, w), I32), pltpu.SemaphoreType.DMA],
    )
    def scatter(x_hbm, dest_hbm, out_hbm, idx_v, rows_v, sem):
        wid = lax.axis_index("s") * SC_CORES + lax.axis_index("c")
        base = wid * per_worker

        @pl.loop(0, nchunk)
        def _(i):
            t0 = base + i * SC_WINDOW
            pltpu.sync_copy(x_hbm.at[pl.ds(row0 + t0, SC_WINDOW)], rows_v)
            for k in range(TOP_K):
                pltpu.sync_copy(dest_hbm.at[pl.ds(k * t + t0, SC_WINDOW)], idx_v)
                pltpu.async_copy(rows_v, out_hbm.at[idx_v], sem).wait()

    return scatter(x_words, dest_kmajor)


def _sc_gather_rows(table, idx):
    a = idx.shape[0]
    w = table.shape[1]
    per_worker = a // SC_WORKERS
    nchunk = per_worker // SC_WINDOW
    mesh = plsc.VectorSubcoreMesh(core_axis_name="c", subcore_axis_name="s")

    @functools.partial(
        pl.kernel, mesh=mesh,
        out_type=jax.ShapeDtypeStruct((a, w), I32),
        scratch_types=[pltpu.VMEM((SC_WINDOW,), I32), pltpu.VMEM((SC_WINDOW, w), I32), pltpu.SemaphoreType.DMA],
    )
    def gather(table_hbm, idx_hbm, out_hbm, idx_v, rows_v, sem):
        wid = lax.axis_index("s") * SC_CORES + lax.axis_index("c")
        base = wid * per_worker

        @pl.loop(0, nchunk)
        def _(i):
            off = base + i * SC_WINDOW
            pltpu.sync_copy(idx_hbm.at[pl.ds(off, SC_WINDOW)], idx_v)
            pltpu.async_copy(table_hbm.at[idx_v], rows_v, sem).wait()
            pltpu.sync_copy(rows_v, out_hbm.at[pl.ds(off, SC_WINDOW)])

    return gather(table, idx)


def _moe_layer(h2, h2_words, w_router, router_bias, w_gate, w_up, w_down, ws_gate, ws_up, ws_down, ln3_g, ln3_b):
    t_all, d = h2.shape
    vec = lambda v: v.astype(F32).reshape(1, -1)
    w_router_t = w_router.astype(F32).T
    bias_col = router_bias.astype(F32).reshape(-1, 1)
    shared = (ws_gate.astype(BF16), ws_up.astype(BF16), ws_down.astype(BF16), vec(ln3_g), vec(ln3_b))
    t = t_all // MOE_TOKEN_PARTS
    n_blocks = t * TOP_K // MOE_BLOCK + N_EXPERTS
    out = None
    for part in range(MOE_TOKEN_PARTS):
        row0 = part * t
        eidx, gate, rank, cnt = _router(h2, w_router_t, bias_col, row0, t)
        dest, bstart, nb = _dispatch(cnt, eidx, rank)
        dest_flat = dest.reshape(TOP_K * t)
        xs = _sc_scatter_rows(h2_words, dest_flat, n_blocks * MOE_BLOCK, row0)
        ys = _experts(bstart.reshape(N_EXPERTS), nb.reshape(N_EXPERTS), xs, w_gate, w_up, w_down)
        yg = _sc_gather_rows(ys, dest_flat).reshape(TOP_K, t, d // 2)
        out = _final(h2, yg, gate.T, *shared, row0, out)
    return out


def kernel(x, mem, ln_in_g, ln_in_b, w_in, conv_w, conv_b, dt_bias_f, dt_bias_b, a_log_f, a_log_b, d_skip, ssd_norm_w, w_out_mix, ln1_g, ln1_b, wq_mem, wk_mem, wv_mem, wo_mem, ln2_g, ln2_b, w_router, router_bias, w_gate, w_up, w_down, ws_gate, ws_up, ws_down, ln3_g, ln3_b):
    l = 0
    h1, (wg_bf16, wu_bf16, wd_bf16) = _mixer_layer(
        x, ln_in_g, ln_in_b, w_in[l], conv_w[l], conv_b[l], dt_bias_f[l], dt_bias_b[l], a_log_f[l], a_log_b[l],
        d_skip[l], ssd_norm_w[l], w_out_mix[l], ln1_g[l], ln1_b[l], cast=(w_gate[l], w_up[l], w_down[l]))
    bsz, s, d = x.shape
    vec = lambda v: v.astype(F32).reshape(1, -1)
    km, vm = _memkv(mem, wk_mem[l].astype(BF16), wv_mem[l].astype(BF16))
    h2, h2_words = _memattn(h1.reshape(bsz, s, d), km, vm, wq_mem[l].astype(BF16), wo_mem[l].astype(BF16),
                            vec(ln2_g[l]), vec(ln2_b[l]))
    out = _moe_layer(h2.reshape(bsz * s, d), h2_words.reshape(bsz * s, d // 2), w_router[l], router_bias[l],
                     wg_bf16, wu_bf16, wd_bf16, ws_gate[l], ws_up[l], ws_down[l], ln3_g[l], ln3_b[l])
    return out.reshape(x.shape)
```

```python
import functools
import math

import numpy as np
import jax
import jax.numpy as jnp
from jax import lax
from jax.experimental import pallas as pl
from jax.experimental.pallas import tpu as pltpu
from jax.experimental.pallas import tpu_sc as plsc

F32 = jnp.float32
BF16 = jnp.bfloat16
I32 = jnp.int32

D_MODEL = 1024
SSD_WIDTH = 1024
SSD_HEAD_DIM = 64
SSD_HEADS = 16
SSD_STATE = 128
SSD_GROUPS = 4
SSD_HEADS_PER_GROUP = SSD_HEADS // SSD_GROUPS
SSD_CONV = 5
SSD_CHUNK = 128
ATT_WIDTH = 1024
ATT_HEAD_DIM = 64
ATT_HEADS = 16
ATT_RADIUS = 64
DILATIONS = (1, 4, 16)
MEM_HEADS = 4
MEM_HEAD_DIM = 256
N_EXPERTS = 256
ROUTE_GROUPS = 8
GROUP_SIZE = N_EXPERTS // ROUTE_GROUPS
TOPK_GROUPS = 4
TOP_K = 8
EXPERT_DIM = 256
ROUTED_SCALE = 2.5
MOE_BLOCK = 256
MOE_TOKEN_PARTS = 2
DN_ALPHA = 2.0 ** 0.25
LN_EPS = 1e-5
RMS_EPS = 1e-5
NEG_INF = -1e30
LOG2_E = math.log2(math.e)
GN = SSD_GROUPS * SSD_STATE
CONV_CH = SSD_WIDTH + 2 * GN
XBC_OFF = SSD_WIDTH
DT_OFF = XBC_OFF + CONV_CH
Q_OFF = DT_OFF + 2 * SSD_HEADS
IN_COLS = Q_OFF + 3 * ATT_WIDTH

LANES = 128
INPROJ_COLS = 512
VMEM_LIMIT_BYTES = 56 * 1024 * 1024


def _cparams(*sem):
    return pltpu.CompilerParams(dimension_semantics=sem, vmem_limit_bytes=VMEM_LIMIT_BYTES)


def _ln_rows(x, g, b):
    mu = jnp.mean(x, -1, keepdims=True)
    xc = x - mu
    var = jnp.mean(xc * xc, -1, keepdims=True)
    return xc * lax.rsqrt(var + LN_EPS) * g + b


def _dot(a, b):
    return jnp.dot(a, b, preferred_element_type=F32)


def _dot_nt(a, b):
    return lax.dot_general(a, b, (((1,), (1,)), ((), ())), preferred_element_type=F32)


def _silu(x):
    return x * jax.nn.sigmoid(x)


def _pack_bf16_pairs(x):
    n = x.shape[1] // 2
    u = lax.bitcast_convert_type(x.astype(BF16).astype(F32), jnp.uint32)
    word = (u[:, :n] >> 16) | (u[:, n:] & jnp.uint32(0xFFFF0000))
    return lax.bitcast_convert_type(word, I32)


def _unpack_bf16_pairs(w):
    u = lax.bitcast_convert_type(w, jnp.uint32)
    lo = lax.bitcast_convert_type(u << 16, F32)
    hi = lax.bitcast_convert_type(u & jnp.uint32(0xFFFF0000), F32)
    return jnp.concatenate([lo, hi], axis=1)


def _inproj_kernel(x_ref, g_ref, b_ref, wz_ref, wxbc_ref, wdt_ref, wqkv_ref,
                   h_ref, z_ref, xbc_ref, dt_ref, qkv1_ref, qkv4_ref, qkv16_ref, stage_ref):
    tm = x_ref.shape[1]
    h = _ln_rows(x_ref[0], g_ref[...], b_ref[...])
    h_ref[0] = h
    hb = h.astype(BF16)
    w = INPROJ_COLS
    for c in range(SSD_WIDTH // w):
        cols = slice(c * w, (c + 1) * w)
        z_ref[0, :, cols] = _dot(hb, wz_ref[:, cols]).astype(BF16)
    for c in range(CONV_CH // w):
        cols = slice(c * w, (c + 1) * w)
        xbc_ref[0, :, cols] = _dot(hb, wxbc_ref[:, cols]).astype(BF16)
    dt_ref[0] = _dot(hb, wdt_ref[...])
    for c in range(3):
        for part in range(ATT_WIDTH // w):
            lo = part * w
            res = _dot(hb, wqkv_ref[:, c * ATT_WIDTH + lo:c * ATT_WIDTH + lo + w])
            if c == 0:
                res = res * (ATT_HEAD_DIM ** -0.5 * LOG2_E)
            qkv1_ref[c, 0, :, lo:lo + w] = res.astype(BF16)
            for j in range(w // LANES):
                lanes = slice(lo + j * LANES, lo + (j + 1) * LANES)
                stage_ref[j] = res[:, j * LANES:(j + 1) * LANES]
                for r in range(4):
                    qkv4_ref[c, 0, r, :, lanes] = stage_ref[j, pl.ds(r, tm // 4, stride=4), :].astype(BF16)
                for r in range(16):
                    qkv16_ref[c, 0, r, :, lanes] = stage_ref[j, pl.ds(r, tm // 16, stride=16), :].astype(BF16)


def _inproj(x, g, b, wz, wxbc, wdt, wqkv, tm=256):
    bsz, s, d = x.shape
    nt = s // tm
    const = lambda *_: (0, 0)
    wspec = lambda w: pl.BlockSpec(w.shape, const)
    row = lambda n: pl.BlockSpec((1, tm, n), lambda bi, i: (bi, i, 0))
    out_shape = (
        jax.ShapeDtypeStruct((bsz, s, d), F32),
        jax.ShapeDtypeStruct((bsz, s, SSD_WIDTH), BF16),
        jax.ShapeDtypeStruct((bsz, s, CONV_CH), BF16),
        jax.ShapeDtypeStruct((bsz, s, SSD_GROUPS * LANES), F32),
        jax.ShapeDtypeStruct((3, bsz, s, ATT_WIDTH), BF16),
        jax.ShapeDtypeStruct((3, bsz, 4, s // 4, ATT_WIDTH), BF16),
        jax.ShapeDtypeStruct((3, bsz, 16, s // 16, ATT_WIDTH), BF16),
    )
    out_specs = (
        row(d), row(SSD_WIDTH), row(CONV_CH), row(SSD_GROUPS * LANES),
        pl.BlockSpec((3, 1, tm, ATT_WIDTH), lambda bi, i: (0, bi, i, 0)),
        pl.BlockSpec((3, 1, 4, tm // 4, ATT_WIDTH), lambda bi, i: (0, bi, 0, i, 0)),
        pl.BlockSpec((3, 1, 16, tm // 16, ATT_WIDTH), lambda bi, i: (0, bi, 0, i, 0)),
    )
    return pl.pallas_call(
        _inproj_kernel,
        grid=(bsz, nt),
        in_specs=[row(d), wspec(g), wspec(b), wspec(wz), wspec(wxbc), wspec(wdt), wspec(wqkv)],
        out_specs=out_specs,
        out_shape=out_shape,
        scratch_shapes=[pltpu.VMEM((ATT_WIDTH // LANES, tm, LANES), F32)],
        compiler_params=_cparams("parallel", "parallel"),
        name="ln_inproj",
    )(x, g, b, wz, wxbc, wdt, wqkv)


GROUP_X = SSD_HEADS_PER_GROUP * SSD_HEAD_DIM
GROUP_CH = GROUP_X + 2 * SSD_STATE
CONV_ROWS = 256
PAD_ROWS = 8

def _softplus(x):
    return jnp.maximum(x, 0.0) + jnp.log(1.0 + jnp.exp(-jnp.abs(x)))


def _ssd_constants():
    q = SSD_CHUNK
    r = SSD_HEADS_PER_GROUP
    i = np.arange(q)
    tris = [(i[:, None] >= i[None, :]), (i[:, None] <= i[None, :])]
    tri_cat = np.stack([np.concatenate([t, t], axis=1) for t in tris])
    trit_cat = np.stack([np.concatenate([t.T, t.T], axis=0) for t in tris])

    src = np.arange(2 * LANES)[:, None] % LANES
    head = np.arange(GROUP_X)[None, :] // SSD_HEAD_DIM
    sel = np.stack([src == head + off for off in (0, r)])
    as_bf16 = lambda m: jnp.asarray(m.astype(np.float32), dtype=BF16)
    return as_bf16(tri_cat), as_bf16(trit_cat), as_bf16(sel)


def _ssd_kernel(x_ref, b_ref, c_ref, dt_ref, cw_ref, cb_ref, dtb_ref, a_ref, ax_ref, dsk_ref,
                tric_ref, tritc_ref, selx_ref, y_ref, pad_ref, act_ref, yacc_ref, st_ref):
    s = x_ref.shape[1]
    q = SSD_CHUNK
    nchunks = s // q

    zeros = jnp.zeros((PAD_ROWS, GROUP_CH), F32)
    pad_ref[0:PAD_ROWS, :] = zeros
    pad_ref[s + PAD_ROWS:s + 2 * PAD_ROWS, :] = zeros
    pad_ref[PAD_ROWS:s + PAD_ROWS, 0:GROUP_X] = x_ref[0].astype(F32)
    pad_ref[PAD_ROWS:s + PAD_ROWS, GROUP_X:GROUP_X + SSD_STATE] = b_ref[0].astype(F32)
    pad_ref[PAD_ROWS:s + PAD_ROWS, GROUP_X + SSD_STATE:GROUP_CH] = c_ref[0].astype(F32)
    for i in range(s // CONV_ROWS):
        acc = jnp.broadcast_to(cb_ref[0], (CONV_ROWS, GROUP_CH))
        for k in range(SSD_CONV):
            r0 = PAD_ROWS + i * CONV_ROWS + k - SSD_CONV // 2
            acc = acc + cw_ref[0, k:k + 1, :] * pad_ref[r0:r0 + CONV_ROWS, :]
        act_ref[i * CONV_ROWS:(i + 1) * CONV_ROWS, :] = _silu(acc)

    row_i = lax.broadcasted_iota(I32, (q, q), 0)
    col_i = lax.broadcasted_iota(I32, (q, q), 1)
    lane = lax.broadcasted_iota(I32, (q, GROUP_X), 1)
    low_half = lax.broadcasted_iota(I32, (q, LANES), 1) < SSD_HEAD_DIM

    def split(v):
        hi = v.astype(BF16)
        return hi, (v - hi.astype(F32)).astype(BF16)

    st_ref[...] = jnp.zeros_like(st_ref)

    def chunk(ci, reverse):
        di = 1 if reverse else 0
        off = SSD_HEADS_PER_GROUP if reverse else 0
        tri = (row_i <= col_i) if reverse else (row_i >= col_i)
        k = dict(tri_cat=tric_ref[di], trit_cat=tritc_ref[di], sel_x=selx_ref[di])
        c = (nchunks - 1 - ci) if reverse else ci
        rows = pl.ds(c * q, q)
        xa = act_ref[rows, 0:GROUP_X]
        bm = act_ref[rows, GROUP_X:GROUP_X + SSD_STATE]
        cm = act_ref[rows, GROUP_X + SSD_STATE:GROUP_CH]
        dt = _softplus(dt_ref[0, rows, :] + dtb_ref[0])
        dcat = jnp.concatenate(split(dt), axis=1)
        dt_x = _dot(dcat, k["sel_x"])
        cs_x = _dot(k["tri_cat"], jnp.concatenate(split(dt_x * ax_ref[0, di]), axis=0))
        cs_w = []
        for pair in range(GROUP_X // LANES):
            blk = cs_x[:, pair * LANES:(pair + 1) * LANES]
            swapped = pltpu.roll(blk, SSD_HEAD_DIM, axis=1)
            cs_w += [jnp.where(low_half, blk, swapped), jnp.where(low_half, swapped, blk)]
        dta_t = (dt * a_ref[0]).T[off:off + 8, :]
        cs_t = _dot(jnp.concatenate(split(dta_t), axis=1), k["trit_cat"])
        tot = cs_x[0:1, :] if reverse else cs_x[q - 1:q, :]
        cm_b = cm.astype(BF16)
        cbm = _dot_nt(cm_b, bm.astype(BF16))
        xdt = xa * dt_x
        xdt_b = xdt.astype(BF16)
        ydiag = None
        for r in range(SSD_HEADS_PER_GROUP):
            seg = cs_w[r] - cs_t[r:r + 1, :]
            w = (jnp.exp(jnp.where(tri, seg, NEG_INF)) * cbm).astype(BF16)
            yr = _dot(w, xdt_b)
            ydiag = yr if r == 0 else jnp.where(lane >= r * SSD_HEAD_DIM, yr, ydiag)
        xs_b = (xdt * jnp.exp(tot - cs_x)).astype(BF16)
        states = _dot(bm.T.astype(BF16), xs_b)
        hprev = st_ref[di]
        yoff = _dot(cm_b, hprev.astype(BF16)) * jnp.exp(cs_x)
        st_ref[di] = hprev * jnp.exp(tot) + states
        yacc_ref[di, rows, :] = ydiag + yoff

    for ci in range(nchunks):
        chunk(ci, False)
        chunk(ci, True)
    y_ref[0] = (yacc_ref[0] + yacc_ref[1] + act_ref[:, 0:GROUP_X] * dsk_ref[0]).astype(BF16)


def _ssd(xbc, dt, conv_w_g, conv_b_g, dt_bias_g, a_g, dskip_g):
    bsz, s, _ = xbc.shape
    a_lanes, a_x = a_g
    grp = lambda n: pl.BlockSpec((1,) + n, lambda bi, gi: (gi,) + (0,) * len(n))
    in_specs = [
        pl.BlockSpec((1, s, GROUP_X), lambda bi, gi: (bi, 0, gi)),
        pl.BlockSpec((1, s, SSD_STATE), lambda bi, gi: (bi, 0, SSD_WIDTH // SSD_STATE + gi)),
        pl.BlockSpec((1, s, SSD_STATE), lambda bi, gi: (bi, 0, (SSD_WIDTH + GN) // SSD_STATE + gi)),
        pl.BlockSpec((1, s, LANES), lambda bi, gi: (bi, 0, gi)),
        grp((SSD_CONV, GROUP_CH)), grp((1, GROUP_CH)), grp((1, LANES)), grp((1, LANES)),
        grp((2, 1, GROUP_X)), grp((1, GROUP_X)),
    ]
    consts = _ssd_constants()
    in_specs += [pl.BlockSpec(c.shape, lambda bi, gi: (0, 0, 0)) for c in consts]
    return pl.pallas_call(
        _ssd_kernel,
        grid=(bsz, SSD_GROUPS),
        in_specs=in_specs,
        out_specs=pl.BlockSpec((1, s, GROUP_X), lambda bi, gi: (bi, 0, gi)),
        out_shape=jax.ShapeDtypeStruct((bsz, s, SSD_WIDTH), BF16),
        scratch_shapes=[
            pltpu.VMEM((s + 2 * PAD_ROWS, GROUP_CH), F32),
            pltpu.VMEM((s, GROUP_CH), F32),
            pltpu.VMEM((2, s, GROUP_X), F32),
            pltpu.VMEM((2, SSD_STATE, GROUP_X), F32),
        ],
        compiler_params=_cparams("parallel", "parallel"),
        name="ssd_scan",
    )(xbc, xbc, xbc, dt, conv_w_g, conv_b_g, dt_bias_g, a_lanes, a_x, dskip_g, *consts)


ATT_QBLOCK = 128
ATT_PAIR = 2 * ATT_HEAD_DIM
ATT_WINDOW = 2 * ATT_QBLOCK


def _att_kernel(n_cast, q1, k1, v1, q4, k4, v4, q16, k16, v16, slope_ref, *rest):
    cast_in, (o_ref, *cast_out), (km_ref, vm_ref, bias_ref, ob_ref, lb_ref) = (
        rest[:n_cast], rest[n_cast:2 * n_cast + 1], rest[2 * n_cast + 1:])
    for src, dst in zip(cast_in, cast_out):
        dst[...] = src[...].astype(BF16)
    s = o_ref.shape[1]
    qb = ATT_QBLOCK
    first = lax.broadcasted_iota(I32, (qb, ATT_PAIR), 1) < ATT_HEAD_DIM
    first_all = lax.broadcasted_iota(I32, (s, ATT_PAIR), 1) < ATT_HEAD_DIM
    slope_lanes = slope_ref[0]
    ind = first_all.astype(F32).astype(BF16)

    def branch(bi, dil, q_ref, k_ref, v_ref):
        seg_len = s // dil
        win = min(ATT_WINDOW, seg_len)
        kk = k_ref[0, 0]
        vv = v_ref[0, 0]
        zero = jnp.zeros_like(kk)
        km_ref[0] = jnp.where(first_all, kk, zero)
        km_ref[1] = jnp.where(first_all, zero, kk)
        vm_ref[0] = jnp.concatenate([jnp.where(first_all, vv, zero), ind], axis=1)
        vm_ref[1] = jnp.concatenate([jnp.where(first_all, zero, vv), 1 - ind], axis=1)
        shifts = sorted({min(max(w0 - ATT_RADIUS, 0), seg_len - win) - w0 for w0 in range(0, seg_len, qb)},
                        reverse=True)
        rel0 = lax.broadcasted_iota(I32, (qb, win), 1) - lax.broadcasted_iota(I32, (qb, win), 0)
        for vi, shift in enumerate(shifts):
            rel = jnp.abs(rel0 + shift)
            dist = (dil * rel).astype(F32)
            for j in range(2):
                slope = slope_lanes[:, j * ATT_HEAD_DIM:j * ATT_HEAD_DIM + 1]
                bias_ref[vi, :, j * win:(j + 1) * win] = jnp.where(rel <= ATT_RADIUS, -(slope * LOG2_E) * dist, NEG_INF)

        for blk in range(s // qb):
            qs = blk * qb
            seg, within = divmod(qs, seg_len)
            kw = min(max(within - ATT_RADIUS, 0), seg_len - win)
            ks = seg * seg_len + kw
            vi = shifts.index(kw - within)
            kcat = jnp.concatenate([km_ref[0, ks:ks + win, :], km_ref[1, ks:ks + win, :]], axis=0)
            vcat = jnp.concatenate([vm_ref[0, ks:ks + win, :], vm_ref[1, ks:ks + win, :]], axis=0)
            sc = _dot_nt(q_ref[0, 0, qs:qs + qb, :], kcat) + bias_ref[vi, :, 0:2 * win]
            s0, s1 = sc[:, :win], sc[:, win:]
            m0 = jnp.max(s0, axis=-1, keepdims=True)
            m1 = jnp.max(s1, axis=-1, keepdims=True)
            p0 = jnp.exp2(s0 - m0)
            p1 = jnp.exp2(s1 - m1)
            ol = _dot(jnp.concatenate([p0, p1], axis=1).astype(BF16), vcat)
            l = ol[:, ATT_PAIR:]
            o_blk = ol[:, :ATT_PAIR] / l
            l_blk = jnp.where(first, m0, m1) + jnp.log2(l)
            if dil == 1:
                dst = pl.ds(qs, qb)
            else:
                dst = pl.ds(within * dil + seg, qb, stride=dil)
            ob_ref[bi, dst, :] = o_blk
            lb_ref[bi, dst, :] = l_blk

    branch(0, DILATIONS[0], q1, k1, v1)
    branch(1, DILATIONS[1], q4, k4, v4)
    branch(2, DILATIONS[2], q16, k16, v16)

    rows = 256
    for i in range(s // rows):
        sl = slice(i * rows, (i + 1) * rows)
        l0, l1, l2 = lb_ref[0, sl, :], lb_ref[1, sl, :], lb_ref[2, sl, :]
        m = jnp.maximum(jnp.maximum(l0, l1), l2)
        w0, w1, w2 = jnp.exp2(l0 - m), jnp.exp2(l1 - m), jnp.exp2(l2 - m)
        num = w0 * ob_ref[0, sl, :] + w1 * ob_ref[1, sl, :] + w2 * ob_ref[2, sl, :]
        o_ref[0, sl, :] = (num / (w0 + w1 + w2)).astype(BF16)


def _dilated_attention(qkv1, qkv4, qkv16, slopes, cast=()):
    _, bsz, s, _ = qkv1.shape
    npairs = ATT_WIDTH // ATT_PAIR
    spec = lambda c: pl.BlockSpec((1, 1, s, ATT_PAIR), lambda bi, hp, c=c: (c, bi, 0, hp))
    in_specs = [spec(0), spec(1), spec(2)] * 3 + [pl.BlockSpec((1, 1, ATT_PAIR), lambda bi, hp: (hp, 0, 0))]
    steps = bsz * npairs
    cast_specs = [pl.BlockSpec((w.shape[0] // steps,) + w.shape[1:], lambda bi, hp: (bi * npairs + hp, 0, 0))
                  for w in cast]
    out = pl.pallas_call(
        functools.partial(_att_kernel, len(cast)),
        grid=(bsz, npairs),
        in_specs=in_specs + cast_specs,
        out_specs=[pl.BlockSpec((1, s, ATT_PAIR), lambda bi, hp: (bi, 0, hp))] + cast_specs,
        out_shape=[jax.ShapeDtypeStruct((bsz, s, ATT_WIDTH), BF16)]
        + [jax.ShapeDtypeStruct(w.shape, BF16) for w in cast],
        scratch_shapes=[pltpu.VMEM((2, s, ATT_PAIR), BF16), pltpu.VMEM((2, s, 2 * ATT_PAIR), BF16),
                        pltpu.VMEM((3, ATT_QBLOCK, 2 * ATT_WINDOW), F32),
                        pltpu.VMEM((3, s, ATT_PAIR), F32), pltpu.VMEM((3, s, ATT_PAIR), F32)],
        compiler_params=_cparams("parallel", "parallel"),
        name="dilated_attention",
    )(qkv1, qkv1, qkv1, qkv4, qkv4, qkv4, qkv16, qkv16, qkv16, slopes, *cast)
    return out[0], tuple(out[1:])


def _outproj_kernel(h_ref, yssd_ref, z_ref, yatt_ref, nw_ref, wo_ref, g_ref, b_ref, o_ref):
    y = yssd_ref[...].astype(F32) * _silu(z_ref[...].astype(F32))
    yn = y * lax.rsqrt(jnp.mean(y * y, -1, keepdims=True) + RMS_EPS) * nw_ref[...]
    mix = _dot(yn.astype(BF16), wo_ref[0:SSD_WIDTH, :]) + _dot(yatt_ref[...], wo_ref[SSD_WIDTH:, :])
    o_ref[...] = _ln_rows(DN_ALPHA * h_ref[...] + mix, g_ref[...], b_ref[...])


def _outproj(h, yssd, z, yatt, norm_w, w_out, g, b, tm=512):
    t, d = h.shape
    const = lambda *_: (0, 0)
    wspec = lambda w: pl.BlockSpec(w.shape, const)
    row = lambda n: pl.BlockSpec((tm, n), lambda i: (i, 0))
    return pl.pallas_call(
        _outproj_kernel,
        grid=(t // tm,),
        in_specs=[row(d), row(SSD_WIDTH), row(SSD_WIDTH), row(ATT_WIDTH),
                  wspec(norm_w), wspec(w_out), wspec(g), wspec(b)],
        out_specs=row(d),
        out_shape=jax.ShapeDtypeStruct((t, d), F32),
        compiler_params=_cparams("parallel"),
        name="outproj_ln",
    )(h, yssd, z, yatt, norm_w, w_out, g, b)


def _alibi_slopes():
    sl = np.array([2.0 ** (-8.0 * (h + 1) / ATT_HEADS) for h in range(ATT_HEADS)], dtype=np.float32)
    return jnp.asarray(np.repeat(sl, ATT_HEAD_DIM).reshape(ATT_WIDTH // ATT_PAIR, 1, ATT_PAIR))


def _mixer_params(w_in, conv_w, conv_b, dt_bias_f, dt_bias_b, a_log_f, a_log_b, d_skip):
    r = SSD_HEADS_PER_GROUP
    wz = w_in[:, 0:XBC_OFF].astype(BF16)
    wxbc = w_in[:, XBC_OFF:DT_OFF].astype(BF16)
    wqkv = w_in[:, Q_OFF:IN_COLS].astype(BF16)
    wdt_f = w_in[:, DT_OFF:DT_OFF + SSD_HEADS].reshape(D_MODEL, SSD_GROUPS, r)
    wdt_b = w_in[:, DT_OFF + SSD_HEADS:Q_OFF].reshape(D_MODEL, SSD_GROUPS, r)
    lane_pad = ((0, 0), (0, 0), (0, LANES - 2 * r))
    wdt = jnp.pad(jnp.concatenate([wdt_f, wdt_b], -1), lane_pad).reshape(D_MODEL, SSD_GROUPS * LANES).astype(BF16)

    def per_group_lanes(f, b):
        v = jnp.concatenate([f.reshape(SSD_GROUPS, 1, r), b.reshape(SSD_GROUPS, 1, r)], -1)
        return jnp.pad(v.astype(F32), lane_pad)

    dtb = per_group_lanes(dt_bias_f, dt_bias_b)
    a_f = -jnp.exp(a_log_f.astype(F32))
    a_b = -jnp.exp(a_log_b.astype(F32))

    def per_head_lanes(n):
        v = jnp.stack([a_f.reshape(SSD_GROUPS, r), a_b.reshape(SSD_GROUPS, r)], axis=1)
        return jnp.repeat(v, n, axis=-1).reshape(SSD_GROUPS, 2, 1, r * n)

    a = (per_group_lanes(a_f, a_b), per_head_lanes(SSD_HEAD_DIM))

    def group_cols(w):
        xs = w[..., 0:SSD_WIDTH].reshape(w.shape[:-1] + (SSD_GROUPS, GROUP_X))
        bs = w[..., SSD_WIDTH:SSD_WIDTH + GN].reshape(w.shape[:-1] + (SSD_GROUPS, SSD_STATE))
        cs = w[..., SSD_WIDTH + GN:].reshape(w.shape[:-1] + (SSD_GROUPS, SSD_STATE))
        return jnp.moveaxis(jnp.concatenate([xs, bs, cs], -1), -2, 0)

    cw = group_cols(conv_w.astype(F32))
    cb = group_cols(conv_b.astype(F32)[None])
    dsk = jnp.repeat(d_skip.astype(F32), SSD_HEAD_DIM).reshape(SSD_GROUPS, 1, GROUP_X)
    return wz, wxbc, wdt, wqkv, cw, cb, dtb, a, dsk


def _mixer_layer(x, ln_in_g, ln_in_b, w_in, conv_w, conv_b, dt_bias_f, dt_bias_b, a_log_f, a_log_b,
                 d_skip, ssd_norm_w, w_out_mix, ln1_g, ln1_b, cast=()):
    bsz, s, d = x.shape
    vec = lambda v: v.astype(F32).reshape(1, -1)
    wz, wxbc, wdt, wqkv, cw, cb, dtb, a, dsk = _mixer_params(
        w_in, conv_w, conv_b, dt_bias_f, dt_bias_b, a_log_f, a_log_b, d_skip)
    h, z, xbc, dt, qkv1, qkv4, qkv16 = _inproj(x, vec(ln_in_g), vec(ln_in_b), wz, wxbc, wdt, wqkv)
    yssd = _ssd(xbc, dt, cw, cb, dtb, a, dsk)
    yatt, cast_bf16 = _dilated_attention(qkv1, qkv4.reshape(qkv1.shape), qkv16.reshape(qkv1.shape),
                                         _alibi_slopes(), cast)
    t = bsz * s
    h1 = _outproj(h.reshape(t, d), yssd.reshape(t, SSD_WIDTH), z.reshape(t, SSD_WIDTH),
                  yatt.reshape(t, ATT_WIDTH), vec(ssd_norm_w), w_out_mix.astype(BF16), vec(ln1_g), vec(ln1_b))
    return h1, cast_bf16


def _memkv_kernel(mem_ref, wk_ref, wv_ref, k_ref, v_ref):
    m = mem_ref[0].astype(BF16)
    k_ref[0] = _dot(m, wk_ref[...]).astype(BF16)
    v_ref[0] = _dot(m, wv_ref[...]).astype(BF16)


def _memkv(mem, wk, wv):
    bsz, m, d = mem.shape
    const = lambda *_: (0, 0)
    blk = pl.BlockSpec((1, m, d), lambda bi: (bi, 0, 0))
    return pl.pallas_call(
        _memkv_kernel,
        grid=(bsz,),
        in_specs=[blk, pl.BlockSpec(wk.shape, const), pl.BlockSpec(wv.shape, const)],
        out_specs=(blk, blk),
        out_shape=(jax.ShapeDtypeStruct((bsz, m, d), BF16),) * 2,
        compiler_params=_cparams("parallel"),
        name="mem_kv",
    )(mem, wk, wv)


def _memattn_kernel(h_ref, k_ref, v_ref, wq_ref, wo_ref, g_ref, b_ref, o_ref, ow_ref):
    h = h_ref[0]
    q = (_dot(h.astype(BF16), wq_ref[...]) * (MEM_HEAD_DIM ** -0.5)).astype(BF16)
    xa = None
    for hd in range(MEM_HEADS):
        cols = slice(hd * MEM_HEAD_DIM, (hd + 1) * MEM_HEAD_DIM)
        sc = _dot_nt(q[:, cols], k_ref[0, :, cols])
        p = jnp.exp(sc - jnp.max(sc, axis=-1, keepdims=True))
        o = _dot(p.astype(BF16), v_ref[0, :, cols]) / jnp.sum(p, axis=-1, keepdims=True)
        part = _dot(o.astype(BF16), wo_ref[cols, :])
        xa = part if xa is None else xa + part
    out = _ln_rows(DN_ALPHA * h + xa, g_ref[...], b_ref[...])
    o_ref[0] = out
    ow_ref[0] = _pack_bf16_pairs(out)


def _memattn(h, k, v, wq, wo, g, b, tm=512):
    bsz, s, d = h.shape
    m = k.shape[1]
    const = lambda *_: (0, 0)
    wspec = lambda w: pl.BlockSpec(w.shape, const)
    row = pl.BlockSpec((1, tm, d), lambda bi, i: (bi, i, 0))
    kv = pl.BlockSpec((1, m, d), lambda bi, i: (bi, 0, 0))
    return pl.pallas_call(
        _memattn_kernel,
        grid=(bsz, s // tm),
        in_specs=[row, kv, kv, wspec(wq), wspec(wo), wspec(g), wspec(b)],
        out_specs=(row, pl.BlockSpec((1, tm, d // 2), lambda bi, i: (bi, i, 0))),
        out_shape=(jax.ShapeDtypeStruct((bsz, s, d), F32), jax.ShapeDtypeStruct((bsz, s, d // 2), I32)),
        compiler_params=_cparams("parallel", "parallel"),
        name="mem_attn_ln",
    )(h, k, v, wq, wo, g, b)


def _first_max(work, idx_iota, sentinel):
    m = jnp.max(work, axis=0, keepdims=True)
    idx = jnp.min(jnp.where(work == m, idx_iota, sentinel), axis=0, keepdims=True)
    return m, idx


def _router_kernel(h_ref, wrt_ref, bias_ref, eidx_ref, gate_ref, rank_ref, cnt_ref, run_ref):
    tm = h_ref.shape[0]
    e = N_EXPERTS

    @pl.when(pl.program_id(0) == 0)
    def _():
        run_ref[...] = jnp.zeros_like(run_ref)

    logits = lax.dot_general(wrt_ref[...], h_ref[...], (((1,), (1,)), ((), ())),
                             precision=lax.Precision.HIGHEST, preferred_element_type=F32)
    scores = jax.nn.sigmoid(logits)
    choice = scores + bias_ref[...]
    row = lax.broadcasted_iota(I32, (e, tm), 0)
    grow = lax.broadcasted_iota(I32, (GROUP_SIZE, tm), 0)
    gs = []
    for g in range(ROUTE_GROUPS):
        blk = choice[g * GROUP_SIZE:(g + 1) * GROUP_SIZE, :]
        m1, i1 = _first_max(blk, grow, GROUP_SIZE)
        m2 = jnp.max(jnp.where(grow == i1, -jnp.inf, blk), axis=0, keepdims=True)
        gs.append(m1 + m2)
    work = jnp.concatenate(gs, axis=0)
    giota = lax.broadcasted_iota(I32, (ROUTE_GROUPS, tm), 0)
    gmask = jnp.zeros((ROUTE_GROUPS, tm), jnp.bool_)
    for _ in range(TOPK_GROUPS):
        _, gi = _first_max(work, giota, ROUTE_GROUPS)
        hit = giota == gi
        gmask = jnp.logical_or(gmask, hit)
        work = jnp.where(hit, -jnp.inf, work)
    work = jnp.concatenate(
        [jnp.where(gmask[g:g + 1, :], choice[g * GROUP_SIZE:(g + 1) * GROUP_SIZE, :], NEG_INF)
         for g in range(ROUTE_GROUPS)], axis=0)
    chosen = jnp.zeros((e, tm), jnp.bool_)
    idxs, svals = [], []
    for _ in range(TOP_K):
        _, ei = _first_max(work, row, e)
        hit = row == ei
        svals.append(jnp.sum(jnp.where(hit, scores, 0.0), axis=0, keepdims=True))
        idxs.append(ei)
        chosen = jnp.logical_or(chosen, hit)
        work = jnp.where(hit, -jnp.inf, work)
    eidx = jnp.concatenate(idxs, axis=0)
    sv = jnp.concatenate(svals, axis=0)
    eidx_ref[...] = eidx
    gate_ref[...] = sv / jnp.sum(sv, axis=0, keepdims=True) * ROUTED_SCALE
    upper = (lax.broadcasted_iota(I32, (tm, tm), 0) < lax.broadcasted_iota(I32, (tm, tm), 1))
    mt = chosen.astype(F32)
    cum = _dot(mt.astype(BF16), upper.astype(F32).astype(BF16)) + run_ref[...]
    rank_ref[...] = jnp.concatenate(
        [jnp.sum(jnp.where(row == idxs[k], cum, 0.0), axis=0, keepdims=True) for k in range(TOP_K)],
        axis=0).astype(I32)
    run_ref[...] += jnp.sum(mt, axis=1, keepdims=True)
    cnt_ref[...] = run_ref[...]


def _router(h, w_router_t, bias_col, row0, t, tm=512):
    d = h.shape[1]
    const = lambda *_: (0, 0)
    col = lambda dt: pl.BlockSpec((TOP_K, tm), lambda i: (0, i))
    return pl.pallas_call(
        _router_kernel,
        grid=(t // tm,),
        in_specs=[pl.BlockSpec((tm, d), lambda i: (i + row0 // tm, 0)), pl.BlockSpec(w_router_t.shape, const),
                  pl.BlockSpec(bias_col.shape, const)],
        out_specs=(col(I32), col(F32), col(I32), pl.BlockSpec((N_EXPERTS, 1), const)),
        out_shape=(jax.ShapeDtypeStruct((TOP_K, t), I32), jax.ShapeDtypeStruct((TOP_K, t), F32),
                   jax.ShapeDtypeStruct((TOP_K, t), I32), jax.ShapeDtypeStruct((N_EXPERTS, 1), F32)),
        scratch_shapes=[pltpu.VMEM((N_EXPERTS, 1), F32)],
        compiler_params=_cparams("arbitrary"),
        name="router_topk",
    )(h, w_router_t, bias_col)


def _dispatch_kernel(cnt_ref, eidx_ref, rank_ref, dest_ref, bstart_ref, nb_ref):
    tm = eidx_ref.shape[1]
    e = N_EXPERTS
    nb = jnp.floor((cnt_ref[...] + (MOE_BLOCK - 1)) * (1.0 / MOE_BLOCK))
    strict = (lax.broadcasted_iota(I32, (e, e), 1) < lax.broadcasted_iota(I32, (e, e), 0))
    bstart = _dot(strict.astype(F32).astype(BF16), jnp.broadcast_to(nb, (e, LANES)).astype(BF16))[:, 0:1]
    pstart = bstart * MOE_BLOCK
    row = lax.broadcasted_iota(I32, (e, tm), 0)
    eidx = eidx_ref[...]
    rank = rank_ref[...]
    dest_ref[...] = jnp.concatenate(
        [jnp.sum(jnp.where(row == eidx[k:k + 1, :], pstart, 0.0), axis=0, keepdims=True).astype(I32)
         + rank[k:k + 1, :] for k in range(TOP_K)], axis=0)
    bstart_ref[...] = bstart.astype(I32)
    nb_ref[...] = nb.astype(I32)


def _dispatch(cnt, eidx, rank, tm=2048):
    t = eidx.shape[1]
    const = lambda *_: (0, 0)
    col = pl.BlockSpec((TOP_K, tm), lambda i: (0, i))
    per_expert = pl.BlockSpec((N_EXPERTS, 1), const)
    return pl.pallas_call(
        _dispatch_kernel,
        grid=(t // tm,),
        in_specs=[per_expert, col, col],
        out_specs=(col, per_expert, per_expert),
        out_shape=(jax.ShapeDtypeStruct((TOP_K, t), I32), jax.ShapeDtypeStruct((N_EXPERTS, 1), I32),
                   jax.ShapeDtypeStruct((N_EXPERTS, 1), I32)),
        compiler_params=_cparams("arbitrary"),
        name="dispatch_slots",
    )(cnt, eidx, rank)


EXPERT_ROW_BUFFERS = 4


def _experts_kernel(bstart_ref, nb_ref, xs_hbm, wg_ref, wu_ref, wd_ref, ys_hbm,
                    xbuf, ybuf, in_sem, out_sem):
    e = pl.program_id(0)
    last_e = pl.num_programs(0) - 1
    first_block = bstart_ref[e]
    nused = bstart_ref[last_e] + nb_ref[last_e]
    nbuf = xbuf.shape[0]

    def block_rows(blk):
        return pl.ds(pl.multiple_of(blk * MOE_BLOCK, MOE_BLOCK), MOE_BLOCK)

    def in_copy(blk, slot):
        return pltpu.make_async_copy(xs_hbm.at[block_rows(blk)], xbuf.at[slot], in_sem.at[slot])

    def out_copy(blk, slot):
        return pltpu.make_async_copy(ybuf.at[slot], ys_hbm.at[block_rows(blk)], out_sem.at[slot])

    @pl.when(e == 0)
    def _():
        for ahead in range(nbuf - 1):
            @pl.when(ahead < nused)
            def _():
                in_copy(ahead, ahead).start()

    def body(j, carry):
        blk = first_block + j
        slot = blk % nbuf
        in_copy(blk, slot).wait()
        nxt = blk + nbuf - 1

        @pl.when(nxt < nused)
        def _():
            in_copy(nxt, nxt % nbuf).start()

        @pl.when(blk >= nbuf)
        def _():
            out_copy(blk - nbuf, slot).wait()

        x = _unpack_bf16_pairs(xbuf[slot]).astype(BF16)
        g = _dot(x, wg_ref[0])
        u = _dot(x, wu_ref[0])
        a = (_silu(g) * u).astype(BF16)
        ybuf[slot] = _pack_bf16_pairs(_dot(a, wd_ref[0]))
        out_copy(blk, slot).start()
        return carry

    lax.fori_loop(0, nb_ref[e], body, 0)

    @pl.when(e == last_e)
    def _():
        for back in range(1, nbuf + 1):
            @pl.when(nused >= back)
            def _():
                out_copy(nused - back, (nused - back) % nbuf).wait()


def _experts(bstart, nb, xs, w_gate, w_up, w_down):
    n_pad, words = xs.shape
    n_exp, d, f = w_gate.shape
    wspec = lambda w: pl.BlockSpec((1,) + w.shape[1:], lambda e, bs, nb: (e, 0, 0))
    hbm = pl.BlockSpec(memory_space=pl.ANY)
    return pl.pallas_call(
        _experts_kernel,
        grid_spec=pltpu.PrefetchScalarGridSpec(
            num_scalar_prefetch=2,
            grid=(n_exp,),
            in_specs=[hbm, wspec(w_gate), wspec(w_up), wspec(w_down)],
            out_specs=hbm,
            scratch_shapes=[
                pltpu.VMEM((EXPERT_ROW_BUFFERS, MOE_BLOCK, words), I32),
                pltpu.VMEM((EXPERT_ROW_BUFFERS, MOE_BLOCK, words), I32),
                pltpu.SemaphoreType.DMA((EXPERT_ROW_BUFFERS,)), pltpu.SemaphoreType.DMA((EXPERT_ROW_BUFFERS,)),
            ],
        ),
        out_shape=jax.ShapeDtypeStruct((n_pad, words), I32),
        compiler_params=_cparams("arbitrary"),
        name="routed_experts",
    )(bstart, nb, xs, w_gate, w_up, w_down)


def _final_kernel(h_ref, yg_ref, gate_ref, wsg_ref, wsu_ref, wsd_ref, g_ref, b_ref, o_ref):
    h = h_ref[...]
    d = h.shape[1]
    hb = h.astype(BF16)
    ff = _dot((_silu(_dot(hb, wsg_ref[...])) * _dot(hb, wsu_ref[...])).astype(BF16), wsd_ref[...])
    gate = gate_ref[...]
    for k in range(TOP_K):
        ff = ff + gate[:, k:k + 1] * _unpack_bf16_pairs(yg_ref[k])
    o_ref[...] = _ln_rows(DN_ALPHA * h + ff, g_ref[...], b_ref[...])


def _final_into_kernel(prev_ref, *refs):
    del prev_ref
    _final_kernel(*refs)


def _final(h, yg, gate, wsg, wsu, wsd, g, b, row0, out_prev=None, tm=256):
    t_all, d = h.shape
    t = yg.shape[1]
    const = lambda *_: (0, 0)
    wspec = lambda w: pl.BlockSpec(w.shape, const)
    part = lambda n: pl.BlockSpec((tm, n), lambda i: (i, 0))
    full = pl.BlockSpec((tm, d), lambda i: (i + row0 // tm, 0))
    in_specs = [full, pl.BlockSpec((TOP_K, tm, d // 2), lambda i: (0, i, 0)), part(TOP_K),
                wspec(wsg), wspec(wsu), wspec(wsd), wspec(g), wspec(b)]
    args = (h, yg, gate, wsg, wsu, wsd, g, b)
    body, aliases = _final_kernel, {}
    if out_prev is not None:
        body, aliases = _final_into_kernel, {0: 0}
        in_specs = [pl.BlockSpec(memory_space=pl.ANY)] + in_specs
        args = (out_prev,) + args
    return pl.pallas_call(
        body,
        grid=(t // tm,),
        in_specs=in_specs,
        out_specs=full,
        out_shape=jax.ShapeDtypeStruct((t_all, d), F32),
        input_output_aliases=aliases,
        compiler_params=_cparams("parallel"),
        name="shared_combine_ln",
    )(*args)


SC_CORES = 2
SC_SUBCORES = 16
SC_WORKERS = SC_CORES * SC_SUBCORES
SC_WINDOW = 64


def _sc_scatter_rows(x_words, dest_kmajor, n_rows, row0=0):
    w = x_words.shape[1]
    t = dest_kmajor.shape[0] // TOP_K
    per_worker = t // SC_WORKERS
    nchunk = per_worker // SC_WINDOW
    mesh = plsc.VectorSubcoreMesh(core_axis_name="c", subcore_axis_name="s")

    @functools.partial(
        pl.kernel, mesh=mesh,
        out_type=jax.ShapeDtypeStruct((n_rows, w), I32),
        scratch_types=[pltpu.VMEM((SC_WINDOW,), I32), pltpu.VMEM((SC_WINDOW, w), I32), pltpu.SemaphoreType.DMA],
    )
    def scatter(x_hbm, dest_hbm, out_hbm, idx_v, rows_v, sem):
        wid = lax.axis_index("s") * SC_CORES + lax.axis_index("c")
        base = wid * per_worker

        @pl.loop(0, nchunk)
        def _(i):
            t0 = base + i * SC_WINDOW
            pltpu.sync_copy(x_hbm.at[pl.ds(row0 + t0, SC_WINDOW)], rows_v)
            for k in range(TOP_K):
                pltpu.sync_copy(dest_hbm.at[pl.ds(k * t + t0, SC_WINDOW)], idx_v)
                pltpu.async_copy(rows_v, out_hbm.at[idx_v], sem).wait()

    return scatter(x_words, dest_kmajor)


def _sc_gather_rows(table, idx):
    a = idx.shape[0]
    w = table.shape[1]
    per_worker = a // SC_WORKERS
    nchunk = per_worker // SC_WINDOW
    mesh = plsc.VectorSubcoreMesh(core_axis_name="c", subcore_axis_name="s")

    @functools.partial(
        pl.kernel, mesh=mesh,
        out_type=jax.ShapeDtypeStruct((a, w), I32),
        scratch_types=[pltpu.VMEM((SC_WINDOW,), I32), pltpu.VMEM((SC_WINDOW, w), I32), pltpu.SemaphoreType.DMA],
    )
    def gather(table_hbm, idx_hbm, out_hbm, idx_v, rows_v, sem):
        wid = lax.axis_index("s") * SC_CORES + lax.axis_index("c")
        base = wid * per_worker

        @pl.loop(0, nchunk)
        def _(i):
            off = base + i * SC_WINDOW
            pltpu.sync_copy(idx_hbm.at[pl.ds(off, SC_WINDOW)], idx_v)
            pltpu.async_copy(table_hbm.at[idx_v], rows_v, sem).wait()
            pltpu.sync_copy(rows_v, out_hbm.at[pl.ds(off, SC_WINDOW)])

    return gather(table, idx)


def _moe_layer(h2, h2_words, w_router, router_bias, w_gate, w_up, w_down, ws_gate, ws_up, ws_down, ln3_g, ln3_b):
    t_all, d = h2.shape
    vec = lambda v: v.astype(F32).reshape(1, -1)
    w_router_t = w_router.astype(F32).T
    bias_col = router_bias.astype(F32).reshape(-1, 1)
    shared = (ws_gate.astype(BF16), ws_up.astype(BF16), ws_down.astype(BF16), vec(ln3_g), vec(ln3_b))
    t = t_all // MOE_TOKEN_PARTS
    n_blocks = t * TOP_K // MOE_BLOCK + N_EXPERTS
    out = None
    for part in range(MOE_TOKEN_PARTS):
        row0 = part * t
        eidx, gate, rank, cnt = _router(h2, w_router_t, bias_col, row0, t)
        dest, bstart, nb = _dispatch(cnt, eidx, rank)
        dest_flat = dest.reshape(TOP_K * t)
        xs = _sc_scatter_rows(h2_words, dest_flat, n_blocks * MOE_BLOCK, row0)
        ys = _experts(bstart.reshape(N_EXPERTS), nb.reshape(N_EXPERTS), xs, w_gate, w_up, w_down)
        yg = _sc_gather_rows(ys, dest_flat).reshape(TOP_K, t, d // 2)
        out = _final(h2, yg, gate.T, *shared, row0, out)
    return out


def kernel(x, mem, ln_in_g, ln_in_b, w_in, conv_w, conv_b, dt_bias_f, dt_bias_b, a_log_f, a_log_b, d_skip, ssd_norm_w, w_out_mix, ln1_g, ln1_b, wq_mem, wk_mem, wv_mem, wo_mem, ln2_g, ln2_b, w_router, router_bias, w_gate, w_up, w_down, ws_gate, ws_up, ws_down, ln3_g, ln3_b):
    l = 0
    h1, (wg_bf16, wu_bf16, wd_bf16) = _mixer_layer(
        x, ln_in_g, ln_in_b, w_in[l], conv_w[l], conv_b[l], dt_bias_f[l], dt_bias_b[l], a_log_f[l], a_log_b[l],
        d_skip[l], ssd_norm_w[l], w_out_mix[l], ln1_g[l], ln1_b[l], cast=(w_gate[l], w_up[l], w_down[l]))
    bsz, s, d = x.shape
    vec = lambda v: v.astype(F32).reshape(1, -1)
    km, vm = _memkv(mem, wk_mem[l].astype(BF16), wv_mem[l].astype(BF16))
    h2, h2_words = _memattn(h1.reshape(bsz, s, d), km, vm, wq_mem[l].astype(BF16), wo_mem[l].astype(BF16),
                            vec(ln2_g[l]), vec(ln2_b[l]))
    out = _moe_layer(h2.reshape(bsz * s, d), h2_words.reshape(bsz * s, d // 2), w_router[l], router_bias[l],
                     wg_bf16, wu_bf16, wd_bf16, ws_gate[l], ws_up[l], ws_down[l], ln3_g[l], ln3_b[l])
    return out.reshape(x.shape)
```
